```python
import jax, jax.numpy as jnp
from jax import lax
import numpy as np

D_MODEL = 2048
BATCH = 2
SEQ = 4096
DEPTH = 2

CHUNK = 64
HEAD_DIM = 128
N_MIXERS = 4
HEADS_PER_MIXER = D_MODEL // (N_MIXERS * HEAD_DIM)
D_GROUP = HEADS_PER_MIXER * HEAD_DIM
D_MIX = N_MIXERS * D_GROUP
DN_CONV = 4
B_PREV_CHUNKS = 8
REL_CLIP = 256
SWA_WINDOW = 128
C_PREV_CHUNKS = SWA_WINDOW // CHUNK
C_KV_HEADS = HEADS_PER_MIXER // 2
SGU_BLOCK = 128
D_FF = ((8 * D_MODEL // 3 + 127) // 128) * 128
FFN_CONV = 3
EPS = 1e-6
IN_SPLITS = ((D_GROUP,) * 4 + (HEADS_PER_MIXER,) * 2 + (D_GROUP,) * 3
             + (D_GROUP, C_KV_HEADS * HEAD_DIM, C_KV_HEADS * HEAD_DIM) + (D_GROUP, D_GROUP))
N_IN = sum(IN_SPLITS)

kernel_name = 'hybrid_parallel_group_streaming_encoder'


def rms_norm(x, g):
    xf = x.astype(jnp.float32)
    y = xf * lax.rsqrt(jnp.mean(xf * xf, axis=-1, keepdims=True) + EPS)
    return (y * g.astype(jnp.float32)).astype(x.dtype)


def layer_norm(x, g):
    xf = x.astype(jnp.float32)
    mu = jnp.mean(xf, axis=-1, keepdims=True)
    var = jnp.mean(jnp.square(xf - mu), axis=-1, keepdims=True)
    return ((xf - mu) * lax.rsqrt(var + EPS) * g.astype(jnp.float32)).astype(x.dtype)


def l2_norm(x):
    xf = x.astype(jnp.float32)
    return xf * lax.rsqrt(jnp.sum(xf * xf, axis=-1, keepdims=True) + EPS)


def causal_dwconv(x, w):
    k, ch = w.shape
    return lax.conv_general_dilated(x, w[:, None, :].astype(x.dtype), window_strides=(1,),
                                    padding=[(k - 1, 0)], dimension_numbers=('NWC', 'WIO', 'NWC'),
                                    feature_group_count=ch)


def chunk_band(t, n_prev):
    b, s, h, d = t.shape
    nc = s // CHUNK
    tp = jnp.pad(t.reshape(b, nc, CHUNK, h, d), ((0, 0), (n_prev, 0), (0, 0), (0, 0), (0, 0)))
    idx = jnp.arange(nc)[:, None] + jnp.arange(n_prev + 1)[None, :]
    return tp[:, idx].reshape(b, nc, (n_prev + 1) * CHUNK, h, d)


def band_valid(nc, n_prev):
    chunk_id = jnp.arange(nc)[:, None] - n_prev + jnp.arange(n_prev + 1)[None, :]
    return jnp.repeat(chunk_id >= 0, CHUNK, axis=1)


def chunked_delta_rule(q, k, v, log_alpha, beta):
    f32 = jnp.float32
    b, s, h, dk = q.shape
    dv = v.shape[-1]
    nc = s // CHUNK

    def chunks(t):
        t = t.astype(f32).reshape(b, nc, CHUNK, h, *t.shape[3:])
        return jnp.moveaxis(t, 3, 1)

    q = chunks(q) * (dk ** -0.5)
    k = chunks(k)
    v = chunks(v)
    beta = chunks(beta)
    g = jnp.cumsum(chunks(log_alpha), axis=-1)
    i = jnp.arange(CHUNK)
    causal = i[:, None] >= i[None, :]
    strict = i[:, None] > i[None, :]
    decay = jnp.exp(jnp.where(causal, g[..., :, None] - g[..., None, :], -jnp.inf))
    k_beta = k * beta[..., None]
    a_mat = jnp.where(strict, jnp.einsum('bhncd,bhnsd->bhncs', k_beta, k) * decay, 0.0)
    eye = jnp.broadcast_to(jnp.eye(CHUNK, dtype=f32), a_mat.shape)
    t_inv = lax.linalg.triangular_solve(a_mat, eye, left_side=True, lower=True, unit_diagonal=True)
    u = t_inv @ (v * beta[..., None])
    w = t_inv @ (k_beta * jnp.exp(g)[..., None])
    attn = jnp.einsum('bhncd,bhnsd->bhncs', q, k) * decay
    q_dec = q * jnp.exp(g)[..., None]
    g_last = g[..., -1]
    k_dec = k * jnp.exp(g_last[..., None] - g)[..., None]

    def step(state, xs):
        u_c, w_c, q_c, k_c, a_c, gl_c = xs
        v_new = u_c - jnp.einsum('bhcd,bhde->bhce', w_c, state)
        o = jnp.einsum('bhcd,bhde->bhce', q_c, state) + jnp.einsum('bhcs,bhse->bhce', a_c, v_new)
        state = state * jnp.exp(gl_c)[..., None, None] + jnp.einsum('bhcd,bhce->bhde', k_c, v_new)
        return state, o

    xs = (jnp.moveaxis(u, 2, 0), jnp.moveaxis(w, 2, 0), jnp.moveaxis(q_dec, 2, 0),
          jnp.moveaxis(k_dec, 2, 0), jnp.moveaxis(attn, 2, 0), jnp.moveaxis(g_last, 2, 0))
    _, o = lax.scan(step, jnp.zeros((b, h, dk, dv), f32), xs)
    return jnp.transpose(o, (1, 0, 3, 2, 4)).reshape(b, s, h, dv)


def gated_deltanet(q, k, v, gate, beta_logit, a_logit, conv_w, a_log, dt_bias, norm_g):
    b, s, _ = q.shape
    h = HEADS_PER_MIXER
    qkv = jax.nn.silu(causal_dwconv(jnp.concatenate([q, k, v], axis=-1), conv_w))
    q, k, v = jnp.split(qkv, 3, axis=-1)
    q = l2_norm(q.reshape(b, s, h, HEAD_DIM))
    k = l2_norm(k.reshape(b, s, h, HEAD_DIM))
    v = v.reshape(b, s, h, HEAD_DIM)
    beta = jax.nn.sigmoid(beta_logit.astype(jnp.float32))
    log_alpha = -jnp.exp(a_log.astype(jnp.float32)) * jax.nn.softplus(
        a_logit.astype(jnp.float32) + dt_bias.astype(jnp.float32))
    o = chunked_delta_rule(q, k, v, log_alpha, beta)
    o = rms_norm(o, norm_g) * jax.nn.silu(gate.astype(jnp.float32).reshape(b, s, h, HEAD_DIM))
    return o.reshape(b, s, h * HEAD_DIM).astype(gate.dtype)


def chunked_relbias_attention(q, k, v, rel_bias):
    b, s, h, d = q.shape
    nc = s // CHUNK
    band = (B_PREV_CHUNKS + 1) * CHUNK
    qc = q.reshape(b, nc, CHUNK, h, d)
    kb = chunk_band(k, B_PREV_CHUNKS)
    vb = chunk_band(v, B_PREV_CHUNKS)
    dist = B_PREV_CHUNKS * CHUNK + jnp.arange(CHUNK)[:, None] - jnp.arange(band)[None, :]
    bias = rel_bias[:, jnp.clip(dist, -REL_CLIP, REL_CLIP) + REL_CLIP].astype(jnp.float32)
    scores = jnp.einsum('bnqhd,bnkhd->bhnqk', qc, kb).astype(jnp.float32) * (d ** -0.5) + bias[:, None]
    scores = jnp.where(band_valid(nc, B_PREV_CHUNKS)[:, None, :], scores, -jnp.inf)
    p = jax.nn.softmax(scores, axis=-1).astype(v.dtype)
    o = jnp.einsum('bhnqk,bnkhd->bnqhd', p, vb)
    return o.reshape(b, s, h * d)


def swa_sink_attention(q, k, v, sinks, slopes):
    b, s, hq, d = q.shape
    hk = k.shape[2]
    g = hq // hk
    nc = s // CHUNK
    band = (C_PREV_CHUNKS + 1) * CHUNK
    qc = q.reshape(b, nc, CHUNK, hk, g, d)
    kb = chunk_band(k, C_PREV_CHUNKS)
    vb = chunk_band(v, C_PREV_CHUNKS)
    dist = jnp.abs(C_PREV_CHUNKS * CHUNK + jnp.arange(CHUNK)[:, None]
                   - jnp.arange(band)[None, :]).astype(jnp.float32)
    alibi = -slopes.reshape(hk, g)[:, :, None, None] * dist
    scores = jnp.einsum('bnqkgd,bnskd->bkgnqs', qc, kb).astype(jnp.float32) * (d ** -0.5) + alibi[:, :, None]
    scores = jnp.where(band_valid(nc, C_PREV_CHUNKS)[:, None, :], scores, -jnp.inf)
    sink = sinks.astype(jnp.float32).reshape(hk, g)[:, :, None, None, None]
    m = jnp.maximum(jnp.max(scores, axis=-1, keepdims=True), sink)
    p = jnp.exp(scores - m)
    p = (p / (jnp.sum(p, axis=-1, keepdims=True) + jnp.exp(sink - m))).astype(v.dtype)
    o = jnp.einsum('bkgnqs,bnskd->bnqkgd', p, vb)
    return o.reshape(b, s, hq * d)


def spatial_gating(u, v, norm_g, w_s, b_s):
    b, s, dg = u.shape
    ng = w_s.shape[0]
    cg = dg // ng
    nb = s // SGU_BLOCK
    u = jax.nn.gelu(u)
    v = layer_norm(jax.nn.gelu(v), norm_g)
    tri = jnp.tril(jnp.ones((SGU_BLOCK, SGU_BLOCK), dtype=bool))
    w = jnp.where(tri, w_s, 0.0).astype(v.dtype)
    vr = v.reshape(b, nb, SGU_BLOCK, ng, cg)
    mixed = jnp.einsum('gts,bnsgc->bntgc', w, vr) + b_s.T.astype(v.dtype)[:, :, None]
    return u * mixed.reshape(b, s, dg)


def token_mixers(h, w_in, dn_conv_w, dn_a_log, dn_dt_bias, dn_norm_g, rel_bias, sinks, slopes,
                 sgu_norm_g, sgu_w, sgu_b, w_out):
    b, s, _ = h.shape
    offsets = [int(o) for o in np.cumsum(IN_SPLITS)[:-1]]
    (a_q, a_k, a_v, a_gate, a_beta, a_alpha, b_q, b_k, b_v,
     c_q, c_k, c_v, d_u, d_v) = jnp.split(h @ w_in, offsets, axis=-1)
    out_a = gated_deltanet(a_q, a_k, a_v, a_gate, a_beta, a_alpha, dn_conv_w, dn_a_log, dn_dt_bias, dn_norm_g)
    out_b = chunked_relbias_attention(b_q.reshape(b, s, HEADS_PER_MIXER, HEAD_DIM),
                                      b_k.reshape(b, s, HEADS_PER_MIXER, HEAD_DIM),
                                      b_v.reshape(b, s, HEADS_PER_MIXER, HEAD_DIM), rel_bias)
    out_c = swa_sink_attention(c_q.reshape(b, s, HEADS_PER_MIXER, HEAD_DIM),
                               c_k.reshape(b, s, C_KV_HEADS, HEAD_DIM),
                               c_v.reshape(b, s, C_KV_HEADS, HEAD_DIM), sinks, slopes)
    out_d = spatial_gating(d_u, d_v, sgu_norm_g, sgu_w, sgu_b)
    return jnp.concatenate([out_a, out_b, out_c, out_d], axis=-1) @ w_out


def conv_glu_ffn(h, w_up, conv_w, conv_b, w_down):
    a, gt = jnp.split(h @ w_up, 2, axis=-1)
    a = causal_dwconv(a, conv_w) + conv_b
    return (jax.nn.gelu(a) * gt) @ w_down


def setup_inputs(seed: int = 0) -> dict:
    key = jax.random.key(seed)
    ks = iter(jax.random.split(key, 32))
    L, D, H = DEPTH, D_MODEL, HEADS_PER_MIXER
    f32 = jnp.float32

    def nrm(shape, std):
        return std * jax.random.normal(next(ks), shape, f32)

    def gain(shape):
        return 1.0 + 0.02 * jax.random.normal(next(ks), shape, f32)

    x = nrm((BATCH, SEQ, D), 1.0)
    c = nrm((BATCH, D), 1.0)
    ada_w = nrm((L, D, 6 * D), D ** -0.5)
    ada_b = nrm((L, 6 * D), 0.01)
    mix_pre_g = gain((L, D))
    mix_post_g = gain((L, D))
    w_in = nrm((L, D, N_IN), D ** -0.5)
    dn_conv_w = nrm((L, DN_CONV, 3 * D_GROUP), DN_CONV ** -0.5)
    dn_a_log = jnp.log(jax.random.uniform(next(ks), (L, H), f32, minval=1.0, maxval=16.0))
    dt = jnp.exp(jax.random.uniform(next(ks), (L, H), f32, minval=float(np.log(1e-3)), maxval=float(np.log(1e-1))))
    dn_dt_bias = dt + jnp.log(-jnp.expm1(-dt))
    dn_norm_g = gain((L, HEAD_DIM))
    rel_bias = nrm((L, H, 2 * REL_CLIP + 1), 0.1)
    sinks = nrm((L, H), 0.5)
    sgu_norm_g = gain((L, D_GROUP))
    sgu_w = nrm((L, H, SGU_BLOCK, SGU_BLOCK), 0.5 * SGU_BLOCK ** -0.5)
    sgu_b = gain((L, H, SGU_BLOCK))
    w_out = nrm((L, D_MIX, D), D_MIX ** -0.5)
    ffn_pre_g = gain((L, D))
    ffn_post_g = gain((L, D))
    ffn_w_up = nrm((L, D, 2 * D_FF), D ** -0.5)
    ffn_conv_w = nrm((L, FFN_CONV, D_FF), FFN_CONV ** -0.5)
    ffn_conv_b = nrm((L, D_FF), 0.01)
    ffn_w_down = nrm((L, D_FF, D), D_FF ** -0.5)
    return {'x': x, 'c': c, 'ada_w': ada_w, 'ada_b': ada_b, 'mix_pre_g': mix_pre_g,
            'mix_post_g': mix_post_g, 'w_in': w_in, 'dn_conv_w': dn_conv_w, 'dn_a_log': dn_a_log,
            'dn_dt_bias': dn_dt_bias, 'dn_norm_g': dn_norm_g, 'rel_bias': rel_bias, 'sinks': sinks,
            'sgu_norm_g': sgu_norm_g, 'sgu_w': sgu_w, 'sgu_b': sgu_b, 'w_out': w_out,
            'ffn_pre_g': ffn_pre_g, 'ffn_post_g': ffn_post_g, 'ffn_w_up': ffn_w_up,
            'ffn_conv_w': ffn_conv_w, 'ffn_conv_b': ffn_conv_b, 'ffn_w_down': ffn_w_down}


def reference(x, c, ada_w, ada_b, mix_pre_g, mix_post_g, w_in, dn_conv_w, dn_a_log, dn_dt_bias,
              dn_norm_g, rel_bias, sinks, sgu_norm_g, sgu_w, sgu_b, w_out, ffn_pre_g, ffn_post_g,
              ffn_w_up, ffn_conv_w, ffn_conv_b, ffn_w_down):
    slopes = jnp.asarray(2.0 ** (-8.0 * np.arange(1, HEADS_PER_MIXER + 1) / HEADS_PER_MIXER), dtype=jnp.float32)
    cond = jax.nn.silu(c)
    for l in range(DEPTH):
        mod = cond @ ada_w[l] + ada_b[l]
        sh_m, sc_m, gt_m, sh_f, sc_f, gt_f = jnp.split(mod[:, None, :], 6, axis=-1)
        h = rms_norm(x, mix_pre_g[l]) * (1.0 + sc_m) + sh_m
        y = token_mixers(h, w_in[l], dn_conv_w[l], dn_a_log[l], dn_dt_bias[l], dn_norm_g[l], rel_bias[l],
                         sinks[l], slopes, sgu_norm_g[l], sgu_w[l], sgu_b[l], w_out[l])
        x = x + gt_m * rms_norm(y, mix_post_g[l])
        h = rms_norm(x, ffn_pre_g[l]) * (1.0 + sc_f) + sh_f
        y = conv_glu_ffn(h, ffn_w_up[l], ffn_conv_w[l], ffn_conv_b[l], ffn_w_down[l])
        x = x + gt_f * rms_norm(y, ffn_post_g[l])
    return x
```

```python
import functools

import numpy as np
import jax
import jax.numpy as jnp
from jax import lax
from jax.experimental import pallas as pl
from jax.experimental.pallas import tpu as pltpu

F32 = jnp.float32
BF16 = jnp.bfloat16

D_MODEL = 2048
CHUNK = 64
HEAD_DIM = 128
HEADS = 4
D_GROUP = HEADS * HEAD_DIM
DN_CONV = 4
B_PREV_CHUNKS = 8
REL_CLIP = 256
C_PREV_CHUNKS = 2
C_KV_HEADS = 2
SGU_BLOCK = 128
D_FF = 5504
FFN_CONV = 3
EPS = 1e-6

LANES = 128
SUBLANES = 8
VMEM_LIMIT_BYTES = 56 * 1024 * 1024

N_SMALL = 2 * HEADS
N_MAIN = 11 * D_GROUP
COL_AQ, COL_AK, COL_AV, COL_AG, COL_BQ, COL_BK, COL_BV, COL_CQ, COL_CKV, COL_DU, COL_DV = range(11)
D_FF_PAD = 11 * D_GROUP

ROWS_ATT = 512
ROWS_PROJ = 1024
ROWS_OUT = 512
FFN_HALO = 16
MOD_COLS = 1024

ALIBI_SLOPES = tuple(float(2.0 ** (-8.0 * h / HEADS)) for h in range(1, HEADS + 1))


def _cparams(*sem):
    return pltpu.CompilerParams(dimension_semantics=sem, vmem_limit_bytes=VMEM_LIMIT_BYTES)


def _sigmoid(x):
    return 1.0 / (1.0 + jnp.exp(-x))


def _silu(x):
    return x * _sigmoid(x)


def _softplus(x):
    return jnp.maximum(x, 0.0) + jnp.log1p(jnp.exp(-jnp.abs(x)))


def _gelu_tanh(x):
    return 0.5 * x * (1.0 + jnp.tanh(float(np.sqrt(2.0 / np.pi)) * (x + 0.044715 * (x * x * x))))


def _rms(x, g):
    return x * lax.rsqrt(jnp.mean(x * x, axis=-1, keepdims=True) + EPS) * g


def _dot(a, b):
    return jnp.dot(a.astype(BF16), b.astype(BF16), preferred_element_type=F32)


def _dot_nt(a, b):
    return lax.dot_general(a.astype(BF16), b.astype(BF16), (((1,), (1,)), ((), ())),
                           preferred_element_type=F32)


def _dot_f32(a, b):
    return jnp.dot(a, b, precision=lax.Precision.HIGHEST, preferred_element_type=F32)


def _mod_kernel(c_ref, w_ref, b_ref, o_ref):
    cond = _silu(c_ref[...])
    o_ref[0] = _dot(cond, w_ref[0]) + b_ref[0]


def _modulation(c_pad, ada_w, ada_b):
    depth, d, n = ada_w.shape
    return pl.pallas_call(
        _mod_kernel,
        grid=(depth, n // MOD_COLS),
        in_specs=[pl.BlockSpec((SUBLANES, d), lambda l, j: (0, 0)),
                  pl.BlockSpec((1, d, MOD_COLS), lambda l, j: (l, 0, j)),
                  pl.BlockSpec((1, 1, MOD_COLS), lambda l, j: (l, 0, j))],
        out_specs=pl.BlockSpec((1, SUBLANES, MOD_COLS), lambda l, j: (l, 0, j)),
        out_shape=jax.ShapeDtypeStruct((depth, SUBLANES, n), F32),
        compiler_params=_cparams("parallel", "parallel"),
        name="modulation",
    )(c_pad, ada_w, ada_b.reshape(depth, 1, n))


def _inproj_kernel(x_ref, mod_ref, g_ref, w_ref, ws_ref, wst_ref, p_ref, s_ref, st_ref, h_scr):
    @pl.when(pl.program_id(1) == 0)
    def _():
        h = _rms(x_ref[...], g_ref[...]) * (1.0 + mod_ref[0, 1:2, :]) + mod_ref[0, 0:1, :]
        hb = h.astype(BF16)
        h_scr[...] = hb
        s_ref[...] = jnp.dot(hb, ws_ref[...], preferred_element_type=F32)
        st_ref[...] = _dot_nt(wst_ref[...], hb)

    p_ref[...] = jnp.dot(h_scr[...], w_ref[...], preferred_element_type=F32)


def _in_projection(x2, mod, pre_g, w_main, w_small, w_small_t, seq):
    t, d = x2.shape
    tiles_per_batch = seq // ROWS_PROJ
    return pl.pallas_call(
        _inproj_kernel,
        grid=(t // ROWS_PROJ, N_MAIN // D_GROUP),
        in_specs=[pl.BlockSpec((ROWS_PROJ, d), lambda m, n: (m, 0)),
                  pl.BlockSpec((1, 6, d), lambda m, n: (m // tiles_per_batch, 0, 0)),
                  pl.BlockSpec((1, d), lambda m, n: (0, 0)),
                  pl.BlockSpec((d, D_GROUP), lambda m, n: (0, n)),
                  pl.BlockSpec((d, LANES), lambda m, n: (0, 0)),
                  pl.BlockSpec((SUBLANES, d), lambda m, n: (0, 0))],
        out_specs=[pl.BlockSpec((ROWS_PROJ, D_GROUP), lambda m, n: (m, n)),
                   pl.BlockSpec((ROWS_PROJ, LANES), lambda m, n: (m, 0)),
                   pl.BlockSpec((SUBLANES, ROWS_PROJ), lambda m, n: (0, m))],
        out_shape=[jax.ShapeDtypeStruct((t, N_MAIN), F32),
                   jax.ShapeDtypeStruct((t, LANES), F32),
                   jax.ShapeDtypeStruct((SUBLANES, t), F32)],
        scratch_shapes=[pltpu.VMEM((ROWS_PROJ, d), BF16)],
        compiler_params=_cparams("parallel", "arbitrary"),
        name="in_projection",
    )(x2, mod, pre_g.reshape(1, d), w_main, w_small, w_small_t)


def _unit_lower_inverse(a, row, col):
    blk_r, blk_c = jnp.right_shift(row, 4), jnp.right_shift(col, 4)
    half_r, half_c = jnp.right_shift(row, 5), jnp.right_shift(col, 5)
    eye = (row == col).astype(F32)
    a_diag = jnp.where(blk_r == blk_c, a, 0.0)
    a_pair = jnp.where(half_r == half_c, a, 0.0) - a_diag
    a_far = jnp.where(half_r != half_c, a, 0.0)
    p2 = _dot_f32(a_diag, a_diag)
    p4 = _dot_f32(p2, p2)
    p8 = _dot_f32(p4, p4)
    t = eye - a_diag
    t = t + _dot_f32(t, p2)
    t = t + _dot_f32(t, p4)
    t = t + _dot_f32(t, p8)
    t = t - _dot_f32(_dot_f32(t, a_pair), t)
    t = t - _dot_f32(_dot_f32(t, a_far), t)
    return t


def _delta_kernel(q_ref, k_ref, v_ref, qp_ref, kp_ref, vp_ref, gate_ref, sm_ref, smt_ref, cw_ref,
                  prow_ref, pcol_ref, ng_ref, o_ref, state_scr, xe_scr):
    c = pl.program_id(1)

    @pl.when(c == 0)
    def _():
        state_scr[...] = jnp.zeros_like(state_scr)

    keep = (c > 0).astype(F32)
    xe_scr[0:SUBLANES, :] = jnp.concatenate([qp_ref[...], kp_ref[...], vp_ref[...]], axis=1) * keep
    xe_scr[SUBLANES:, 0:D_GROUP] = q_ref[...]
    xe_scr[SUBLANES:, D_GROUP:2 * D_GROUP] = k_ref[...]
    xe_scr[SUBLANES:, 2 * D_GROUP:] = v_ref[...]
    acc = None
    for tap in range(DN_CONV):
        off = SUBLANES - (DN_CONV - 1) + tap
        term = cw_ref[tap:tap + 1, :] * xe_scr[off:off + CHUNK, :]
        acc = term if acc is None else acc + term
    qkv = _silu(acc)

    row = lax.broadcasted_iota(jnp.int32, (CHUNK, CHUNK), 0)
    col = lax.broadcasted_iota(jnp.int32, (CHUNK, CHUNK), 1)
    causal = row >= col
    strict = row > col

    sm = sm_ref[...]
    la_cols = -jnp.exp(prow_ref[1:2, :]) * _softplus(sm + prow_ref[0:1, :])
    g_cols = _dot_f32(causal.astype(F32), la_cols)
    la_rows = -jnp.exp(pcol_ref[:, 1:2]) * _softplus(smt_ref[0] + pcol_ref[:, 0:1])
    g_rows = _dot_f32(la_rows, (row <= col).astype(F32))
    beta_cols = _sigmoid(sm)

    gate = gate_ref[...]
    scale = HEAD_DIM ** -0.5
    for h in range(HEADS):
        lo, hi = h * HEAD_DIM, (h + 1) * HEAD_DIM
        q = qkv[:, lo:hi]
        k = qkv[:, D_GROUP + lo:D_GROUP + hi]
        v = qkv[:, 2 * D_GROUP + lo:2 * D_GROUP + hi]
        q = q * lax.rsqrt(jnp.sum(q * q, axis=-1, keepdims=True) + EPS) * scale
        k = k * lax.rsqrt(jnp.sum(k * k, axis=-1, keepdims=True) + EPS)
        beta = beta_cols[:, h:h + 1]
        g_col = g_cols[:, HEADS + h:HEADS + h + 1]
        g_row = g_rows[HEADS + h:HEADS + h + 1, :]
        g_last = g_col[CHUNK - 1:CHUNK, :]
        decay = jnp.exp(jnp.where(causal, g_col - g_row, -jnp.inf))
        k_beta = k * beta
        a_mat = jnp.where(strict, _dot_nt(k_beta, k) * decay, 0.0)
        t_inv = _unit_lower_inverse(a_mat, row, col)
        exp_g = jnp.exp(g_col)
        uw = _dot(t_inv, jnp.concatenate([v * beta, k_beta * exp_g], axis=1))
        u, w = uw[:, :HEAD_DIM], uw[:, HEAD_DIM:]
        attn = _dot_nt(q, k) * decay
        q_dec = q * exp_g
        k_dec = k * jnp.exp(g_last - g_col)

        state = state_scr[h]
        ws_qs = _dot(jnp.concatenate([w, q_dec], axis=0), state)
        v_new = u - ws_qs[:CHUNK]
        o = ws_qs[CHUNK:] + _dot(attn, v_new)
        state_scr[h] = state * jnp.exp(g_last) + _dot(k_dec.T, v_new)

        o = _rms(o, ng_ref[...]) * _silu(gate[:, lo:hi])
        o_ref[:, lo:hi] = o.astype(o_ref.dtype)


def _gated_deltanet(p, small, small_t3, conv_w, prm_row, prm_col, norm_g, batch, seq):
    t = p.shape[0]
    nc = seq // CHUNK
    halo_blocks = CHUNK // SUBLANES

    def cur(j):
        return pl.BlockSpec((CHUNK, D_GROUP), lambda b, c: (b * nc + c, j))

    def prev(j):
        return pl.BlockSpec((SUBLANES, D_GROUP),
                            lambda b, c: (jnp.maximum((b * nc + c) * halo_blocks - 1, 0), j))

    return pl.pallas_call(
        _delta_kernel,
        grid=(batch, nc),
        in_specs=[cur(COL_AQ), cur(COL_AK), cur(COL_AV), prev(COL_AQ), prev(COL_AK), prev(COL_AV),
                  cur(COL_AG),
                  pl.BlockSpec((CHUNK, LANES), lambda b, c: (b * nc + c, 0)),
                  pl.BlockSpec((1, SUBLANES, CHUNK), lambda b, c: (b * nc + c, 0, 0)),
                  pl.BlockSpec((DN_CONV, 3 * D_GROUP), lambda b, c: (0, 0)),
                  pl.BlockSpec((SUBLANES, LANES), lambda b, c: (0, 0)),
                  pl.BlockSpec((SUBLANES, LANES), lambda b, c: (0, 0)),
                  pl.BlockSpec((1, HEAD_DIM), lambda b, c: (0, 0))],
        out_specs=pl.BlockSpec((CHUNK, D_GROUP), lambda b, c: (b * nc + c, 0)),
        out_shape=jax.ShapeDtypeStruct((t, D_GROUP), BF16),
        scratch_shapes=[pltpu.VMEM((HEADS, HEAD_DIM, HEAD_DIM), F32),
                        pltpu.VMEM((CHUNK + SUBLANES, 3 * D_GROUP), F32)],
        compiler_params=_cparams("parallel", "arbitrary"),
        name="gated_deltanet",
    )(p, p, p, p, p, p, p, small, small_t3, conv_w, prm_row, prm_col, norm_g.reshape(1, HEAD_DIM))


def _relbias_kernel(q_ref, kc_ref, kp_ref, vc_ref, vp_ref, bias_ref, o_ref, k_scr, v_scr):
    i = pl.program_id(1)
    k_scr[0:ROWS_ATT, :] = kp_ref[...].astype(BF16)
    k_scr[ROWS_ATT:, :] = kc_ref[...].astype(BF16)
    v_scr[0:ROWS_ATT, :] = vp_ref[...].astype(BF16)
    v_scr[ROWS_ATT:, :] = vc_ref[...].astype(BF16)
    band = (B_PREV_CHUNKS + 1) * CHUNK
    key_pos = lax.broadcasted_iota(jnp.int32, (CHUNK, band), 1)
    first_valid = jnp.where(i > 0, 0, ROWS_ATT)
    scale = HEAD_DIM ** -0.5
    for h in range(HEADS):
        lo, hi = h * HEAD_DIM, (h + 1) * HEAD_DIM
        bias = bias_ref[h]
        for c in range(ROWS_ATT // CHUNK):
            r0 = c * CHUNK
            q = q_ref[r0:r0 + CHUNK, lo:hi]
            s = _dot_nt(q, k_scr[r0:r0 + band, lo:hi]) * scale + bias
            s = jnp.where(key_pos + r0 >= first_valid, s, -jnp.inf)
            m = jnp.max(s, axis=-1, keepdims=True)
            e = jnp.exp(s - m)
            prob = e / jnp.sum(e, axis=-1, keepdims=True)
            o_ref[r0:r0 + CHUNK, lo:hi] = _dot(prob, v_scr[r0:r0 + band, lo:hi]).astype(o_ref.dtype)


def _relbias_attention(p, bias, batch, seq):
    t = p.shape[0]
    nb = seq // ROWS_ATT

    def cur(j):
        return pl.BlockSpec((ROWS_ATT, D_GROUP), lambda b, i: (b * nb + i, j))

    def prev(j):
        return pl.BlockSpec((ROWS_ATT, D_GROUP), lambda b, i: (b * nb + jnp.maximum(i - 1, 0), j))

    band = (B_PREV_CHUNKS + 1) * CHUNK
    return pl.pallas_call(
        _relbias_kernel,
        grid=(batch, nb),
        in_specs=[cur(COL_BQ), cur(COL_BK), prev(COL_BK), cur(COL_BV), prev(COL_BV),
                  pl.BlockSpec((HEADS, CHUNK, band), lambda b, i: (0, 0, 0))],
        out_specs=pl.BlockSpec((ROWS_ATT, D_GROUP), lambda b, i: (b * nb + i, 0)),
        out_shape=jax.ShapeDtypeStruct((t, D_GROUP), BF16),
        scratch_shapes=[pltpu.VMEM((2 * ROWS_ATT, D_GROUP), BF16),
                        pltpu.VMEM((2 * ROWS_ATT, D_GROUP), BF16)],
        compiler_params=_cparams("parallel", "arbitrary"),
        name="relbias_attention",
    )(p, p, p, p, p, bias)


def _swa_kernel(sink_ref, q_ref, kvc_ref, kvp_ref, o_ref, kv_scr):
    i = pl.program_id(1)
    halo = C_PREV_CHUNKS * CHUNK
    kv_scr[0:halo, :] = kvp_ref[...].astype(BF16)
    kv_scr[halo:, :] = kvc_ref[...].astype(BF16)
    band = (C_PREV_CHUNKS + 1) * CHUNK
    q_pos = lax.broadcasted_iota(jnp.int32, (CHUNK, band), 0)
    key_pos = lax.broadcasted_iota(jnp.int32, (CHUNK, band), 1)
    dist = jnp.abs(halo + q_pos - key_pos).astype(F32)
    first_valid = jnp.where(i > 0, 0, halo)
    scale = HEAD_DIM ** -0.5
    group = HEADS // C_KV_HEADS
    kv_width = C_KV_HEADS * HEAD_DIM
    for h in range(HEADS):
        lo, hi = h * HEAD_DIM, (h + 1) * HEAD_DIM
        klo = (h // group) * HEAD_DIM
        alibi = -ALIBI_SLOPES[h] * dist
        sink = sink_ref[h]
        for c in range(ROWS_ATT // CHUNK):
            r0 = c * CHUNK
            q = q_ref[r0:r0 + CHUNK, lo:hi]
            s = _dot_nt(q, kv_scr[r0:r0 + band, klo:klo + HEAD_DIM]) * scale + alibi
            s = jnp.where(key_pos + r0 >= first_valid, s, -jnp.inf)
            m = jnp.maximum(jnp.max(s, axis=-1, keepdims=True), sink)
            e = jnp.exp(s - m)
            prob = e / (jnp.sum(e, axis=-1, keepdims=True) + jnp.exp(sink - m))
            vals = kv_scr[r0:r0 + band, kv_width + klo:kv_width + klo + HEAD_DIM]
            o_ref[r0:r0 + CHUNK, lo:hi] = _dot(prob, vals).astype(o_ref.dtype)


def _swa_attention(p, sinks, batch, seq):
    t = p.shape[0]
    nb = seq // ROWS_ATT
    halo = C_PREV_CHUNKS * CHUNK
    halo_per_tile = ROWS_ATT // halo
    return pl.pallas_call(
        _swa_kernel,
        grid=(batch, nb),
        in_specs=[pl.BlockSpec(memory_space=pltpu.SMEM),
                  pl.BlockSpec((ROWS_ATT, D_GROUP), lambda b, i: (b * nb + i, COL_CQ)),
                  pl.BlockSpec((ROWS_ATT, D_GROUP), lambda b, i: (b * nb + i, COL_CKV)),
                  pl.BlockSpec((halo, D_GROUP),
                               lambda b, i: (jnp.maximum((b * nb + i) * halo_per_tile - 1, 0), COL_CKV))],
        out_specs=pl.BlockSpec((ROWS_ATT, D_GROUP), lambda b, i: (b * nb + i, 0)),
        out_shape=jax.ShapeDtypeStruct((t, D_GROUP), BF16),
        scratch_shapes=[pltpu.VMEM((halo + ROWS_ATT, D_GROUP), BF16)],
        compiler_params=_cparams("parallel", "arbitrary"),
        name="swa_attention",
    )(sinks, p, p, p)


def _sgu_kernel(u_ref, v_ref, g_ref, w_ref, b_ref, o_ref):
    u = _gelu_tanh(u_ref[...])
    v = _gelu_tanh(v_ref[...])
    mu = jnp.mean(v, axis=-1, keepdims=True)
    var = jnp.mean(jnp.square(v - mu), axis=-1, keepdims=True)
    v = ((v - mu) * lax.rsqrt(var + EPS) * g_ref[...]).astype(BF16)
    row = lax.broadcasted_iota(jnp.int32, (SGU_BLOCK, SGU_BLOCK), 0)
    col = lax.broadcasted_iota(jnp.int32, (SGU_BLOCK, SGU_BLOCK), 1)
    for g in range(HEADS):
        lo, hi = g * HEAD_DIM, (g + 1) * HEAD_DIM
        w = jnp.where(row >= col, w_ref[g], 0.0).astype(BF16)
        bias = b_ref[:, g:g + 1]
        for n in range(ROWS_ATT // SGU_BLOCK):
            r0 = n * SGU_BLOCK
            mixed = jnp.dot(w, v[r0:r0 + SGU_BLOCK, lo:hi], preferred_element_type=F32) + bias
            o_ref[r0:r0 + SGU_BLOCK, lo:hi] = (u[r0:r0 + SGU_BLOCK, lo:hi] * mixed).astype(o_ref.dtype)


def _spatial_gating(p, norm_g, w_s, b_cols):
    t = p.shape[0]
    return pl.pallas_call(
        _sgu_kernel,
        grid=(t // ROWS_ATT,),
        in_specs=[pl.BlockSpec((ROWS_ATT, D_GROUP), lambda i: (i, COL_DU)),
                  pl.BlockSpec((ROWS_ATT, D_GROUP), lambda i: (i, COL_DV)),
                  pl.BlockSpec((1, D_GROUP), lambda i: (0, 0)),
                  pl.BlockSpec((HEADS, SGU_BLOCK, SGU_BLOCK), lambda i: (0, 0, 0)),
                  pl.BlockSpec((SGU_BLOCK, LANES), lambda i: (0, 0))],
        out_specs=pl.BlockSpec((ROWS_ATT, D_GROUP), lambda i: (i, 0)),
        out_shape=jax.ShapeDtypeStruct((t, D_GROUP), BF16),
        compiler_params=_cparams("parallel"),
        name="spatial_gating",
    )(p, p, norm_g.reshape(1, D_GROUP), w_s, b_cols)


def _outproj_kernel(oa_ref, ob_ref, oc_ref, od_ref, w_ref, x_ref, mod_ref, g_ref, o_ref):
    y = jnp.dot(oa_ref[...], w_ref[0:D_GROUP, :], preferred_element_type=F32)
    y += jnp.dot(ob_ref[...], w_ref[D_GROUP:2 * D_GROUP, :], preferred_element_type=F32)
    y += jnp.dot(oc_ref[...], w_ref[2 * D_GROUP:3 * D_GROUP, :], preferred_element_type=F32)
    y += jnp.dot(od_ref[...], w_ref[3 * D_GROUP:, :], preferred_element_type=F32)
    o_ref[...] = x_ref[...] + mod_ref[0, 2:3, :] * _rms(y, g_ref[...])


def _out_projection(outs, w_out, x2, mod, post_g, seq):
    t, d = x2.shape
    tiles_per_batch = seq // ROWS_OUT
    mix = pl.BlockSpec((ROWS_OUT, D_GROUP), lambda m: (m, 0))
    return pl.pallas_call(
        _outproj_kernel,
        grid=(t // ROWS_OUT,),
        in_specs=[mix, mix, mix, mix,
                  pl.BlockSpec((HEADS * D_GROUP, d), lambda m: (0, 0)),
                  pl.BlockSpec((ROWS_OUT, d), lambda m: (m, 0)),
                  pl.BlockSpec((1, 6, d), lambda m: (m // tiles_per_batch, 0, 0)),
                  pl.BlockSpec((1, d), lambda m: (0, 0))],
        out_specs=pl.BlockSpec((ROWS_OUT, d), lambda m: (m, 0)),
        out_shape=jax.ShapeDtypeStruct((t, d), F32),
        compiler_params=_cparams("parallel"),
        name="out_projection",
    )(*outs, w_out, x2, mod, post_g.reshape(1, d))


def _ffn_kernel(x_ref, xp_ref, mod_ref, pre_ref, post_ref, wa_ref, wg_ref, wd_ref, cw_ref, cb_ref,
                o_ref, h_scr, hp_scr, a_scr, *, tiles_per_batch):
    m = pl.program_id(0)
    f = pl.program_id(1)

    def prenorm(x):
        return (_rms(x, pre_ref[...]) * (1.0 + mod_ref[0, 4:5, :]) + mod_ref[0, 3:4, :]).astype(BF16)

    @pl.when(f == 0)
    def _():
        h_scr[...] = prenorm(x_ref[...])
        hp_scr[...] = prenorm(xp_ref[...])

    keep = (m % tiles_per_batch != 0).astype(F32)
    a_scr[0:FFN_HALO, :] = jnp.dot(hp_scr[...], wa_ref[...], preferred_element_type=F32) * keep
    a_scr[FFN_HALO:, :] = jnp.dot(h_scr[...], wa_ref[...], preferred_element_type=F32)
    gate = jnp.dot(h_scr[...], wg_ref[...], preferred_element_type=F32)
    acc = cb_ref[...]
    for tap in range(FFN_CONV):
        off = FFN_HALO - (FFN_CONV - 1) + tap
        acc = acc + cw_ref[tap:tap + 1, :] * a_scr[off:off + ROWS_OUT, :]
    act = (_gelu_tanh(acc) * gate).astype(BF16)
    y = jnp.dot(act, wd_ref[...], preferred_element_type=F32)

    @pl.when(f == 0)
    def _():
        o_ref[...] = y

    @pl.when(f > 0)
    def _():
        o_ref[...] += y

    @pl.when(f == pl.num_programs(1) - 1)
    def _():
        o_ref[...] = x_ref[...] + mod_ref[0, 5:6, :] * _rms(o_ref[...], post_ref[...])


def _conv_ffn(x2, mod, pre_g, post_g, w_a, w_g, w_d, conv_w, conv_b, seq):
    t, d = x2.shape
    tiles_per_batch = seq // ROWS_OUT
    halo_per_tile = ROWS_OUT // FFN_HALO
    return pl.pallas_call(
        functools.partial(_ffn_kernel, tiles_per_batch=tiles_per_batch),
        grid=(t // ROWS_OUT, D_FF_PAD // D_GROUP),
        in_specs=[pl.BlockSpec((ROWS_OUT, d), lambda m, f: (m, 0)),
                  pl.BlockSpec((FFN_HALO, d), lambda m, f: (jnp.maximum(m * halo_per_tile - 1, 0), 0)),
                  pl.BlockSpec((1, 6, d), lambda m, f: (m // tiles_per_batch, 0, 0)),
                  pl.BlockSpec((1, d), lambda m, f: (0, 0)),
                  pl.BlockSpec((1, d), lambda m, f: (0, 0)),
                  pl.BlockSpec((d, D_GROUP), lambda m, f: (0, f)),
                  pl.BlockSpec((d, D_GROUP), lambda m, f: (0, f)),
                  pl.BlockSpec((D_GROUP, d), lambda m, f: (f, 0)),
                  pl.BlockSpec((FFN_CONV, D_GROUP), lambda m, f: (0, f)),
                  pl.BlockSpec((1, D_GROUP), lambda m, f: (0, f))],
        out_specs=pl.BlockSpec((ROWS_OUT, d), lambda m, f: (m, 0)),
        out_shape=jax.ShapeDtypeStruct((t, d), F32),
        scratch_shapes=[pltpu.VMEM((ROWS_OUT, d), BF16),
                        pltpu.VMEM((FFN_HALO, d), BF16),
                        pltpu.VMEM((FFN_HALO + ROWS_OUT, D_GROUP), F32)],
        compiler_params=_cparams("parallel", "arbitrary"),
        name="conv_ffn",
    )(x2, x2, mod, pre_g.reshape(1, d), post_g.reshape(1, d), w_a, w_g, w_d, conv_w, conv_b)


def _pad_cols(a, width):
    return jnp.pad(a, ((0, 0), (0, width - a.shape[1])))


def _layer(x2, mod, batch, seq, mix_pre_g, mix_post_g, w_in, dn_conv_w, dn_a_log, dn_dt_bias, dn_norm_g,
           rel_bias, sinks, sgu_norm_g, sgu_w, sgu_b, w_out, ffn_pre_g, ffn_post_g, ffn_w_up,
           ffn_conv_w, ffn_conv_b, ffn_w_down):
    d = x2.shape[1]
    t = x2.shape[0]
    small_lo = 4 * D_GROUP
    w_main = jnp.concatenate([w_in[:, :small_lo], w_in[:, small_lo + N_SMALL:]], axis=1).astype(BF16)
    w_small = w_in[:, small_lo:small_lo + N_SMALL]
    w_small_cols = _pad_cols(w_small, LANES).astype(BF16)
    w_small_rows = w_small.T.astype(BF16)
    p, small, small_t = _in_projection(x2, mod, mix_pre_g, w_main, w_small_cols, w_small_rows, seq)
    small_t3 = small_t.reshape(SUBLANES, t // CHUNK, CHUNK).transpose(1, 0, 2)

    prm = jnp.stack([dn_dt_bias, dn_a_log], axis=0)
    prm_row = jnp.zeros((SUBLANES, LANES), F32).at[0:2, HEADS:2 * HEADS].set(prm)
    prm_col = jnp.zeros((SUBLANES, LANES), F32).at[HEADS:2 * HEADS, 0:2].set(prm.T)
    out_a = _gated_deltanet(p, small, small_t3, dn_conv_w, prm_row, prm_col, dn_norm_g, batch, seq)

    band = (B_PREV_CHUNKS + 1) * CHUNK
    dist = B_PREV_CHUNKS * CHUNK + np.arange(CHUNK)[:, None] - np.arange(band)[None, :]
    bias = rel_bias[:, np.clip(dist, -REL_CLIP, REL_CLIP) + REL_CLIP]
    out_b = _relbias_attention(p, bias, batch, seq)
    out_c = _swa_attention(p, sinks, batch, seq)
    out_d = _spatial_gating(p, sgu_norm_g, sgu_w, _pad_cols(sgu_b.T, LANES))

    x2 = _out_projection((out_a, out_b, out_c, out_d), w_out.astype(BF16), x2, mod, mix_post_g, seq)

    w_a = _pad_cols(ffn_w_up[:, :D_FF], D_FF_PAD).astype(BF16)
    w_g = _pad_cols(ffn_w_up[:, D_FF:], D_FF_PAD).astype(BF16)
    w_d = jnp.pad(ffn_w_down, ((0, D_FF_PAD - D_FF), (0, 0))).astype(BF16)
    return _conv_ffn(x2, mod, ffn_pre_g, ffn_post_g, w_a, w_g, w_d, _pad_cols(ffn_conv_w, D_FF_PAD),
                     _pad_cols(ffn_conv_b.reshape(1, D_FF), D_FF_PAD), seq)


def kernel(x, c, ada_w, ada_b, mix_pre_g, mix_post_g, w_in, dn_conv_w, dn_a_log, dn_dt_bias, dn_norm_g,
           rel_bias, sinks, sgu_norm_g, sgu_w, sgu_b, w_out, ffn_pre_g, ffn_post_g, ffn_w_up, ffn_conv_w,
           ffn_conv_b, ffn_w_down):
    batch, seq, d = x.shape
    depth = ada_w.shape[0]
    assert d == D_MODEL and seq % ROWS_PROJ == 0 and batch <= SUBLANES
    c_pad = jnp.pad(c, ((0, SUBLANES - batch), (0, 0)))
    mod_all = _modulation(c_pad, ada_w, ada_b)[:, :batch].reshape(depth, batch, 6, d)
    x2 = x.reshape(batch * seq, d)
    for l in range(depth):
        x2 = _layer(x2, mod_all[l], batch, seq, mix_pre_g[l], mix_post_g[l], w_in[l], dn_conv_w[l],
                    dn_a_log[l], dn_dt_bias[l], dn_norm_g[l], rel_bias[l], sinks[l], sgu_norm_g[l],
                    sgu_w[l], sgu_b[l], w_out[l], ffn_pre_g[l], ffn_post_g[l], ffn_w_up[l],
                    ffn_conv_w[l], ffn_conv_b[l], ffn_w_down[l])
    return x2.reshape(batch, seq, d)
```

```python
import functools

import numpy as np
import jax
import jax.numpy as jnp
from jax import lax
from jax.experimental import pallas as pl
from jax.experimental.pallas import tpu as pltpu

F32 = jnp.float32
BF16 = jnp.bfloat16

D_MODEL = 2048
CHUNK = 64
HEAD_DIM = 128
HEADS = 4
D_GROUP = HEADS * HEAD_DIM
DN_CONV = 4
B_PREV_CHUNKS = 8
REL_CLIP = 256
C_PREV_CHUNKS = 2
C_KV_HEADS = 2
SGU_BLOCK = 128
D_FF = 5504
FFN_CONV = 3
EPS = 1e-6

LANES = 128
SUBLANES = 8
VMEM_LIMIT_BYTES = 56 * 1024 * 1024

N_SMALL = 2 * HEADS
N_MAIN = 11 * D_GROUP
COL_AQ, COL_AK, COL_AV, COL_AG, COL_BQ, COL_BK, COL_BV, COL_CQ, COL_CKV, COL_DU, COL_DV = range(11)
D_FF_PAD = 11 * D_GROUP

ROWS_ATT = 512
Q_ROWS = 2 * CHUNK
DN_CHUNKS = 4
DN_ROWS = DN_CHUNKS * CHUNK
HALO = 16
ROWS_PROJ = 1024
ROWS_OUT = 512
MOD_COLS = 1024

ALIBI_SLOPES = tuple(float(2.0 ** (-8.0 * h / HEADS)) for h in range(1, HEADS + 1))


def _cparams(*sem):
    return pltpu.CompilerParams(dimension_semantics=sem, vmem_limit_bytes=VMEM_LIMIT_BYTES)


def _sigmoid(x):
    return 1.0 / (1.0 + jnp.exp(-x))


def _silu(x):
    return x * _sigmoid(x)


def _softplus(x):
    return jnp.maximum(x, 0.0) + jnp.log1p(jnp.exp(-jnp.abs(x)))


def _gelu_tanh(x):
    return 0.5 * x * (1.0 + jnp.tanh(float(np.sqrt(2.0 / np.pi)) * (x + 0.044715 * (x * x * x))))


def _rms(x, g):
    return x * lax.rsqrt(jnp.mean(x * x, axis=-1, keepdims=True) + EPS) * g


def _dot(a, b):
    return jnp.dot(a.astype(BF16), b.astype(BF16), preferred_element_type=F32)


def _dot_nt(a, b):
    return lax.dot_general(a.astype(BF16), b.astype(BF16), (((1,), (1,)), ((), ())),
                           preferred_element_type=F32)


def _dot_f32(a, b):
    return jnp.dot(a, b, precision=lax.Precision.HIGHEST, preferred_element_type=F32)


def _mod_kernel(c_ref, w_ref, b_ref, o_ref):
    cond = _silu(c_ref[...])
    o_ref[0] = _dot(cond, w_ref[0]) + b_ref[0]


def _modulation(c_pad, ada_w, ada_b):
    depth, d, n = ada_w.shape
    return pl.pallas_call(
        _mod_kernel,
        grid=(depth, n // MOD_COLS),
        in_specs=[pl.BlockSpec((SUBLANES, d), lambda l, j: (0, 0)),
                  pl.BlockSpec((1, d, MOD_COLS), lambda l, j: (l, 0, j)),
                  pl.BlockSpec((1, 1, MOD_COLS), lambda l, j: (l, 0, j))],
        out_specs=pl.BlockSpec((1, SUBLANES, MOD_COLS), lambda l, j: (l, 0, j)),
        out_shape=jax.ShapeDtypeStruct((depth, SUBLANES, n), F32),
        compiler_params=_cparams("parallel", "parallel"),
        name="modulation",
    )(c_pad, ada_w, ada_b.reshape(depth, 1, n))


def _inproj_kernel(x_ref, mod_ref, g_ref, w_ref, ws_ref, wst_ref, p_ref, s_ref, st_ref, h_scr):
    @pl.when(pl.program_id(1) == 0)
    def _():
        h = _rms(x_ref[...], g_ref[...]) * (1.0 + mod_ref[0, 1:2, :]) + mod_ref[0, 0:1, :]
        hb = h.astype(BF16)
        h_scr[...] = hb
        s_ref[...] = jnp.dot(hb, ws_ref[...], preferred_element_type=F32)
        st_ref[...] = _dot_nt(wst_ref[...], hb)

    p_ref[...] = jnp.dot(h_scr[...], w_ref[...], preferred_element_type=F32).astype(p_ref.dtype)


def _in_projection(x2, mod, pre_g, w_main, w_small, w_small_t, seq):
    t, d = x2.shape
    tiles_per_batch = seq // ROWS_PROJ
    return pl.pallas_call(
        _inproj_kernel,
        grid=(t // ROWS_PROJ, N_MAIN // D_GROUP),
        in_specs=[pl.BlockSpec((ROWS_PROJ, d), lambda m, n: (m, 0)),
                  pl.BlockSpec((1, 6, d), lambda m, n: (m // tiles_per_batch, 0, 0)),
                  pl.BlockSpec((1, d), lambda m, n: (0, 0)),
                  pl.BlockSpec((d, D_GROUP), lambda m, n: (0, n)),
                  pl.BlockSpec((d, LANES), lambda m, n: (0, 0)),
                  pl.BlockSpec((SUBLANES, d), lambda m, n: (0, 0))],
        out_specs=[pl.BlockSpec((ROWS_PROJ, D_GROUP), lambda m, n: (m, n)),
                   pl.BlockSpec((ROWS_PROJ, LANES), lambda m, n: (m, 0)),
                   pl.BlockSpec((SUBLANES, ROWS_PROJ), lambda m, n: (0, m))],
        out_shape=[jax.ShapeDtypeStruct((t, N_MAIN), BF16),
                   jax.ShapeDtypeStruct((t, LANES), F32),
                   jax.ShapeDtypeStruct((SUBLANES, t), F32)],
        scratch_shapes=[pltpu.VMEM((ROWS_PROJ, d), BF16)],
        compiler_params=_cparams("parallel", "arbitrary"),
        name="in_projection",
    )(x2, mod, pre_g.reshape(1, d), w_main, w_small, w_small_t)


def _unit_lower_inverse(mats, row, col):
    blk_r, blk_c = jnp.right_shift(row, 4), jnp.right_shift(col, 4)
    half_r, half_c = jnp.right_shift(row, 5), jnp.right_shift(col, 5)
    eye = (row == col).astype(F32)
    a_diag = [jnp.where(blk_r == blk_c, a, 0.0) for a in mats]
    a_pair = [jnp.where(half_r == half_c, a, 0.0) - d for a, d in zip(mats, a_diag)]
    a_far = [jnp.where(half_r != half_c, a, 0.0) for a in mats]
    p2 = [_dot(d, d) for d in a_diag]
    t = [eye - d for d in a_diag]
    t = [x + _dot(x, p) for x, p in zip(t, p2)]
    p4 = [_dot(p, p) for p in p2]
    t = [x + _dot(x, p) for x, p in zip(t, p4)]
    p8 = [_dot(p, p) for p in p4]
    t = [x + _dot(x, p) for x, p in zip(t, p8)]
    for off in (a_pair, a_far):
        left = [_dot(x, a) for x, a in zip(t, off)]
        t = [x - _dot(l, x) for x, l in zip(t, left)]
    return t


def _delta_kernel(q_ref, k_ref, v_ref, qp_ref, kp_ref, vp_ref, gate_ref, sm_ref, smt_ref, cw_ref,
                  prow_ref, pcol_ref, ng_ref, o_ref, state_scr, xe_scr):
    step = pl.program_id(1)

    @pl.when(step == 0)
    def _():
        state_scr[...] = jnp.zeros_like(state_scr)

    keep = (step > 0).astype(F32)
    halo = jnp.concatenate([qp_ref[...], kp_ref[...], vp_ref[...]], axis=1).astype(F32)
    xe_scr[0:HALO, :] = halo * keep
    xe_scr[HALO:, 0:D_GROUP] = q_ref[...].astype(F32)
    xe_scr[HALO:, D_GROUP:2 * D_GROUP] = k_ref[...].astype(F32)
    xe_scr[HALO:, 2 * D_GROUP:] = v_ref[...].astype(F32)
    acc = None
    for tap in range(DN_CONV):
        off = HALO - (DN_CONV - 1) + tap
        term = cw_ref[tap:tap + 1, :] * xe_scr[off:off + DN_ROWS, :]
        acc = term if acc is None else acc + term
    qkv = _silu(acc)

    row = lax.broadcasted_iota(jnp.int32, (CHUNK, CHUNK), 0)
    col = lax.broadcasted_iota(jnp.int32, (CHUNK, CHUNK), 1)
    causal = row >= col
    strict = row > col
    lower_ones = causal.astype(F32)
    upper_ones = (row <= col).astype(F32)

    sm = sm_ref[...]
    la_cols = -jnp.exp(prow_ref[1:2, :]) * _softplus(sm + prow_ref[0:1, :])
    beta_cols = _sigmoid(sm)
    neg_a_col = -jnp.exp(pcol_ref[:, 1:2])
    dt_col = pcol_ref[:, 0:1]

    scale = HEAD_DIM ** -0.5
    qs, ks, vbs, kbs = [], [], [], []
    for h in range(HEADS):
        lo, hi = h * HEAD_DIM, (h + 1) * HEAD_DIM
        q = qkv[:, lo:hi]
        k = qkv[:, D_GROUP + lo:D_GROUP + hi]
        v = qkv[:, 2 * D_GROUP + lo:2 * D_GROUP + hi]
        beta = beta_cols[:, h:h + 1]
        k = k * lax.rsqrt(jnp.sum(k * k, axis=-1, keepdims=True) + EPS)
        qs.append(q * lax.rsqrt(jnp.sum(q * q, axis=-1, keepdims=True) + EPS) * scale)
        ks.append(k)
        kbs.append(k * beta)
        vbs.append(v * beta)

    probs = [(c, h) for c in range(DN_CHUNKS) for h in range(HEADS)]
    g_cols, g_rows = [], []
    for c in range(DN_CHUNKS):
        rows = slice(c * CHUNK, (c + 1) * CHUNK)
        g_cols.append(_dot_f32(lower_ones, la_cols[rows]))
        g_rows.append(_dot_f32(neg_a_col * _softplus(smt_ref[c] + dt_col), upper_ones))

    def rows_of(c):
        return slice(c * CHUNK, (c + 1) * CHUNK)

    g_col = [g_cols[c][:, HEADS + h:HEADS + h + 1] for c, h in probs]
    g_last = [g[CHUNK - 1:CHUNK, :] for g in g_col]
    decay = [jnp.exp(jnp.where(causal, g - g_rows[c][HEADS + h:HEADS + h + 1, :], -jnp.inf))
             for g, (c, h) in zip(g_col, probs)]
    kk = [_dot_nt(kbs[h][rows_of(c)], ks[h][rows_of(c)]) for c, h in probs]
    qk = [_dot_nt(qs[h][rows_of(c)], ks[h][rows_of(c)]) for c, h in probs]
    t_inv = _unit_lower_inverse([jnp.where(strict, x * d, 0.0) for x, d in zip(kk, decay)], row, col)
    exp_g = [jnp.exp(g) for g in g_col]
    uw = [_dot(t, jnp.concatenate([vbs[h][rows_of(c)], kbs[h][rows_of(c)] * e], axis=1))
          for t, e, (c, h) in zip(t_inv, exp_g, probs)]
    attn = [(x * d).astype(BF16) for x, d in zip(qk, decay)]
    wq = [jnp.concatenate([x[:, HEAD_DIM:], qs[h][rows_of(c)] * e], axis=0).astype(BF16)
          for x, e, (c, h) in zip(uw, exp_g, probs)]
    k_dec_t = [(ks[h][rows_of(c)] * jnp.exp(gl - g)).T.astype(BF16)
               for g, gl, (c, h) in zip(g_col, g_last, probs)]
    decay_last = [jnp.exp(gl) for gl in g_last]

    state = [state_scr[h] for h in range(HEADS)]
    for c in range(DN_CHUNKS):
        idx = [c * HEADS + h for h in range(HEADS)]
        ws_qs = [jnp.dot(wq[i], state[h].astype(BF16), preferred_element_type=F32)
                 for h, i in enumerate(idx)]
        v_new = [(uw[i][:, :HEAD_DIM] - x[:CHUNK]).astype(BF16) for x, i in zip(ws_qs, idx)]
        state = [state[h] * decay_last[i] + jnp.dot(k_dec_t[i], v_new[h], preferred_element_type=F32)
                 for h, i in enumerate(idx)]
        for h, i in enumerate(idx):
            lo, hi = h * HEAD_DIM, (h + 1) * HEAD_DIM
            o = ws_qs[h][CHUNK:] + jnp.dot(attn[i], v_new[h], preferred_element_type=F32)
            o = _rms(o, ng_ref[...]) * _silu(gate_ref[rows_of(c), lo:hi].astype(F32))
            o_ref[rows_of(c), lo:hi] = o.astype(o_ref.dtype)
    for h in range(HEADS):
        state_scr[h] = state[h]


def _gated_deltanet(p, small, small_t3, conv_w, prm_row, prm_col, norm_g, batch, seq):
    t = p.shape[0]
    steps = seq // DN_ROWS
    halo_per_step = DN_ROWS // HALO

    def cur(j):
        return pl.BlockSpec((DN_ROWS, D_GROUP), lambda b, s: (b * steps + s, j))

    def prev(j):
        return pl.BlockSpec((HALO, D_GROUP),
                            lambda b, s: (jnp.maximum((b * steps + s) * halo_per_step - 1, 0), j))

    return pl.pallas_call(
        _delta_kernel,
        grid=(batch, steps),
        in_specs=[cur(COL_AQ), cur(COL_AK), cur(COL_AV), prev(COL_AQ), prev(COL_AK), prev(COL_AV),
                  cur(COL_AG),
                  pl.BlockSpec((DN_ROWS, LANES), lambda b, s: (b * steps + s, 0)),
                  pl.BlockSpec((DN_CHUNKS, SUBLANES, CHUNK), lambda b, s: (b * steps + s, 0, 0)),
                  pl.BlockSpec((DN_CONV, 3 * D_GROUP), lambda b, s: (0, 0)),
                  pl.BlockSpec((SUBLANES, LANES), lambda b, s: (0, 0)),
                  pl.BlockSpec((SUBLANES, LANES), lambda b, s: (0, 0)),
                  pl.BlockSpec((1, HEAD_DIM), lambda b, s: (0, 0))],
        out_specs=pl.BlockSpec((DN_ROWS, D_GROUP), lambda b, s: (b * steps + s, 0)),
        out_shape=jax.ShapeDtypeStruct((t, D_GROUP), BF16),
        scratch_shapes=[pltpu.VMEM((HEADS, HEAD_DIM, HEAD_DIM), F32),
                        pltpu.VMEM((HALO + DN_ROWS, 3 * D_GROUP), F32)],
        compiler_params=_cparams("parallel", "arbitrary"),
        name="gated_deltanet",
    )(p, p, p, p, p, p, p, small, small_t3, conv_w, prm_row, prm_col, norm_g.reshape(1, HEAD_DIM))


def _relbias_kernel(q_ref, kc_ref, kp_ref, vc_ref, vp_ref, bias_ref, o_ref, k_scr, v_scr):
    i = pl.program_id(1)
    k_scr[0:ROWS_ATT, :] = kp_ref[...]
    k_scr[ROWS_ATT:, :] = kc_ref[...]
    v_scr[0:ROWS_ATT, :] = vp_ref[...]
    v_scr[ROWS_ATT:, :] = vc_ref[...]
    span = B_PREV_CHUNKS * CHUNK + Q_ROWS
    key_pos = lax.broadcasted_iota(jnp.int32, (Q_ROWS, span), 1)
    first_valid = jnp.where(i > 0, 0, ROWS_ATT)
    scale = HEAD_DIM ** -0.5
    subs = [sub * Q_ROWS for sub in range(ROWS_ATT // Q_ROWS)]
    for h in range(HEADS):
        lo, hi = h * HEAD_DIM, (h + 1) * HEAD_DIM
        bias = bias_ref[h]
        scores = [_dot_nt(q_ref[r0:r0 + Q_ROWS, lo:hi], k_scr[r0:r0 + span, lo:hi]) for r0 in subs]
        probs, invs = [], []
        for r0, s in zip(subs, scores):
            s = jnp.where(key_pos + r0 >= first_valid, s * scale + bias, -jnp.inf)
            e = jnp.exp(s - jnp.max(s, axis=-1, keepdims=True))
            probs.append(e.astype(BF16))
            invs.append(1.0 / jnp.sum(e, axis=-1, keepdims=True))
        outs = [jnp.dot(e, v_scr[r0:r0 + span, lo:hi], preferred_element_type=F32)
                for r0, e in zip(subs, probs)]
        for r0, o, inv in zip(subs, outs, invs):
            o_ref[r0:r0 + Q_ROWS, lo:hi] = (o * inv).astype(o_ref.dtype)


def _relbias_attention(p, bias, batch, seq):
    t = p.shape[0]
    nb = seq // ROWS_ATT

    def cur(j):
        return pl.BlockSpec((ROWS_ATT, D_GROUP), lambda b, i: (b * nb + i, j))

    def prev(j):
        return pl.BlockSpec((ROWS_ATT, D_GROUP), lambda b, i: (b * nb + jnp.maximum(i - 1, 0), j))

    return pl.pallas_call(
        _relbias_kernel,
        grid=(batch, nb),
        in_specs=[cur(COL_BQ), cur(COL_BK), prev(COL_BK), cur(COL_BV), prev(COL_BV),
                  pl.BlockSpec(bias.shape, lambda b, i: (0, 0, 0))],
        out_specs=pl.BlockSpec((ROWS_ATT, D_GROUP), lambda b, i: (b * nb + i, 0)),
        out_shape=jax.ShapeDtypeStruct((t, D_GROUP), BF16),
        scratch_shapes=[pltpu.VMEM((2 * ROWS_ATT, D_GROUP), BF16),
                        pltpu.VMEM((2 * ROWS_ATT, D_GROUP), BF16)],
        compiler_params=_cparams("parallel", "arbitrary"),
        name="relbias_attention",
    )(p, p, p, p, p, bias)


def _swa_kernel(sink_ref, q_ref, kvc_ref, kvp_ref, bias_ref, o_ref, kv_scr):
    i = pl.program_id(1)
    halo = C_PREV_CHUNKS * CHUNK
    kv_scr[0:halo, :] = kvp_ref[...]
    kv_scr[halo:, :] = kvc_ref[...]
    span = halo + Q_ROWS
    group = HEADS // C_KV_HEADS
    stacked = group * Q_ROWS
    key_pos = lax.broadcasted_iota(jnp.int32, (stacked, span), 1)
    q_row = lax.broadcasted_iota(jnp.int32, (stacked, 1), 0)
    first_valid = jnp.where(i > 0, 0, halo)
    scale = HEAD_DIM ** -0.5
    kv_width = C_KV_HEADS * HEAD_DIM
    for kvh in range(C_KV_HEADS):
        klo = kvh * HEAD_DIM
        h0 = kvh * group
        sink = jnp.where(q_row < Q_ROWS, sink_ref[h0], sink_ref[h0 + 1])
        bias = bias_ref[kvh]
        subs = [sub * Q_ROWS for sub in range(ROWS_ATT // Q_ROWS)]
        scores = []
        for r0 in subs:
            q = jnp.concatenate([q_ref[r0:r0 + Q_ROWS, (h0 + g) * HEAD_DIM:(h0 + g + 1) * HEAD_DIM]
                                 for g in range(group)], axis=0)
            scores.append(_dot_nt(q, kv_scr[r0:r0 + span, klo:klo + HEAD_DIM]))
        probs, invs = [], []
        for r0, s in zip(subs, scores):
            s = jnp.where(key_pos + r0 >= first_valid, s * scale + bias, -jnp.inf)
            m = jnp.maximum(jnp.max(s, axis=-1, keepdims=True), sink)
            e = jnp.exp(s - m)
            probs.append(e.astype(BF16))
            invs.append(1.0 / (jnp.sum(e, axis=-1, keepdims=True) + jnp.exp(sink - m)))
        vlo = kv_width + klo
        outs = [jnp.dot(e, kv_scr[r0:r0 + span, vlo:vlo + HEAD_DIM], preferred_element_type=F32)
                for r0, e in zip(subs, probs)]
        for r0, o, inv in zip(subs, outs, invs):
            o = o * inv
            for g in range(group):
                o_ref[r0:r0 + Q_ROWS, (h0 + g) * HEAD_DIM:(h0 + g + 1) * HEAD_DIM] = (
                    o[g * Q_ROWS:(g + 1) * Q_ROWS].astype(o_ref.dtype))


def _swa_attention(p, sinks, bias, batch, seq):
    t = p.shape[0]
    nb = seq // ROWS_ATT
    halo = C_PREV_CHUNKS * CHUNK
    halo_per_tile = ROWS_ATT // halo
    return pl.pallas_call(
        _swa_kernel,
        grid=(batch, nb),
        in_specs=[pl.BlockSpec(memory_space=pltpu.SMEM),
                  pl.BlockSpec((ROWS_ATT, D_GROUP), lambda b, i: (b * nb + i, COL_CQ)),
                  pl.BlockSpec((ROWS_ATT, D_GROUP), lambda b, i: (b * nb + i, COL_CKV)),
                  pl.BlockSpec((halo, D_GROUP),
                               lambda b, i: (jnp.maximum((b * nb + i) * halo_per_tile - 1, 0), COL_CKV)),
                  pl.BlockSpec(bias.shape, lambda b, i: (0, 0, 0))],
        out_specs=pl.BlockSpec((ROWS_ATT, D_GROUP), lambda b, i: (b * nb + i, 0)),
        out_shape=jax.ShapeDtypeStruct((t, D_GROUP), BF16),
        scratch_shapes=[pltpu.VMEM((halo + ROWS_ATT, D_GROUP), BF16)],
        compiler_params=_cparams("parallel", "arbitrary"),
        name="swa_attention",
    )(sinks, p, p, p, bias)


def _pair_mask(n_prev):
    span = n_prev * CHUNK + Q_ROWS
    r = np.arange(Q_ROWS)[:, None]
    j = np.arange(span)[None, :]
    band_pos = j - (r // CHUNK) * CHUNK
    return np.where((band_pos >= 0) & (band_pos < (n_prev + 1) * CHUNK), 0.0, -np.inf).astype(np.float32)


def _relbias_table(rel_bias):
    span = B_PREV_CHUNKS * CHUNK + Q_ROWS
    period = Q_ROWS + span
    k = np.arange(period) - (Q_ROWS - 1)
    idx = np.clip(B_PREV_CHUNKS * CHUNK - k, -REL_CLIP, REL_CLIP) + REL_CLIP
    g = rel_bias[:, idx]
    skew = jnp.tile(g, (1, Q_ROWS))[:, :Q_ROWS * (period - 1)].reshape(HEADS, Q_ROWS, period - 1)
    return skew[:, :, Q_ROWS - 1:Q_ROWS - 1 + span] + _pair_mask(B_PREV_CHUNKS)


def _alibi_table():
    halo = C_PREV_CHUNKS * CHUNK
    span = halo + Q_ROWS
    r = np.arange(Q_ROWS)[:, None]
    j = np.arange(span)[None, :]
    dist = np.abs(halo + r - j).astype(np.float32)
    per_head = [(-np.float32(s)) * dist + _pair_mask(C_PREV_CHUNKS) for s in ALIBI_SLOPES]
    group = HEADS // C_KV_HEADS
    return np.stack([np.concatenate(per_head[kv * group:(kv + 1) * group], axis=0)
                     for kv in range(C_KV_HEADS)], axis=0)


def _sgu_kernel(u_ref, v_ref, g_ref, w_ref, b_ref, o_ref):
    u = _gelu_tanh(u_ref[...].astype(F32))
    v = _gelu_tanh(v_ref[...].astype(F32))
    mu = jnp.mean(v, axis=-1, keepdims=True)
    var = jnp.mean(jnp.square(v - mu), axis=-1, keepdims=True)
    v = ((v - mu) * lax.rsqrt(var + EPS) * g_ref[...]).astype(BF16)
    row = lax.broadcasted_iota(jnp.int32, (SGU_BLOCK, SGU_BLOCK), 0)
    col = lax.broadcasted_iota(jnp.int32, (SGU_BLOCK, SGU_BLOCK), 1)
    for g in range(HEADS):
        lo, hi = g * HEAD_DIM, (g + 1) * HEAD_DIM
        w = jnp.where(row >= col, w_ref[g], 0.0).astype(BF16)
        bias = b_ref[:, g:g + 1]
        for n in range(ROWS_ATT // SGU_BLOCK):
            r0 = n * SGU_BLOCK
            mixed = jnp.dot(w, v[r0:r0 + SGU_BLOCK, lo:hi], preferred_element_type=F32) + bias
            o_ref[r0:r0 + SGU_BLOCK, lo:hi] = (u[r0:r0 + SGU_BLOCK, lo:hi] * mixed).astype(o_ref.dtype)


def _spatial_gating(p, norm_g, w_s, b_cols):
    t = p.shape[0]
    return pl.pallas_call(
        _sgu_kernel,
        grid=(t // ROWS_ATT,),
        in_specs=[pl.BlockSpec((ROWS_ATT, D_GROUP), lambda i: (i, COL_DU)),
                  pl.BlockSpec((ROWS_ATT, D_GROUP), lambda i: (i, COL_DV)),
                  pl.BlockSpec((1, D_GROUP), lambda i: (0, 0)),
                  pl.BlockSpec((HEADS, SGU_BLOCK, SGU_BLOCK), lambda i: (0, 0, 0)),
                  pl.BlockSpec((SGU_BLOCK, LANES), lambda i: (0, 0))],
        out_specs=pl.BlockSpec((ROWS_ATT, D_GROUP), lambda i: (i, 0)),
        out_shape=jax.ShapeDtypeStruct((t, D_GROUP), BF16),
        compiler_params=_cparams("parallel"),
        name="spatial_gating",
    )(p, p, norm_g.reshape(1, D_GROUP), w_s, b_cols)


def _outproj_kernel(oa_ref, ob_ref, oc_ref, od_ref, w_ref, x_ref, mod_ref, g_ref, o_ref):
    y = jnp.dot(oa_ref[...], w_ref[0:D_GROUP, :], preferred_element_type=F32)
    y += jnp.dot(ob_ref[...], w_ref[D_GROUP:2 * D_GROUP, :], preferred_element_type=F32)
    y += jnp.dot(oc_ref[...], w_ref[2 * D_GROUP:3 * D_GROUP, :], preferred_element_type=F32)
    y += jnp.dot(od_ref[...], w_ref[3 * D_GROUP:, :], preferred_element_type=F32)
    o_ref[...] = x_ref[...] + mod_ref[0, 2:3, :] * _rms(y, g_ref[...])


def _out_projection(outs, w_out, x2, mod, post_g, seq):
    t, d = x2.shape
    tiles_per_batch = seq // ROWS_OUT
    mix = pl.BlockSpec((ROWS_OUT, D_GROUP), lambda m: (m, 0))
    return pl.pallas_call(
        _outproj_kernel,
        grid=(t // ROWS_OUT,),
        in_specs=[mix, mix, mix, mix,
                  pl.BlockSpec((HEADS * D_GROUP, d), lambda m: (0, 0)),
                  pl.BlockSpec((ROWS_OUT, d), lambda m: (m, 0)),
                  pl.BlockSpec((1, 6, d), lambda m: (m // tiles_per_batch, 0, 0)),
                  pl.BlockSpec((1, d), lambda m: (0, 0))],
        out_specs=pl.BlockSpec((ROWS_OUT, d), lambda m: (m, 0)),
        out_shape=jax.ShapeDtypeStruct((t, d), F32),
        compiler_params=_cparams("parallel"),
        name="out_projection",
    )(*outs, w_out, x2, mod, post_g.reshape(1, d))


def _ffn_kernel(x_ref, xp_ref, mod_ref, pre_ref, post_ref, wa_ref, wg_ref, wd_ref, cw_ref, cb_ref,
                o_ref, h_scr, hp_scr, a_scr, *, tiles_per_batch):
    m = pl.program_id(0)
    f = pl.program_id(1)

    def prenorm(x):
        return (_rms(x, pre_ref[...]) * (1.0 + mod_ref[0, 4:5, :]) + mod_ref[0, 3:4, :]).astype(BF16)

    @pl.when(f == 0)
    def _():
        h_scr[...] = prenorm(x_ref[...])
        hp_scr[...] = prenorm(xp_ref[...])
        o_ref[...] = jnp.zeros_like(o_ref)

    keep = (m % tiles_per_batch != 0).astype(F32)
    a_scr[0:HALO, :] = jnp.dot(hp_scr[...], wa_ref[...], preferred_element_type=F32) * keep
    a_scr[HALO:, :] = jnp.dot(h_scr[...], wa_ref[...], preferred_element_type=F32)
    gate = jnp.dot(h_scr[...], wg_ref[...], preferred_element_type=F32)
    acc = cb_ref[...]
    for tap in range(FFN_CONV):
        off = HALO - (FFN_CONV - 1) + tap
        acc = acc + cw_ref[tap:tap + 1, :] * a_scr[off:off + ROWS_OUT, :]
    act = (_gelu_tanh(acc) * gate).astype(BF16)
    o_ref[...] += jnp.dot(act, wd_ref[...], preferred_element_type=F32)

    @pl.when(f == pl.num_programs(1) - 1)
    def _():
        o_ref[...] = x_ref[...] + mod_ref[0, 5:6, :] * _rms(o_ref[...], post_ref[...])


def _conv_ffn(x2, mod, pre_g, post_g, w_a, w_g, w_d, conv_w, conv_b, seq):
    t, d = x2.shape
    tiles_per_batch = seq // ROWS_OUT
    halo_per_tile = ROWS_OUT // HALO
    return pl.pallas_call(
        functools.partial(_ffn_kernel, tiles_per_batch=tiles_per_batch),
        grid=(t // ROWS_OUT, D_FF_PAD // D_GROUP),
        in_specs=[pl.BlockSpec((ROWS_OUT, d), lambda m, f: (m, 0)),
                  pl.BlockSpec((HALO, d), lambda m, f: (jnp.maximum(m * halo_per_tile - 1, 0), 0)),
                  pl.BlockSpec((1, 6, d), lambda m, f: (m // tiles_per_batch, 0, 0)),
                  pl.BlockSpec((1, d), lambda m, f: (0, 0)),
                  pl.BlockSpec((1, d), lambda m, f: (0, 0)),
                  pl.BlockSpec((d, D_GROUP), lambda m, f: (0, f)),
                  pl.BlockSpec((d, D_GROUP), lambda m, f: (0, f)),
                  pl.BlockSpec((D_GROUP, d), lambda m, f: (f, 0)),
                  pl.BlockSpec((FFN_CONV, D_GROUP), lambda m, f: (0, f)),
                  pl.BlockSpec((1, D_GROUP), lambda m, f: (0, f))],
        out_specs=pl.BlockSpec((ROWS_OUT, d), lambda m, f: (m, 0)),
        out_shape=jax.ShapeDtypeStruct((t, d), F32),
        scratch_shapes=[pltpu.VMEM((ROWS_OUT, d), BF16),
                        pltpu.VMEM((HALO, d), BF16),
                        pltpu.VMEM((HALO + ROWS_OUT, D_GROUP), F32)],
        compiler_params=_cparams("parallel", "arbitrary"),
        name="conv_ffn",
    )(x2, x2, mod, pre_g.reshape(1, d), post_g.reshape(1, d), w_a, w_g, w_d, conv_w, conv_b)


def _pad_cols(a, width):
    return jnp.pad(a, ((0, 0), (0, width - a.shape[1])))


def _layer(x2, mod, batch, seq, mix_pre_g, mix_post_g, w_in, dn_conv_w, dn_a_log, dn_dt_bias, dn_norm_g,
           rel_bias, sinks, sgu_norm_g, sgu_w, sgu_b, w_out, ffn_pre_g, ffn_post_g, ffn_w_up,
           ffn_conv_w, ffn_conv_b, ffn_w_down):
    t = x2.shape[0]
    small_lo = 4 * D_GROUP
    w_main = jnp.concatenate([w_in[:, :small_lo], w_in[:, small_lo + N_SMALL:]], axis=1).astype(BF16)
    w_small = w_in[:, small_lo:small_lo + N_SMALL]
    w_small_cols = _pad_cols(w_small, LANES).astype(BF16)
    w_small_rows = w_small.T.astype(BF16)
    p, small, small_t = _in_projection(x2, mod, mix_pre_g, w_main, w_small_cols, w_small_rows, seq)
    small_t3 = small_t.reshape(SUBLANES, t // CHUNK, CHUNK).transpose(1, 0, 2)

    prm = jnp.stack([dn_dt_bias, dn_a_log], axis=0)
    prm_row = jnp.zeros((SUBLANES, LANES), F32).at[0:2, HEADS:2 * HEADS].set(prm)
    prm_col = jnp.zeros((SUBLANES, LANES), F32).at[HEADS:2 * HEADS, 0:2].set(prm.T)
    out_a = _gated_deltanet(p, small, small_t3, dn_conv_w, prm_row, prm_col, dn_norm_g, batch, seq)
    out_b = _relbias_attention(p, _relbias_table(rel_bias), batch, seq)
    out_c = _swa_attention(p, sinks, jnp.asarray(_alibi_table()), batch, seq)
    out_d = _spatial_gating(p, sgu_norm_g, sgu_w, _pad_cols(sgu_b.T, LANES))

    x2 = _out_projection((out_a, out_b, out_c, out_d), w_out.astype(BF16), x2, mod, mix_post_g, seq)

    w_a = _pad_cols(ffn_w_up[:, :D_FF], D_FF_PAD).astype(BF16)
    w_g = _pad_cols(ffn_w_up[:, D_FF:], D_FF_PAD).astype(BF16)
    w_d = jnp.pad(ffn_w_down, ((0, D_FF_PAD - D_FF), (0, 0))).astype(BF16)
    return _conv_ffn(x2, mod, ffn_pre_g, ffn_post_g, w_a, w_g, w_d, _pad_cols(ffn_conv_w, D_FF_PAD),
                     _pad_cols(ffn_conv_b.reshape(1, D_FF), D_FF_PAD), seq)


def kernel(x, c, ada_w, ada_b, mix_pre_g, mix_post_g, w_in, dn_conv_w, dn_a_log, dn_dt_bias, dn_norm_g,
           rel_bias, sinks, sgu_norm_g, sgu_w, sgu_b, w_out, ffn_pre_g, ffn_post_g, ffn_w_up, ffn_conv_w,
           ffn_conv_b, ffn_w_down):
    batch, seq, d = x.shape
    depth = ada_w.shape[0]
    assert d == D_MODEL and seq % ROWS_PROJ == 0 and batch <= SUBLANES
    c_pad = jnp.pad(c, ((0, SUBLANES - batch), (0, 0)))
    mod_all = _modulation(c_pad, ada_w, ada_b)[:, :batch].reshape(depth, batch, 6, d)
    x2 = x.reshape(batch * seq, d)
    for l in range(depth):
        x2 = _layer(x2, mod_all[l], batch, seq, mix_pre_g[l], mix_post_g[l], w_in[l], dn_conv_w[l],
                    dn_a_log[l], dn_dt_bias[l], dn_norm_g[l], rel_bias[l], sinks[l], sgu_norm_g[l],
                    sgu_w[l], sgu_b[l], w_out[l], ffn_pre_g[l], ffn_post_g[l], ffn_w_up[l],
                    ffn_conv_w[l], ffn_conv_b[l], ffn_w_down[l])
    return x2.reshape(batch, seq, d)
```

```python
import functools

import numpy as np
import jax
import jax.numpy as jnp
from jax import lax
from jax.experimental import pallas as pl
from jax.experimental.pallas import tpu as pltpu

F32 = jnp.float32
BF16 = jnp.bfloat16

D_MODEL = 2048
CHUNK = 64
HEAD_DIM = 128
HEADS = 4
D_GROUP = HEADS * HEAD_DIM
DN_CONV = 4
B_PREV_CHUNKS = 8
REL_CLIP = 256
C_PREV_CHUNKS = 2
C_KV_HEADS = 2
SGU_BLOCK = 128
D_FF = 5504
FFN_CONV = 3
EPS = 1e-6

LANES = 128
SUBLANES = 8
VMEM_LIMIT_BYTES = 56 * 1024 * 1024

N_SMALL = 2 * HEADS
N_MAIN = 11 * D_GROUP
N_LO_TILES = 4
COL_AQ, COL_AK, COL_AV, COL_AG, COL_BQ, COL_BK, COL_BV, COL_CQ, COL_CKV, COL_DU, COL_DV = range(11)
D_FF_PAD = 11 * D_GROUP

ROWS_ATT = 512
Q_ROWS = 2 * CHUNK
DN_CHUNKS = 4
DN_ROWS = DN_CHUNKS * CHUNK
HALO = 16
ROWS_PROJ = 1024
ROWS_OUT = 512
MOD_COLS = 1024

ALIBI_SLOPES = tuple(float(2.0 ** (-8.0 * h / HEADS)) for h in range(1, HEADS + 1))


def _cparams(*sem):
    return pltpu.CompilerParams(dimension_semantics=sem, vmem_limit_bytes=VMEM_LIMIT_BYTES)


def _sigmoid(x):
    return 1.0 / (1.0 + jnp.exp(-x))


def _silu(x):
    return x * _sigmoid(x)


def _softplus(x):
    return jnp.maximum(x, 0.0) + jnp.log1p(jnp.exp(-jnp.abs(x)))


def _gelu_tanh(x):
    return 0.5 * x * (1.0 + jnp.tanh(float(np.sqrt(2.0 / np.pi)) * (x + 0.044715 * (x * x * x))))


def _rms(x, g):
    return x * lax.rsqrt(jnp.mean(x * x, axis=-1, keepdims=True) + EPS) * g


def _modulated_rms(x, gain, shift):
    return x * lax.rsqrt(jnp.mean(x * x, axis=-1, keepdims=True) + EPS) * gain + shift


def _dot(a, b):
    return jnp.dot(a.astype(BF16), b.astype(BF16), preferred_element_type=F32)


def _dot_nt(a, b):
    return lax.dot_general(a.astype(BF16), b.astype(BF16), (((1,), (1,)), ((), ())),
                           preferred_element_type=F32)


def _dot_f32(a, b):
    return jnp.dot(a, b, precision=lax.Precision.HIGHEST, preferred_element_type=F32)


def _mod_kernel(c_ref, w_ref, b_ref, o_ref):
    cond = _silu(c_ref[...])
    o_ref[0] = _dot(cond, w_ref[0]) + b_ref[0]


def _modulation(c_pad, ada_w, ada_b):
    depth, d, n = ada_w.shape
    return pl.pallas_call(
        _mod_kernel,
        grid=(depth, n // MOD_COLS),
        in_specs=[pl.BlockSpec((SUBLANES, d), lambda l, j: (0, 0)),
                  pl.BlockSpec((1, d, MOD_COLS), lambda l, j: (l, 0, j)),
                  pl.BlockSpec((1, 1, MOD_COLS), lambda l, j: (l, 0, j))],
        out_specs=pl.BlockSpec((1, SUBLANES, MOD_COLS), lambda l, j: (l, 0, j)),
        out_shape=jax.ShapeDtypeStruct((depth, SUBLANES, n), F32),
        compiler_params=_cparams("parallel", "parallel"),
        name="modulation",
    )(c_pad, ada_w, ada_b.reshape(depth, 1, n))


def _inproj_kernel(x_ref, mod_ref, g_ref, wlo_ref, whi_ref, ws_ref, p_ref, s_ref, st_ref, h_scr):
    n = pl.program_id(1)

    @pl.when(n == 0)
    def _():
        gain = g_ref[...] * (1.0 + mod_ref[0, 1:2, :])
        hb = _modulated_rms(x_ref[...], gain, mod_ref[0, 0:1, :]).astype(BF16)
        h_scr[...] = hb
        small = jnp.dot(hb, ws_ref[...], preferred_element_type=F32)
        s_ref[...] = small
        st_ref[...] = small.T[0:SUBLANES, :]

    @pl.when(n < N_LO_TILES)
    def _():
        p_ref[...] = _dot(h_scr[...], wlo_ref[...]).astype(p_ref.dtype)

    @pl.when(n >= N_LO_TILES)
    def _():
        p_ref[...] = _dot(h_scr[...], whi_ref[...]).astype(p_ref.dtype)


def _in_projection(x2, mod, pre_g, w_full, w_tail, w_small, layer, seq):
    t, d = x2.shape
    tiles_per_batch = seq // ROWS_PROJ
    return pl.pallas_call(
        _inproj_kernel,
        grid=(t // ROWS_PROJ, N_MAIN // D_GROUP),
        in_specs=[pl.BlockSpec((ROWS_PROJ, d), lambda m, n: (m, 0)),
                  pl.BlockSpec((1, 6, d), lambda m, n: (m // tiles_per_batch, 0, 0)),
                  pl.BlockSpec((1, d), lambda m, n: (0, 0)),
                  pl.BlockSpec((None, d, D_GROUP), lambda m, n: (layer, 0, jnp.minimum(n, N_LO_TILES - 1))),
                  pl.BlockSpec((None, d, D_GROUP), lambda m, n: (layer, 0, jnp.maximum(n - N_LO_TILES, 0))),
                  pl.BlockSpec((None, d, LANES), lambda m, n: (layer, 0, 0))],
        out_specs=[pl.BlockSpec((ROWS_PROJ, D_GROUP), lambda m, n: (m, n)),
                   pl.BlockSpec((ROWS_PROJ, LANES), lambda m, n: (m, 0)),
                   pl.BlockSpec((SUBLANES, ROWS_PROJ), lambda m, n: (0, m))],
        out_shape=[jax.ShapeDtypeStruct((t, N_MAIN), BF16),
                   jax.ShapeDtypeStruct((t, LANES), F32),
                   jax.ShapeDtypeStruct((SUBLANES, t), F32)],
        scratch_shapes=[pltpu.VMEM((ROWS_PROJ, d), BF16)],
        compiler_params=_cparams("parallel", "arbitrary"),
        name="in_projection",
    )(x2, mod, pre_g.reshape(1, d), w_full, w_tail, w_small)


def _unit_lower_inverse(mats, row, col):
    blk_r, blk_c = jnp.right_shift(row, 4), jnp.right_shift(col, 4)
    half_r, half_c = jnp.right_shift(row, 5), jnp.right_shift(col, 5)
    eye = (row == col).astype(F32)
    a_diag = [jnp.where(blk_r == blk_c, a, 0.0) for a in mats]
    a_pair = [jnp.where(half_r == half_c, a, 0.0) - d for a, d in zip(mats, a_diag)]
    a_far = [jnp.where(half_r != half_c, a, 0.0) for a in mats]
    p2 = [_dot(d, d) for d in a_diag]
    t = [eye - d for d in a_diag]
    t = [x + _dot(x, p) for x, p in zip(t, p2)]
    p4 = [_dot(p, p) for p in p2]
    t = [x + _dot(x, p) for x, p in zip(t, p4)]
    p8 = [_dot(p, p) for p in p4]
    t = [x + _dot(x, p) for x, p in zip(t, p8)]
    for off in (a_pair, a_far):
        left = [_dot(x, a) for x, a in zip(t, off)]
        t = [x - _dot(l, x) for x, l in zip(t, left)]
    return t


def _delta_kernel(q_ref, k_ref, v_ref, qp_ref, kp_ref, vp_ref, gate_ref, sm_ref, smt_ref, cw_ref,
                  prow_ref, pcol_ref, ng_ref, o_ref, state_scr, xe_scr):
    step = pl.program_id(1)

    @pl.when(step == 0)
    def _():
        state_scr[...] = jnp.zeros_like(state_scr)

    keep = (step > 0).astype(F32)
    halo = jnp.concatenate([qp_ref[...], kp_ref[...], vp_ref[...]], axis=1).astype(F32)
    xe_scr[0:HALO, :] = halo * keep
    xe_scr[HALO:, 0:D_GROUP] = q_ref[...].astype(F32)
    xe_scr[HALO:, D_GROUP:2 * D_GROUP] = k_ref[...].astype(F32)
    xe_scr[HALO:, 2 * D_GROUP:] = v_ref[...].astype(F32)
    acc = None
    for tap in range(DN_CONV):
        off = HALO - (DN_CONV - 1) + tap
        term = cw_ref[tap:tap + 1, :] * xe_scr[off:off + DN_ROWS, :]
        acc = term if acc is None else acc + term
    qkv = _silu(acc)

    row = lax.broadcasted_iota(jnp.int32, (CHUNK, CHUNK), 0)
    col = lax.broadcasted_iota(jnp.int32, (CHUNK, CHUNK), 1)
    causal = row >= col
    strict = row > col
    lower_ones = causal.astype(F32)
    upper_ones = (row <= col).astype(F32)

    sm = sm_ref[...]
    la_cols = -jnp.exp(prow_ref[1:2, :]) * _softplus(sm + prow_ref[0:1, :])
    beta_cols = _sigmoid(sm)
    neg_a_col = -jnp.exp(pcol_ref[:, 1:2])
    dt_col = pcol_ref[:, 0:1]

    scale = HEAD_DIM ** -0.5
    qs, ks, vbs, kbs = [], [], [], []
    for h in range(HEADS):
        lo, hi = h * HEAD_DIM, (h + 1) * HEAD_DIM
        q = qkv[:, lo:hi]
        k = qkv[:, D_GROUP + lo:D_GROUP + hi]
        v = qkv[:, 2 * D_GROUP + lo:2 * D_GROUP + hi]
        beta = beta_cols[:, h:h + 1]
        k = k * lax.rsqrt(jnp.sum(k * k, axis=-1, keepdims=True) + EPS)
        qs.append(q * lax.rsqrt(jnp.sum(q * q, axis=-1, keepdims=True) + EPS) * scale)
        ks.append(k)
        kbs.append(k * beta)
        vbs.append(v * beta)

    probs = [(c, h) for c in range(DN_CHUNKS) for h in range(HEADS)]
    g_cols, g_rows = [], []
    for c in range(DN_CHUNKS):
        rows = slice(c * CHUNK, (c + 1) * CHUNK)
        g_cols.append(_dot_f32(lower_ones, la_cols[rows]))
        g_rows.append(_dot_f32(neg_a_col * _softplus(smt_ref[c] + dt_col), upper_ones))

    def rows_of(c):
        return slice(c * CHUNK, (c + 1) * CHUNK)

    g_col = [g_cols[c][:, HEADS + h:HEADS + h + 1] for c, h in probs]
    g_last = [g[CHUNK - 1:CHUNK, :] for g in g_col]
    decay = [jnp.exp(jnp.where(causal, g - g_rows[c][HEADS + h:HEADS + h + 1, :], -jnp.inf))
             for g, (c, h) in zip(g_col, probs)]
    kk = [_dot_nt(kbs[h][rows_of(c)], ks[h][rows_of(c)]) for c, h in probs]
    qk = [_dot_nt(qs[h][rows_of(c)], ks[h][rows_of(c)]) for c, h in probs]
    t_inv = _unit_lower_inverse([jnp.where(strict, x * d, 0.0) for x, d in zip(kk, decay)], row, col)
    exp_g = [jnp.exp(g) for g in g_col]
    uw = [_dot(t, jnp.concatenate([vbs[h][rows_of(c)], kbs[h][rows_of(c)] * e], axis=1))
          for t, e, (c, h) in zip(t_inv, exp_g, probs)]
    attn = [(x * d).astype(BF16) for x, d in zip(qk, decay)]
    wq = [jnp.concatenate([x[:, HEAD_DIM:], qs[h][rows_of(c)] * e], axis=0).astype(BF16)
          for x, e, (c, h) in zip(uw, exp_g, probs)]
    k_dec_t = [(ks[h][rows_of(c)] * jnp.exp(gl - g)).T.astype(BF16)
               for g, gl, (c, h) in zip(g_col, g_last, probs)]
    decay_last = [jnp.exp(gl) for gl in g_last]

    state = [state_scr[h] for h in range(HEADS)]
    for c in range(DN_CHUNKS):
        idx = [c * HEADS + h for h in range(HEADS)]
        ws_qs = [jnp.dot(wq[i], state[h].astype(BF16), preferred_element_type=F32)
                 for h, i in enumerate(idx)]
        v_new = [(uw[i][:, :HEAD_DIM] - x[:CHUNK]).astype(BF16) for x, i in zip(ws_qs, idx)]
        state = [state[h] * decay_last[i] + jnp.dot(k_dec_t[i], v_new[h], preferred_element_type=F32)
                 for h, i in enumerate(idx)]
        for h, i in enumerate(idx):
            lo, hi = h * HEAD_DIM, (h + 1) * HEAD_DIM
            o = ws_qs[h][CHUNK:] + jnp.dot(attn[i], v_new[h], preferred_element_type=F32)
            o = _rms(o, ng_ref[...]) * _silu(gate_ref[rows_of(c), lo:hi].astype(F32))
            o_ref[rows_of(c), lo:hi] = o.astype(o_ref.dtype)
    for h in range(HEADS):
        state_scr[h] = state[h]


def _gated_deltanet(p, small, small_t3, conv_w, prm_row, prm_col, norm_g, batch, seq):
    t = p.shape[0]
    steps = seq // DN_ROWS
    halo_per_step = DN_ROWS // HALO

    def cur(j):
        return pl.BlockSpec((DN_ROWS, D_GROUP), lambda b, s: (b * steps + s, j))

    def prev(j):
        return pl.BlockSpec((HALO, D_GROUP),
                            lambda b, s: (jnp.maximum((b * steps + s) * halo_per_step - 1, 0), j))

    return pl.pallas_call(
        _delta_kernel,
        grid=(batch, steps),
        in_specs=[cur(COL_AQ), cur(COL_AK), cur(COL_AV), prev(COL_AQ), prev(COL_AK), prev(COL_AV),
                  cur(COL_AG),
                  pl.BlockSpec((DN_ROWS, LANES), lambda b, s: (b * steps + s, 0)),
                  pl.BlockSpec((DN_CHUNKS, SUBLANES, CHUNK), lambda b, s: (b * steps + s, 0, 0)),
                  pl.BlockSpec((DN_CONV, 3 * D_GROUP), lambda b, s: (0, 0)),
                  pl.BlockSpec((SUBLANES, LANES), lambda b, s: (0, 0)),
                  pl.BlockSpec((SUBLANES, LANES), lambda b, s: (0, 0)),
                  pl.BlockSpec((1, HEAD_DIM), lambda b, s: (0, 0))],
        out_specs=pl.BlockSpec((DN_ROWS, D_GROUP), lambda b, s: (b * steps + s, 0)),
        out_shape=jax.ShapeDtypeStruct((t, D_GROUP), BF16),
        scratch_shapes=[pltpu.VMEM((HEADS, HEAD_DIM, HEAD_DIM), F32),
                        pltpu.VMEM((HALO + DN_ROWS, 3 * D_GROUP), F32)],
        compiler_params=_cparams("parallel", "arbitrary"),
        name="gated_deltanet",
    )(p, p, p, p, p, p, p, small, small_t3, conv_w, prm_row, prm_col, norm_g.reshape(1, HEAD_DIM))


def _relbias_kernel(q_ref, kc_ref, kp_ref, vc_ref, vp_ref, bias_ref, o_ref, k_scr, v_scr):
    i = pl.program_id(1)
    k_scr[0:ROWS_ATT, :] = kp_ref[...]
    k_scr[ROWS_ATT:, :] = kc_ref[...]
    v_scr[0:ROWS_ATT, :] = vp_ref[...]
    v_scr[ROWS_ATT:, :] = vc_ref[...]
    span = B_PREV_CHUNKS * CHUNK + Q_ROWS
    key_pos = lax.broadcasted_iota(jnp.int32, (Q_ROWS, span), 1)
    first_valid = jnp.where(i > 0, 0, ROWS_ATT)
    scale = HEAD_DIM ** -0.5
    subs = [sub * Q_ROWS for sub in range(ROWS_ATT // Q_ROWS)]
    for h in range(HEADS):
        lo, hi = h * HEAD_DIM, (h + 1) * HEAD_DIM
        bias = bias_ref[h]
        scores = [_dot_nt(q_ref[r0:r0 + Q_ROWS, lo:hi], k_scr[r0:r0 + span, lo:hi]) for r0 in subs]
        probs, invs = [], []
        for r0, s in zip(subs, scores):
            s = jnp.where(key_pos + r0 >= first_valid, s * scale + bias, -jnp.inf)
            e = jnp.exp(s - jnp.max(s, axis=-1, keepdims=True))
            probs.append(e.astype(BF16))
            invs.append(1.0 / jnp.sum(e, axis=-1, keepdims=True))
        outs = [jnp.dot(e, v_scr[r0:r0 + span, lo:hi], preferred_element_type=F32)
                for r0, e in zip(subs, probs)]
        for r0, o, inv in zip(subs, outs, invs):
            o_ref[r0:r0 + Q_ROWS, lo:hi] = (o * inv).astype(o_ref.dtype)


def _relbias_attention(p, bias, batch, seq):
    t = p.shape[0]
    nb = seq // ROWS_ATT

    def cur(j):
        return pl.BlockSpec((ROWS_ATT, D_GROUP), lambda b, i: (b * nb + i, j))

    def prev(j):
        return pl.BlockSpec((ROWS_ATT, D_GROUP), lambda b, i: (b * nb + jnp.maximum(i - 1, 0), j))

    return pl.pallas_call(
        _relbias_kernel,
        grid=(batch, nb),
        in_specs=[cur(COL_BQ), cur(COL_BK), prev(COL_BK), cur(COL_BV), prev(COL_BV),
                  pl.BlockSpec(bias.shape, lambda b, i: (0, 0, 0))],
        out_specs=pl.BlockSpec((ROWS_ATT, D_GROUP), lambda b, i: (b * nb + i, 0)),
        out_shape=jax.ShapeDtypeStruct((t, D_GROUP), BF16),
        scratch_shapes=[pltpu.VMEM((2 * ROWS_ATT, D_GROUP), BF16),
                        pltpu.VMEM((2 * ROWS_ATT, D_GROUP), BF16)],
        compiler_params=_cparams("parallel", "arbitrary"),
        name="relbias_attention",
    )(p, p, p, p, p, bias)


def _swa_kernel(sink_ref, q_ref, kvc_ref, kvp_ref, bias_ref, o_ref, kv_scr):
    i = pl.program_id(1)
    halo = C_PREV_CHUNKS * CHUNK
    kv_scr[0:halo, :] = kvp_ref[...]
    kv_scr[halo:, :] = kvc_ref[...]
    span = halo + Q_ROWS
    group = HEADS // C_KV_HEADS
    stacked = group * Q_ROWS
    key_pos = lax.broadcasted_iota(jnp.int32, (stacked, span), 1)
    q_row = lax.broadcasted_iota(jnp.int32, (stacked, 1), 0)
    first_valid = jnp.where(i > 0, 0, halo)
    scale = HEAD_DIM ** -0.5
    kv_width = C_KV_HEADS * HEAD_DIM
    for kvh in range(C_KV_HEADS):
        klo = kvh * HEAD_DIM
        h0 = kvh * group
        sink = jnp.where(q_row < Q_ROWS, sink_ref[h0], sink_ref[h0 + 1])
        bias = bias_ref[kvh]
        subs = [sub * Q_ROWS for sub in range(ROWS_ATT // Q_ROWS)]
        scores = []
        for r0 in subs:
            q = jnp.concatenate([q_ref[r0:r0 + Q_ROWS, (h0 + g) * HEAD_DIM:(h0 + g + 1) * HEAD_DIM]
                                 for g in range(group)], axis=0)
            scores.append(_dot_nt(q, kv_scr[r0:r0 + span, klo:klo + HEAD_DIM]))
        probs, invs = [], []
        for r0, s in zip(subs, scores):
            s = jnp.where(key_pos + r0 >= first_valid, s * scale + bias, -jnp.inf)
            m = jnp.maximum(jnp.max(s, axis=-1, keepdims=True), sink)
            e = jnp.exp(s - m)
            probs.append(e.astype(BF16))
            invs.append(1.0 / (jnp.sum(e, axis=-1, keepdims=True) + jnp.exp(sink - m)))
        vlo = kv_width + klo
        outs = [jnp.dot(e, kv_scr[r0:r0 + span, vlo:vlo + HEAD_DIM], preferred_element_type=F32)
                for r0, e in zip(subs, probs)]
        for r0, o, inv in zip(subs, outs, invs):
            o = o * inv
            for g in range(group):
                o_ref[r0:r0 + Q_ROWS, (h0 + g) * HEAD_DIM:(h0 + g + 1) * HEAD_DIM] = (
                    o[g * Q_ROWS:(g + 1) * Q_ROWS].astype(o_ref.dtype))


def _swa_attention(p, sinks, bias, batch, seq):
    t = p.shape[0]
    nb = seq // ROWS_ATT
    halo = C_PREV_CHUNKS * CHUNK
    halo_per_tile = ROWS_ATT // halo
    return pl.pallas_call(
        _swa_kernel,
        grid=(batch, nb),
        in_specs=[pl.BlockSpec(memory_space=pltpu.SMEM),
                  pl.BlockSpec((ROWS_ATT, D_GROUP), lambda b, i: (b * nb + i, COL_CQ)),
                  pl.BlockSpec((ROWS_ATT, D_GROUP), lambda b, i: (b * nb + i, COL_CKV)),
                  pl.BlockSpec((halo, D_GROUP),
                               lambda b, i: (jnp.maximum((b * nb + i) * halo_per_tile - 1, 0), COL_CKV)),
                  pl.BlockSpec(bias.shape, lambda b, i: (0, 0, 0))],
        out_specs=pl.BlockSpec((ROWS_ATT, D_GROUP), lambda b, i: (b * nb + i, 0)),
        out_shape=jax.ShapeDtypeStruct((t, D_GROUP), BF16),
        scratch_shapes=[pltpu.VMEM((halo + ROWS_ATT, D_GROUP), BF16)],
        compiler_params=_cparams("parallel", "arbitrary"),
        name="swa_attention",
    )(sinks, p, p, p, bias)


def _pair_mask(n_prev):
    span = n_prev * CHUNK + Q_ROWS
    r = np.arange(Q_ROWS)[:, None]
    j = np.arange(span)[None, :]
    band_pos = j - (r // CHUNK) * CHUNK
    return np.where((band_pos >= 0) & (band_pos < (n_prev + 1) * CHUNK), 0.0, -np.inf).astype(np.float32)


def _relbias_table(rel_bias):
    span = B_PREV_CHUNKS * CHUNK + Q_ROWS
    period = Q_ROWS + span
    k = np.arange(period) - (Q_ROWS - 1)
    idx = np.clip(B_PREV_CHUNKS * CHUNK - k, -REL_CLIP, REL_CLIP) + REL_CLIP
    g = rel_bias[:, idx]
    skew = jnp.tile(g, (1, Q_ROWS))[:, :Q_ROWS * (period - 1)].reshape(HEADS, Q_ROWS, period - 1)
    return skew[:, :, Q_ROWS - 1:Q_ROWS - 1 + span] + _pair_mask(B_PREV_CHUNKS)


def _alibi_table():
    halo = C_PREV_CHUNKS * CHUNK
    span = halo + Q_ROWS
    r = np.arange(Q_ROWS)[:, None]
    j = np.arange(span)[None, :]
    dist = np.abs(halo + r - j).astype(np.float32)
    per_head = [(-np.float32(s)) * dist + _pair_mask(C_PREV_CHUNKS) for s in ALIBI_SLOPES]
    group = HEADS // C_KV_HEADS
    return np.stack([np.concatenate(per_head[kv * group:(kv + 1) * group], axis=0)
                     for kv in range(C_KV_HEADS)], axis=0)


def _sgu_kernel(u_ref, v_ref, g_ref, w_ref, b_ref, o_ref):
    u = _gelu_tanh(u_ref[...].astype(F32))
    v = _gelu_tanh(v_ref[...].astype(F32))
    mu = jnp.mean(v, axis=-1, keepdims=True)
    var = jnp.mean(jnp.square(v - mu), axis=-1, keepdims=True)
    v = ((v - mu) * lax.rsqrt(var + EPS) * g_ref[...]).astype(BF16)
    row = lax.broadcasted_iota(jnp.int32, (SGU_BLOCK, SGU_BLOCK), 0)
    col = lax.broadcasted_iota(jnp.int32, (SGU_BLOCK, SGU_BLOCK), 1)
    for g in range(HEADS):
        lo, hi = g * HEAD_DIM, (g + 1) * HEAD_DIM
        w = jnp.where(row >= col, w_ref[g], 0.0).astype(BF16)
        bias = b_ref[:, g:g + 1]
        for n in range(ROWS_ATT // SGU_BLOCK):
            r0 = n * SGU_BLOCK
            mixed = jnp.dot(w, v[r0:r0 + SGU_BLOCK, lo:hi], preferred_element_type=F32) + bias
            o_ref[r0:r0 + SGU_BLOCK, lo:hi] = (u[r0:r0 + SGU_BLOCK, lo:hi] * mixed).astype(o_ref.dtype)


def _spatial_gating(p, norm_g, w_s, b_cols):
    t = p.shape[0]
    return pl.pallas_call(
        _sgu_kernel,
        grid=(t // ROWS_ATT,),
        in_specs=[pl.BlockSpec((ROWS_ATT, D_GROUP), lambda i: (i, COL_DU)),
                  pl.BlockSpec((ROWS_ATT, D_GROUP), lambda i: (i, COL_DV)),
                  pl.BlockSpec((1, D_GROUP), lambda i: (0, 0)),
                  pl.BlockSpec((HEADS, SGU_BLOCK, SGU_BLOCK), lambda i: (0, 0, 0)),
                  pl.BlockSpec((SGU_BLOCK, LANES), lambda i: (0, 0))],
        out_specs=pl.BlockSpec((ROWS_ATT, D_GROUP), lambda i: (i, 0)),
        out_shape=jax.ShapeDtypeStruct((t, D_GROUP), BF16),
        compiler_params=_cparams("parallel"),
        name="spatial_gating",
    )(p, p, norm_g.reshape(1, D_GROUP), w_s, b_cols)


def _outproj_kernel(oa_ref, ob_ref, oc_ref, od_ref, w_ref, x_ref, mod_ref, g_ref, o_ref, w_scr):
    @pl.when(pl.program_id(0) == 0)
    def _():
        w_scr[...] = w_ref[...].astype(BF16)

    y = jnp.dot(oa_ref[...], w_scr[0:D_GROUP, :], preferred_element_type=F32)
    y += jnp.dot(ob_ref[...], w_scr[D_GROUP:2 * D_GROUP, :], preferred_element_type=F32)
    y += jnp.dot(oc_ref[...], w_scr[2 * D_GROUP:3 * D_GROUP, :], preferred_element_type=F32)
    y += jnp.dot(od_ref[...], w_scr[3 * D_GROUP:, :], preferred_element_type=F32)
    o_ref[...] = x_ref[...] + mod_ref[0, 2:3, :] * _rms(y, g_ref[...])


def _out_projection(outs, w_out, x2, mod, post_g, layer, seq):
    t, d = x2.shape
    tiles_per_batch = seq // ROWS_OUT
    mix = pl.BlockSpec((ROWS_OUT, D_GROUP), lambda m: (m, 0))
    return pl.pallas_call(
        _outproj_kernel,
        grid=(t // ROWS_OUT,),
        in_specs=[mix, mix, mix, mix,
                  pl.BlockSpec((None, HEADS * D_GROUP, d), lambda m: (layer, 0, 0),
                               pipeline_mode=pl.Buffered(1)),
                  pl.BlockSpec((ROWS_OUT, d), lambda m: (m, 0)),
                  pl.BlockSpec((1, 6, d), lambda m: (m // tiles_per_batch, 0, 0)),
                  pl.BlockSpec((1, d), lambda m: (0, 0))],
        out_specs=pl.BlockSpec((ROWS_OUT, d), lambda m: (m, 0)),
        out_shape=jax.ShapeDtypeStruct((t, d), F32),
        scratch_shapes=[pltpu.VMEM((HEADS * D_GROUP, d), BF16)],
        compiler_params=_cparams("arbitrary"),
        name="out_projection",
    )(*outs, w_out, x2, mod, post_g.reshape(1, d))


def _ffn_kernel(x_ref, xp_ref, mod_ref, pre_ref, post_ref, wa_ref, wg_ref, wd_ref, cw_ref, cb_ref,
                o_ref, h_scr, a_scr, *, tiles_per_batch):
    m = pl.program_id(0)
    f = pl.program_id(1)

    @pl.when(f == 0)
    def _():
        gain = pre_ref[...] * (1.0 + mod_ref[0, 4:5, :])
        shift = mod_ref[0, 3:4, :]
        h_scr[0:HALO, :] = _modulated_rms(xp_ref[...], gain, shift).astype(BF16)
        h_scr[HALO:, :] = _modulated_rms(x_ref[...], gain, shift).astype(BF16)
        o_ref[...] = jnp.zeros_like(o_ref)

    keep = (m % tiles_per_batch != 0).astype(F32)
    a = jnp.dot(h_scr[...], wa_ref[...], preferred_element_type=F32)
    a_scr[0:HALO, :] = a[0:HALO] * keep
    a_scr[HALO:, :] = a[HALO:]
    gate = jnp.dot(h_scr[HALO:, :], wg_ref[...], preferred_element_type=F32)
    acc = cb_ref[...]
    for tap in range(FFN_CONV):
        off = HALO - (FFN_CONV - 1) + tap
        acc = acc + cw_ref[tap:tap + 1, :] * a_scr[off:off + ROWS_OUT, :]
    act = (_gelu_tanh(acc) * gate).astype(BF16)
    o_ref[...] += jnp.dot(act, wd_ref[...], preferred_element_type=F32)

    @pl.when(f == pl.num_programs(1) - 1)
    def _():
        o_ref[...] = x_ref[...] + mod_ref[0, 5:6, :] * _rms(o_ref[...], post_ref[...])


def _conv_ffn(x2, mod, pre_g, post_g, w_a, w_g, w_d, conv_w, conv_b, layer, seq):
    t, d = x2.shape
    tiles_per_batch = seq // ROWS_OUT
    halo_per_tile = ROWS_OUT // HALO
    return pl.pallas_call(
        functools.partial(_ffn_kernel, tiles_per_batch=tiles_per_batch),
        grid=(t // ROWS_OUT, D_FF_PAD // D_GROUP),
        in_specs=[pl.BlockSpec((ROWS_OUT, d), lambda m, f: (m, 0)),
                  pl.BlockSpec((HALO, d), lambda m, f: (jnp.maximum(m * halo_per_tile - 1, 0), 0)),
                  pl.BlockSpec((1, 6, d), lambda m, f: (m // tiles_per_batch, 0, 0)),
                  pl.BlockSpec((1, d), lambda m, f: (0, 0)),
                  pl.BlockSpec((1, d), lambda m, f: (0, 0)),
                  pl.BlockSpec((None, d, D_GROUP), lambda m, f: (layer, 0, f)),
                  pl.BlockSpec((None, d, D_GROUP), lambda m, f: (layer, 0, f)),
                  pl.BlockSpec((None, D_GROUP, d), lambda m, f: (layer, f, 0)),
                  pl.BlockSpec((None, FFN_CONV, D_GROUP), lambda m, f: (layer, 0, f)),
                  pl.BlockSpec((None, 1, D_GROUP), lambda m, f: (layer, 0, f))],
        out_specs=pl.BlockSpec((ROWS_OUT, d), lambda m, f: (m, 0)),
        out_shape=jax.ShapeDtypeStruct((t, d), F32),
        scratch_shapes=[pltpu.VMEM((HALO + ROWS_OUT, d), BF16),
                        pltpu.VMEM((HALO + ROWS_OUT, D_GROUP), F32)],
        compiler_params=_cparams("parallel", "arbitrary"),
        name="conv_ffn",
    )(x2, x2, mod, pre_g.reshape(1, d), post_g.reshape(1, d), w_a, w_g, w_d, conv_w, conv_b)


def _ffn_up_prep_kernel(a_ref, g0_ref, g1_ref, g2_ref, g3_ref, ao_ref, go_ref):
    col = pl.program_id(1) * D_GROUP + lax.broadcasted_iota(jnp.int32, (1, D_GROUP), 1)
    valid = col < D_FF
    gate = jnp.concatenate([g0_ref[...], g1_ref[...], g2_ref[...], g3_ref[...]], axis=1)
    ao_ref[...] = jnp.where(valid, a_ref[...], 0.0).astype(BF16)
    go_ref[...] = jnp.where(valid, gate, 0.0).astype(BF16)


def _ffn_down_prep_kernel(r0_ref, r1_ref, r2_ref, r3_ref, o_ref):
    row = pl.program_id(1) * D_GROUP + lax.broadcasted_iota(jnp.int32, (D_GROUP, 1), 0)
    rows = jnp.concatenate([r0_ref[...], r1_ref[...], r2_ref[...], r3_ref[...]], axis=0)
    o_ref[...] = jnp.where(row < D_FF, rows, 0.0).astype(BF16)


def _ffn_weight_layout(w_up, w_down):
    depth, d, _ = w_up.shape
    per_tile = D_GROUP // LANES
    n_in = D_FF // LANES
    tiles = D_FF_PAD // D_GROUP

    def gate_spec(k):
        return pl.BlockSpec((None, d, LANES),
                            lambda l, j: (l, 0, jnp.minimum(n_in + per_tile * j + k, 2 * n_in - 1)))

    def row_spec(k):
        return pl.BlockSpec((None, LANES, d), lambda l, j: (l, jnp.minimum(per_tile * j + k, n_in - 1), 0))

    out_cols = pl.BlockSpec((None, d, D_GROUP), lambda l, j: (l, 0, j))
    w_a, w_g = pl.pallas_call(
        _ffn_up_prep_kernel,
        grid=(depth, tiles),
        in_specs=[pl.BlockSpec((None, d, D_GROUP), lambda l, j: (l, 0, j))] + [gate_spec(k) for k in range(per_tile)],
        out_specs=[out_cols, out_cols],
        out_shape=[jax.ShapeDtypeStruct((depth, d, D_FF_PAD), BF16)] * 2,
        compiler_params=_cparams("parallel", "parallel"),
        name="ffn_up_layout",
    )(w_up, w_up, w_up, w_up, w_up)
    w_d = pl.pallas_call(
        _ffn_down_prep_kernel,
        grid=(depth, tiles),
        in_specs=[row_spec(k) for k in range(per_tile)],
        out_specs=pl.BlockSpec((None, D_GROUP, d), lambda l, j: (l, j, 0)),
        out_shape=jax.ShapeDtypeStruct((depth, D_FF_PAD, d), BF16),
        compiler_params=_cparams("parallel", "parallel"),
        name="ffn_down_layout",
    )(w_down, w_down, w_down, w_down)
    return w_a, w_g, w_d


def _pad_last(a, width):
    return jnp.pad(a, [(0, 0)] * (a.ndim - 1) + [(0, width - a.shape[-1])])


def _layer(x2, mod, layer, batch, seq, weights, mix_pre_g, mix_post_g, dn_conv_w, dn_a_log, dn_dt_bias,
           dn_norm_g, rel_bias, sinks, sgu_norm_g, sgu_w, sgu_b, ffn_pre_g, ffn_post_g):
    t = x2.shape[0]
    p, small, small_t = _in_projection(x2, mod, mix_pre_g, weights["in_full"], weights["in_tail"],
                                       weights["in_small"], layer, seq)
    small_t3 = small_t.reshape(SUBLANES, t // CHUNK, CHUNK).transpose(1, 0, 2)

    prm = jnp.stack([dn_dt_bias, dn_a_log], axis=0)
    prm_row = jnp.zeros((SUBLANES, LANES), F32).at[0:2, HEADS:2 * HEADS].set(prm)
    prm_col = jnp.zeros((SUBLANES, LANES), F32).at[HEADS:2 * HEADS, 0:2].set(prm.T)
    out_a = _gated_deltanet(p, small, small_t3, dn_conv_w, prm_row, prm_col, dn_norm_g, batch, seq)
    out_b = _relbias_attention(p, _relbias_table(rel_bias), batch, seq)
    out_c = _swa_attention(p, sinks, jnp.asarray(_alibi_table()), batch, seq)
    out_d = _spatial_gating(p, sgu_norm_g, sgu_w, _pad_last(sgu_b.T, LANES))

    x2 = _out_projection((out_a, out_b, out_c, out_d), weights["out"], x2, mod, mix_post_g, layer, seq)
    return _conv_ffn(x2, mod, ffn_pre_g, ffn_post_g, weights["ffn_a"], weights["ffn_g"], weights["ffn_d"],
                     weights["ffn_conv_w"], weights["ffn_conv_b"], layer, seq)


def _prepare_weights(w_in, w_out, ffn_w_up, ffn_conv_w, ffn_conv_b, ffn_w_down):
    small_lo = N_LO_TILES * D_GROUP
    w_a, w_g, w_d = _ffn_weight_layout(ffn_w_up, ffn_w_down)
    return {
        "in_full": w_in,
        "in_tail": w_in[:, :, small_lo + N_SMALL:],
        "in_small": _pad_last(w_in[:, :, small_lo:small_lo + N_SMALL], LANES).astype(BF16),
        "out": w_out,
        "ffn_a": w_a,
        "ffn_g": w_g,
        "ffn_d": w_d,
        "ffn_conv_w": _pad_last(ffn_conv_w, D_FF_PAD),
        "ffn_conv_b": _pad_last(ffn_conv_b[:, None, :], D_FF_PAD),
    }


def kernel(x, c, ada_w, ada_b, mix_pre_g, mix_post_g, w_in, dn_conv_w, dn_a_log, dn_dt_bias, dn_norm_g,
           rel_bias, sinks, sgu_norm_g, sgu_w, sgu_b, w_out, ffn_pre_g, ffn_post_g, ffn_w_up, ffn_conv_w,
           ffn_conv_b, ffn_w_down):
    batch, seq, d = x.shape
    depth = ada_w.shape[0]
    assert d == D_MODEL and seq % ROWS_PROJ == 0 and batch <= SUBLANES
    c_pad = jnp.pad(c, ((0, SUBLANES - batch), (0, 0)))
    mod_all = _modulation(c_pad, ada_w, ada_b)[:, :batch].reshape(depth, batch, 6, d)
    weights = _prepare_weights(w_in, w_out, ffn_w_up, ffn_conv_w, ffn_conv_b, ffn_w_down)
    x2 = x.reshape(batch * seq, d)
    for l in range(depth):
        x2 = _layer(x2, mod_all[l], l, batch, seq, weights, mix_pre_g[l], mix_post_g[l], dn_conv_w[l],
                    dn_a_log[l], dn_dt_bias[l], dn_norm_g[l], rel_bias[l], sinks[l], sgu_norm_g[l],
                    sgu_w[l], sgu_b[l], ffn_pre_g[l], ffn_post_g[l])
    return x2.reshape(batch, seq, d)
```

```python
import functools

import numpy as np
import jax
import jax.numpy as jnp
from jax import lax
from jax.experimental import pallas as pl
from jax.experimental.pallas import tpu as pltpu

F32 = jnp.float32
BF16 = jnp.bfloat16

D_MODEL = 2048
CHUNK = 64
HEAD_DIM = 128
HEADS = 4
D_GROUP = HEADS * HEAD_DIM
DN_CONV = 4
B_PREV_CHUNKS = 8
REL_CLIP = 256
C_PREV_CHUNKS = 2
C_KV_HEADS = 2
SGU_BLOCK = 128
D_FF = 5504
FFN_CONV = 3
EPS = 1e-6

LANES = 128
SUBLANES = 8
VMEM_LIMIT_BYTES = 56 * 1024 * 1024

N_SMALL = 2 * HEADS
N_MAIN = 11 * D_GROUP
N_LO_TILES = 4
COL_AQ, COL_AK, COL_AV, COL_AG, COL_BQ, COL_BK, COL_BV, COL_CQ, COL_CKV, COL_DU, COL_DV = range(11)
D_FF_PAD = 11 * D_GROUP

ROWS_ATT = 512
Q_ROWS = 2 * CHUNK
DN_CHUNKS = 4
DN_ROWS = DN_CHUNKS * CHUNK
HALO = 16
ROWS_PROJ = 2048
ROWS_OUT = 512
MOD_COLS = 1024

ALIBI_SLOPES = tuple(float(2.0 ** (-8.0 * h / HEADS)) for h in range(1, HEADS + 1))


def _cparams(*sem):
    return pltpu.CompilerParams(dimension_semantics=sem, vmem_limit_bytes=VMEM_LIMIT_BYTES)


def _sigmoid(x):
    return 1.0 / (1.0 + jnp.exp(-x))


def _silu(x):
    return x * _sigmoid(x)


def _softplus(x):
    return jnp.maximum(x, 0.0) + jnp.log1p(jnp.exp(-jnp.abs(x)))


def _gelu_tanh(x):
    return 0.5 * x * (1.0 + jnp.tanh(float(np.sqrt(2.0 / np.pi)) * (x + 0.044715 * (x * x * x))))


def _rms(x, g):
    return x * lax.rsqrt(jnp.mean(x * x, axis=-1, keepdims=True) + EPS) * g


def _modulated_rms(x, gain, shift):
    return x * lax.rsqrt(jnp.mean(x * x, axis=-1, keepdims=True) + EPS) * gain + shift


def _dot(a, b):
    return jnp.dot(a.astype(BF16), b.astype(BF16), preferred_element_type=F32)


def _dot_nt(a, b):
    return lax.dot_general(a.astype(BF16), b.astype(BF16), (((1,), (1,)), ((), ())),
                           preferred_element_type=F32)


def _dot_f32(a, b):
    return jnp.dot(a, b, precision=lax.Precision.HIGHEST, preferred_element_type=F32)


def _mod_kernel(c_ref, w_ref, b_ref, o_ref):
    cond = _silu(c_ref[...])
    o_ref[0] = _dot(cond, w_ref[0]) + b_ref[0]


def _modulation(c_pad, ada_w, ada_b):
    depth, d, n = ada_w.shape
    return pl.pallas_call(
        _mod_kernel,
        grid=(depth, n // MOD_COLS),
        in_specs=[pl.BlockSpec((SUBLANES, d), lambda l, j: (0, 0)),
                  pl.BlockSpec((1, d, MOD_COLS), lambda l, j: (l, 0, j)),
                  pl.BlockSpec((1, 1, MOD_COLS), lambda l, j: (l, 0, j))],
        out_specs=pl.BlockSpec((1, SUBLANES, MOD_COLS), lambda l, j: (l, 0, j)),
        out_shape=jax.ShapeDtypeStruct((depth, SUBLANES, n), F32),
        compiler_params=_cparams("parallel", "parallel"),
        name="modulation",
    )(c_pad, ada_w, ada_b.reshape(depth, 1, n))


def _inproj_kernel(x_ref, mod_ref, g_ref, wlo_ref, whi_ref, ws_ref, p_ref, s_ref, st_ref, h_scr):
    n = pl.program_id(1)

    @pl.when(n == 0)
    def _():
        gain = g_ref[...] * (1.0 + mod_ref[0, 1:2, :])
        hb = _modulated_rms(x_ref[...], gain, mod_ref[0, 0:1, :]).astype(BF16)
        h_scr[...] = hb
        small_t = _dot_nt(ws_ref[...], hb)
        s_ref[...] = small_t.T
        st_ref[...] = small_t[0:SUBLANES, :]

    @pl.when(n < N_LO_TILES)
    def _():
        p_ref[...] = _dot_nt(h_scr[...], wlo_ref[...]).astype(p_ref.dtype)

    @pl.when(n >= N_LO_TILES)
    def _():
        p_ref[...] = _dot_nt(h_scr[...], whi_ref[0]).astype(p_ref.dtype)


def _in_projection(x2, mod, pre_g, w_full, w_tail, w_small, layer, seq):
    t, d = x2.shape
    tiles_per_batch = seq // ROWS_PROJ
    return pl.pallas_call(
        _inproj_kernel,
        grid=(t // ROWS_PROJ, N_MAIN // D_GROUP),
        in_specs=[pl.BlockSpec((ROWS_PROJ, d), lambda m, n: (m, 0), pipeline_mode=pl.Buffered(1)),
                  pl.BlockSpec((1, 6, d), lambda m, n: (m // tiles_per_batch, 0, 0)),
                  pl.BlockSpec((1, d), lambda m, n: (0, 0)),
                  pl.BlockSpec((None, D_GROUP, d), lambda m, n: (layer, jnp.minimum(n, N_LO_TILES - 1), 0)),
                  pl.BlockSpec((pl.Element(1), pl.Element(D_GROUP), pl.Element(d)),
                               lambda m, n: (layer, pl.multiple_of(
                                   N_SMALL + D_GROUP * jnp.maximum(n, N_LO_TILES), SUBLANES), 0)),
                  pl.BlockSpec((None, LANES, d), lambda m, n: (layer, 0, 0))],
        out_specs=[pl.BlockSpec((ROWS_PROJ, D_GROUP), lambda m, n: (m, n)),
                   pl.BlockSpec((ROWS_PROJ, LANES), lambda m, n: (m, 0)),
                   pl.BlockSpec((SUBLANES, ROWS_PROJ), lambda m, n: (0, m))],
        out_shape=[jax.ShapeDtypeStruct((t, N_MAIN), BF16),
                   jax.ShapeDtypeStruct((t, LANES), F32),
                   jax.ShapeDtypeStruct((SUBLANES, t), F32)],
        scratch_shapes=[pltpu.VMEM((ROWS_PROJ, d), BF16)],
        compiler_params=_cparams("parallel", "arbitrary"),
        name="in_projection",
    )(x2, mod, pre_g.reshape(1, d), w_full, w_tail, w_small)


def _unit_lower_inverse(mats, row, col):
    blk_r, blk_c = jnp.right_shift(row, 4), jnp.right_shift(col, 4)
    half_r, half_c = jnp.right_shift(row, 5), jnp.right_shift(col, 5)
    eye = (row == col).astype(F32)
    a_diag = [jnp.where(blk_r == blk_c, a, 0.0) for a in mats]
    a_pair = [jnp.where(half_r == half_c, a, 0.0) - d for a, d in zip(mats, a_diag)]
    a_far = [jnp.where(half_r != half_c, a, 0.0) for a in mats]
    p2 = [_dot(d, d) for d in a_diag]
    t = [eye - d for d in a_diag]
    t = [x + _dot(x, p) for x, p in zip(t, p2)]
    p4 = [_dot(p, p) for p in p2]
    t = [x + _dot(x, p) for x, p in zip(t, p4)]
    p8 = [_dot(p, p) for p in p4]
    t = [x + _dot(x, p) for x, p in zip(t, p8)]
    for off in (a_pair, a_far):
        left = [_dot(x, a) for x, a in zip(t, off)]
        t = [x - _dot(l, x) for x, l in zip(t, left)]
    return t


def _delta_kernel(q_ref, k_ref, v_ref, qp_ref, kp_ref, vp_ref, gate_ref, sm_ref, smt_ref, cw_ref,
                  prow_ref, pcol_ref, ng_ref, o_ref, state_scr, xe_scr):
    step = pl.program_id(1)

    @pl.when(step == 0)
    def _():
        state_scr[...] = jnp.zeros_like(state_scr)

    keep = (step > 0).astype(F32)
    halo = jnp.concatenate([qp_ref[...], kp_ref[...], vp_ref[...]], axis=1).astype(F32)
    xe_scr[0:HALO, :] = halo * keep
    xe_scr[HALO:, 0:D_GROUP] = q_ref[...].astype(F32)
    xe_scr[HALO:, D_GROUP:2 * D_GROUP] = k_ref[...].astype(F32)
    xe_scr[HALO:, 2 * D_GROUP:] = v_ref[...].astype(F32)
    acc = None
    for tap in range(DN_CONV):
        off = HALO - (DN_CONV - 1) + tap
        term = cw_ref[tap:tap + 1, :] * xe_scr[off:off + DN_ROWS, :]
        acc = term if acc is None else acc + term
    qkv = _silu(acc)

    row = lax.broadcasted_iota(jnp.int32, (CHUNK, CHUNK), 0)
    col = lax.broadcasted_iota(jnp.int32, (CHUNK, CHUNK), 1)
    causal = row >= col
    strict = row > col
    lower_ones = causal.astype(F32)
    upper_ones = (row <= col).astype(F32)

    sm = sm_ref[...]
    la_cols = -jnp.exp(prow_ref[1:2, :]) * _softplus(sm + prow_ref[0:1, :])
    beta_cols = _sigmoid(sm)
    neg_a_col = -jnp.exp(pcol_ref[:, 1:2])
    dt_col = pcol_ref[:, 0:1]

    scale = HEAD_DIM ** -0.5
    qs, ks, vbs, kbs = [], [], [], []
    for h in range(HEADS):
        lo, hi = h * HEAD_DIM, (h + 1) * HEAD_DIM
        q = qkv[:, lo:hi]
        k = qkv[:, D_GROUP + lo:D_GROUP + hi]
        v = qkv[:, 2 * D_GROUP + lo:2 * D_GROUP + hi]
        beta = beta_cols[:, h:h + 1]
        k = k * lax.rsqrt(jnp.sum(k * k, axis=-1, keepdims=True) + EPS)
        qs.append(q * lax.rsqrt(jnp.sum(q * q, axis=-1, keepdims=True) + EPS) * scale)
        ks.append(k)
        kbs.append(k * beta)
        vbs.append(v * beta)

    probs = [(c, h) for c in range(DN_CHUNKS) for h in range(HEADS)]
    g_cols, g_rows = [], []
    for c in range(DN_CHUNKS):
        rows = slice(c * CHUNK, (c + 1) * CHUNK)
        g_cols.append(_dot_f32(lower_ones, la_cols[rows]))
        g_rows.append(_dot_f32(neg_a_col * _softplus(smt_ref[c] + dt_col), upper_ones))

    def rows_of(c):
        return slice(c * CHUNK, (c + 1) * CHUNK)

    g_col = [g_cols[c][:, HEADS + h:HEADS + h + 1] for c, h in probs]
    g_last = [g[CHUNK - 1:CHUNK, :] for g in g_col]
    decay = [jnp.exp(jnp.where(causal, g - g_rows[c][HEADS + h:HEADS + h + 1, :], -jnp.inf))
             for g, (c, h) in zip(g_col, probs)]
    kk = [_dot_nt(kbs[h][rows_of(c)], ks[h][rows_of(c)]) for c, h in probs]
    qk = [_dot_nt(qs[h][rows_of(c)], ks[h][rows_of(c)]) for c, h in probs]
    t_inv = _unit_lower_inverse([jnp.where(strict, x * d, 0.0) for x, d in zip(kk, decay)], row, col)
    exp_g = [jnp.exp(g) for g in g_col]
    uw = [_dot(t, jnp.concatenate([vbs[h][rows_of(c)], kbs[h][rows_of(c)] * e], axis=1))
          for t, e, (c, h) in zip(t_inv, exp_g, probs)]
    attn = [(x * d).astype(BF16) for x, d in zip(qk, decay)]
    wq = [jnp.concatenate([x[:, HEAD_DIM:], qs[h][rows_of(c)] * e], axis=0).astype(BF16)
          for x, e, (c, h) in zip(uw, exp_g, probs)]
    k_dec_t = [(ks[h][rows_of(c)] * jnp.exp(gl - g)).T.astype(BF16)
               for g, gl, (c, h) in zip(g_col, g_last, probs)]
    decay_last = [jnp.exp(gl) for gl in g_last]

    state = [state_scr[h] for h in range(HEADS)]
    for c in range(DN_CHUNKS):
        idx = [c * HEADS + h for h in range(HEADS)]
        ws_qs = [jnp.dot(wq[i], state[h].astype(BF16), preferred_element_type=F32)
                 for h, i in enumerate(idx)]
        v_new = [(uw[i][:, :HEAD_DIM] - x[:CHUNK]).astype(BF16) for x, i in zip(ws_qs, idx)]
        state = [state[h] * decay_last[i] + jnp.dot(k_dec_t[i], v_new[h], preferred_element_type=F32)
                 for h, i in enumerate(idx)]
        for h, i in enumerate(idx):
            lo, hi = h * HEAD_DIM, (h + 1) * HEAD_DIM
            o = ws_qs[h][CHUNK:] + jnp.dot(attn[i], v_new[h], preferred_element_type=F32)
            o = _rms(o, ng_ref[...]) * _silu(gate_ref[rows_of(c), lo:hi].astype(F32))
            o_ref[rows_of(c), lo:hi] = o.astype(o_ref.dtype)
    for h in range(HEADS):
        state_scr[h] = state[h]


def _gated_deltanet(p, small, small_t3, conv_w, prm_row, prm_col, norm_g, batch, seq):
    t = p.shape[0]
    steps = seq // DN_ROWS
    halo_per_step = DN_ROWS // HALO

    def cur(j):
        return pl.BlockSpec((DN_ROWS, D_GROUP), lambda b, s: (b * steps + s, j))

    def prev(j):
        return pl.BlockSpec((HALO, D_GROUP),
                            lambda b, s: (jnp.maximum((b * steps + s) * halo_per_step - 1, 0), j))

    return pl.pallas_call(
        _delta_kernel,
        grid=(batch, steps),
        in_specs=[cur(COL_AQ), cur(COL_AK), cur(COL_AV), prev(COL_AQ), prev(COL_AK), prev(COL_AV),
                  cur(COL_AG),
                  pl.BlockSpec((DN_ROWS, LANES), lambda b, s: (b * steps + s, 0)),
                  pl.BlockSpec((DN_CHUNKS, SUBLANES, CHUNK), lambda b, s: (b * steps + s, 0, 0)),
                  pl.BlockSpec((DN_CONV, 3 * D_GROUP), lambda b, s: (0, 0)),
                  pl.BlockSpec((SUBLANES, LANES), lambda b, s: (0, 0)),
                  pl.BlockSpec((SUBLANES, LANES), lambda b, s: (0, 0)),
                  pl.BlockSpec((1, HEAD_DIM), lambda b, s: (0, 0))],
        out_specs=pl.BlockSpec((DN_ROWS, D_GROUP), lambda b, s: (b * steps + s, 0)),
        out_shape=jax.ShapeDtypeStruct((t, D_GROUP), BF16),
        scratch_shapes=[pltpu.VMEM((HEADS, HEAD_DIM, HEAD_DIM), F32),
                        pltpu.VMEM((HALO + DN_ROWS, 3 * D_GROUP), F32)],
        compiler_params=_cparams("parallel", "arbitrary"),
        name="gated_deltanet",
    )(p, p, p, p, p, p, p, small, small_t3, conv_w, prm_row, prm_col, norm_g.reshape(1, HEAD_DIM))


def _relbias_kernel(q_ref, kc_ref, kp_ref, vc_ref, vp_ref, bias_ref, o_ref, k_scr, v_scr):
    i = pl.program_id(1)
    k_scr[0:ROWS_ATT, :] = kp_ref[...]
    k_scr[ROWS_ATT:, :] = kc_ref[...]
    v_scr[0:ROWS_ATT, :] = vp_ref[...]
    v_scr[ROWS_ATT:, :] = vc_ref[...]
    span = B_PREV_CHUNKS * CHUNK + Q_ROWS
    key_pos = lax.broadcasted_iota(jnp.int32, (Q_ROWS, span), 1)
    first_valid = jnp.where(i > 0, 0, ROWS_ATT)
    scale = HEAD_DIM ** -0.5
    subs = [sub * Q_ROWS for sub in range(ROWS_ATT // Q_ROWS)]
    for h in range(HEADS):
        lo, hi = h * HEAD_DIM, (h + 1) * HEAD_DIM
        bias = bias_ref[h]
        scores = [_dot_nt(q_ref[r0:r0 + Q_ROWS, lo:hi], k_scr[r0:r0 + span, lo:hi]) for r0 in subs]
        probs, invs = [], []
        for r0, s in zip(subs, scores):
            s = jnp.where(key_pos + r0 >= first_valid, s * scale + bias, -jnp.inf)
            e = jnp.exp(s - jnp.max(s, axis=-1, keepdims=True))
            probs.append(e.astype(BF16))
            invs.append(1.0 / jnp.sum(e, axis=-1, keepdims=True))
        outs = [jnp.dot(e, v_scr[r0:r0 + span, lo:hi], preferred_element_type=F32)
                for r0, e in zip(subs, probs)]
        for r0, o, inv in zip(subs, outs, invs):
            o_ref[r0:r0 + Q_ROWS, lo:hi] = (o * inv).astype(o_ref.dtype)


def _relbias_attention(p, bias, batch, seq):
    t = p.shape[0]
    nb = seq // ROWS_ATT

    def cur(j):
        return pl.BlockSpec((ROWS_ATT, D_GROUP), lambda b, i: (b * nb + i, j))

    def prev(j):
        return pl.BlockSpec((ROWS_ATT, D_GROUP), lambda b, i: (b * nb + jnp.maximum(i - 1, 0), j))

    return pl.pallas_call(
        _relbias_kernel,
        grid=(batch, nb),
        in_specs=[cur(COL_BQ), cur(COL_BK), prev(COL_BK), cur(COL_BV), prev(COL_BV),
                  pl.BlockSpec(bias.shape, lambda b, i: (0, 0, 0))],
        out_specs=pl.BlockSpec((ROWS_ATT, D_GROUP), lambda b, i: (b * nb + i, 0)),
        out_shape=jax.ShapeDtypeStruct((t, D_GROUP), BF16),
        scratch_shapes=[pltpu.VMEM((2 * ROWS_ATT, D_GROUP), BF16),
                        pltpu.VMEM((2 * ROWS_ATT, D_GROUP), BF16)],
        compiler_params=_cparams("parallel", "arbitrary"),
        name="relbias_attention",
    )(p, p, p, p, p, bias)


def _swa_kernel(sink_ref, q_ref, kvc_ref, kvp_ref, bias_ref, o_ref, kv_scr):
    i = pl.program_id(1)
    halo = C_PREV_CHUNKS * CHUNK
    kv_scr[0:halo, :] = kvp_ref[...]
    kv_scr[halo:, :] = kvc_ref[...]
    span = halo + Q_ROWS
    group = HEADS // C_KV_HEADS
    stacked = group * Q_ROWS
    key_pos = lax.broadcasted_iota(jnp.int32, (stacked, span), 1)
    q_row = lax.broadcasted_iota(jnp.int32, (stacked, 1), 0)
    first_valid = jnp.where(i > 0, 0, halo)
    scale = HEAD_DIM ** -0.5
    kv_width = C_KV_HEADS * HEAD_DIM
    for kvh in range(C_KV_HEADS):
        klo = kvh * HEAD_DIM
        h0 = kvh * group
        sink = jnp.where(q_row < Q_ROWS, sink_ref[h0], sink_ref[h0 + 1])
        bias = bias_ref[kvh]
        subs = [sub * Q_ROWS for sub in range(ROWS_ATT // Q_ROWS)]
        scores = []
        for r0 in subs:
            q = jnp.concatenate([q_ref[r0:r0 + Q_ROWS, (h0 + g) * HEAD_DIM:(h0 + g + 1) * HEAD_DIM]
                                 for g in range(group)], axis=0)
            scores.append(_dot_nt(q, kv_scr[r0:r0 + span, klo:klo + HEAD_DIM]))
        probs, invs = [], []
        for r0, s in zip(subs, scores):
            s = jnp.where(key_pos + r0 >= first_valid, s * scale + bias, -jnp.inf)
            m = jnp.maximum(jnp.max(s, axis=-1, keepdims=True), sink)
            e = jnp.exp(s - m)
            probs.append(e.astype(BF16))
            invs.append(1.0 / (jnp.sum(e, axis=-1, keepdims=True) + jnp.exp(sink - m)))
        vlo = kv_width + klo
        outs = [jnp.dot(e, kv_scr[r0:r0 + span, vlo:vlo + HEAD_DIM], preferred_element_type=F32)
                for r0, e in zip(subs, probs)]
        for r0, o, inv in zip(subs, outs, invs):
            o = o * inv
            for g in range(group):
                o_ref[r0:r0 + Q_ROWS, (h0 + g) * HEAD_DIM:(h0 + g + 1) * HEAD_DIM] = (
                    o[g * Q_ROWS:(g + 1) * Q_ROWS].astype(o_ref.dtype))


def _swa_attention(p, sinks, bias, batch, seq):
    t = p.shape[0]
    nb = seq // ROWS_ATT
    halo = C_PREV_CHUNKS * CHUNK
    halo_per_tile = ROWS_ATT // halo
    return pl.pallas_call(
        _swa_kernel,
        grid=(batch, nb),
        in_specs=[pl.BlockSpec(memory_space=pltpu.SMEM),
                  pl.BlockSpec((ROWS_ATT, D_GROUP), lambda b, i: (b * nb + i, COL_CQ)),
                  pl.BlockSpec((ROWS_ATT, D_GROUP), lambda b, i: (b * nb + i, COL_CKV)),
                  pl.BlockSpec((halo, D_GROUP),
                               lambda b, i: (jnp.maximum((b * nb + i) * halo_per_tile - 1, 0), COL_CKV)),
                  pl.BlockSpec(bias.shape, lambda b, i: (0, 0, 0))],
        out_specs=pl.BlockSpec((ROWS_ATT, D_GROUP), lambda b, i: (b * nb + i, 0)),
        out_shape=jax.ShapeDtypeStruct((t, D_GROUP), BF16),
        scratch_shapes=[pltpu.VMEM((halo + ROWS_ATT, D_GROUP), BF16)],
        compiler_params=_cparams("parallel", "arbitrary"),
        name="swa_attention",
    )(sinks, p, p, p, bias)


def _pair_mask(n_prev):
    span = n_prev * CHUNK + Q_ROWS
    r = np.arange(Q_ROWS)[:, None]
    j = np.arange(span)[None, :]
    band_pos = j - (r // CHUNK) * CHUNK
    return np.where((band_pos >= 0) & (band_pos < (n_prev + 1) * CHUNK), 0.0, -np.inf).astype(np.float32)


def _relbias_table(rel_bias):
    span = B_PREV_CHUNKS * CHUNK + Q_ROWS
    period = Q_ROWS + span
    k = np.arange(period) - (Q_ROWS - 1)
    idx = np.clip(B_PREV_CHUNKS * CHUNK - k, -REL_CLIP, REL_CLIP) + REL_CLIP
    g = rel_bias[:, idx]
    skew = jnp.tile(g, (1, Q_ROWS))[:, :Q_ROWS * (period - 1)].reshape(HEADS, Q_ROWS, period - 1)
    return skew[:, :, Q_ROWS - 1:Q_ROWS - 1 + span] + _pair_mask(B_PREV_CHUNKS)


def _alibi_table():
    halo = C_PREV_CHUNKS * CHUNK
    span = halo + Q_ROWS
    r = np.arange(Q_ROWS)[:, None]
    j = np.arange(span)[None, :]
    dist = np.abs(halo + r - j).astype(np.float32)
    per_head = [(-np.float32(s)) * dist + _pair_mask(C_PREV_CHUNKS) for s in ALIBI_SLOPES]
    group = HEADS // C_KV_HEADS
    return np.stack([np.concatenate(per_head[kv * group:(kv + 1) * group], axis=0)
                     for kv in range(C_KV_HEADS)], axis=0)


def _sgu_kernel(u_ref, v_ref, g_ref, w_ref, b_ref, o_ref):
    u = _gelu_tanh(u_ref[...].astype(F32))
    v = _gelu_tanh(v_ref[...].astype(F32))
    mu = jnp.mean(v, axis=-1, keepdims=True)
    var = jnp.mean(jnp.square(v - mu), axis=-1, keepdims=True)
    v = ((v - mu) * lax.rsqrt(var + EPS) * g_ref[...]).astype(BF16)
    row = lax.broadcasted_iota(jnp.int32, (SGU_BLOCK, SGU_BLOCK), 0)
    col = lax.broadcasted_iota(jnp.int32, (SGU_BLOCK, SGU_BLOCK), 1)
    for g in range(HEADS):
        lo, hi = g * HEAD_DIM, (g + 1) * HEAD_DIM
        w = jnp.where(row >= col, w_ref[g], 0.0).astype(BF16)
        bias = b_ref[:, g:g + 1]
        for n in range(ROWS_ATT // SGU_BLOCK):
            r0 = n * SGU_BLOCK
            mixed = jnp.dot(w, v[r0:r0 + SGU_BLOCK, lo:hi], preferred_element_type=F32) + bias
            o_ref[r0:r0 + SGU_BLOCK, lo:hi] = (u[r0:r0 + SGU_BLOCK, lo:hi] * mixed).astype(o_ref.dtype)


def _spatial_gating(p, norm_g, w_s, b_cols):
    t = p.shape[0]
    return pl.pallas_call(
        _sgu_kernel,
        grid=(t // ROWS_ATT,),
        in_specs=[pl.BlockSpec((ROWS_ATT, D_GROUP), lambda i: (i, COL_DU)),
                  pl.BlockSpec((ROWS_ATT, D_GROUP), lambda i: (i, COL_DV)),
                  pl.BlockSpec((1, D_GROUP), lambda i: (0, 0)),
                  pl.BlockSpec((HEADS, SGU_BLOCK, SGU_BLOCK), lambda i: (0, 0, 0)),
                  pl.BlockSpec((SGU_BLOCK, LANES), lambda i: (0, 0))],
        out_specs=pl.BlockSpec((ROWS_ATT, D_GROUP), lambda i: (i, 0)),
        out_shape=jax.ShapeDtypeStruct((t, D_GROUP), BF16),
        compiler_params=_cparams("parallel"),
        name="spatial_gating",
    )(p, p, norm_g.reshape(1, D_GROUP), w_s, b_cols)


def _outproj_kernel(oa_ref, ob_ref, oc_ref, od_ref, w_ref, x_ref, mod_ref, g_ref, o_ref, w_scr):
    @pl.when(pl.program_id(0) == 0)
    def _():
        w_scr[...] = w_ref[...].astype(BF16)

    y = jnp.dot(oa_ref[...], w_scr[0:D_GROUP, :], preferred_element_type=F32)
    y += jnp.dot(ob_ref[...], w_scr[D_GROUP:2 * D_GROUP, :], preferred_element_type=F32)
    y += jnp.dot(oc_ref[...], w_scr[2 * D_GROUP:3 * D_GROUP, :], preferred_element_type=F32)
    y += jnp.dot(od_ref[...], w_scr[3 * D_GROUP:, :], preferred_element_type=F32)
    o_ref[...] = x_ref[...] + mod_ref[0, 2:3, :] * _rms(y, g_ref[...])


def _out_projection(outs, w_out, x2, mod, post_g, layer, seq):
    t, d = x2.shape
    tiles_per_batch = seq // ROWS_OUT
    mix = pl.BlockSpec((ROWS_OUT, D_GROUP), lambda m: (m, 0))
    return pl.pallas_call(
        _outproj_kernel,
        grid=(t // ROWS_OUT,),
        in_specs=[mix, mix, mix, mix,
                  pl.BlockSpec((None, HEADS * D_GROUP, d), lambda m: (layer, 0, 0),
                               pipeline_mode=pl.Buffered(1)),
                  pl.BlockSpec((ROWS_OUT, d), lambda m: (m, 0)),
                  pl.BlockSpec((1, 6, d), lambda m: (m // tiles_per_batch, 0, 0)),
                  pl.BlockSpec((1, d), lambda m: (0, 0))],
        out_specs=pl.BlockSpec((ROWS_OUT, d), lambda m: (m, 0)),
        out_shape=jax.ShapeDtypeStruct((t, d), F32),
        scratch_shapes=[pltpu.VMEM((HEADS * D_GROUP, d), BF16)],
        compiler_params=_cparams("arbitrary"),
        name="out_projection",
    )(*outs, w_out, x2, mod, post_g.reshape(1, d))


def _ffn_kernel(x_ref, xp_ref, mod_ref, pre_ref, post_ref, wa_ref, wg_ref, wd_ref, cw_ref, cb_ref,
                o_ref, h_scr, a_scr, *, tiles_per_batch):
    m = pl.program_id(0)
    f = pl.program_id(1)

    @pl.when(f == 0)
    def _():
        gain = pre_ref[...] * (1.0 + mod_ref[0, 4:5, :])
        shift = mod_ref[0, 3:4, :]
        h_scr[0:HALO, :] = _modulated_rms(xp_ref[...], gain, shift).astype(BF16)
        h_scr[HALO:, :] = _modulated_rms(x_ref[...], gain, shift).astype(BF16)
        o_ref[...] = jnp.zeros_like(o_ref)

    keep = (m % tiles_per_batch != 0).astype(F32)
    a = jnp.dot(h_scr[...], wa_ref[...], preferred_element_type=F32)
    a_scr[0:HALO, :] = a[0:HALO] * keep
    a_scr[HALO:, :] = a[HALO:]
    gate = jnp.dot(h_scr[HALO:, :], wg_ref[...], preferred_element_type=F32)
    acc = cb_ref[...]
    for tap in range(FFN_CONV):
        off = HALO - (FFN_CONV - 1) + tap
        acc = acc + cw_ref[tap:tap + 1, :] * a_scr[off:off + ROWS_OUT, :]
    act = (_gelu_tanh(acc) * gate).astype(BF16)
    o_ref[...] += jnp.dot(act, wd_ref[...], preferred_element_type=F32)

    @pl.when(f == pl.num_programs(1) - 1)
    def _():
        o_ref[...] = x_ref[...] + mod_ref[0, 5:6, :] * _rms(o_ref[...], post_ref[...])


def _conv_ffn(x2, mod, pre_g, post_g, w_a, w_g, w_d, conv_w, conv_b, layer, seq):
    t, d = x2.shape
    tiles_per_batch = seq // ROWS_OUT
    halo_per_tile = ROWS_OUT // HALO
    return pl.pallas_call(
        functools.partial(_ffn_kernel, tiles_per_batch=tiles_per_batch),
        grid=(t // ROWS_OUT, D_FF_PAD // D_GROUP),
        in_specs=[pl.BlockSpec((ROWS_OUT, d), lambda m, f: (m, 0)),
                  pl.BlockSpec((HALO, d), lambda m, f: (jnp.maximum(m * halo_per_tile - 1, 0), 0)),
                  pl.BlockSpec((1, 6, d), lambda m, f: (m // tiles_per_batch, 0, 0)),
                  pl.BlockSpec((1, d), lambda m, f: (0, 0)),
                  pl.BlockSpec((1, d), lambda m, f: (0, 0)),
                  pl.BlockSpec((None, d, D_GROUP), lambda m, f: (layer, 0, f)),
                  pl.BlockSpec((None, d, D_GROUP), lambda m, f: (layer, 0, f)),
                  pl.BlockSpec((None, D_GROUP, d), lambda m, f: (layer, f, 0)),
                  pl.BlockSpec((None, FFN_CONV, D_GROUP), lambda m, f: (layer, 0, f)),
                  pl.BlockSpec((None, 1, D_GROUP), lambda m, f: (layer, 0, f))],
        out_specs=pl.BlockSpec((ROWS_OUT, d), lambda m, f: (m, 0)),
        out_shape=jax.ShapeDtypeStruct((t, d), F32),
        scratch_shapes=[pltpu.VMEM((HALO + ROWS_OUT, d), BF16),
                        pltpu.VMEM((HALO + ROWS_OUT, D_GROUP), F32)],
        compiler_params=_cparams("parallel", "arbitrary"),
        name="conv_ffn",
    )(x2, x2, mod, pre_g.reshape(1, d), post_g.reshape(1, d), w_a, w_g, w_d, conv_w, conv_b)


def _ffn_up_prep_kernel(a_ref, g0_ref, g1_ref, g2_ref, g3_ref, ao_ref, go_ref):
    col = pl.program_id(1) * D_GROUP + lax.broadcasted_iota(jnp.int32, (1, D_GROUP), 1)
    valid = col < D_FF
    gate = jnp.concatenate([g0_ref[...], g1_ref[...], g2_ref[...], g3_ref[...]], axis=1)
    ao_ref[...] = jnp.where(valid, a_ref[...], 0.0).astype(BF16)
    go_ref[...] = jnp.where(valid, gate, 0.0).astype(BF16)


def _ffn_down_prep_kernel(r0_ref, r1_ref, r2_ref, r3_ref, o_ref):
    row = pl.program_id(1) * D_GROUP + lax.broadcasted_iota(jnp.int32, (D_GROUP, 1), 0)
    rows = jnp.concatenate([r0_ref[...], r1_ref[...], r2_ref[...], r3_ref[...]], axis=0)
    o_ref[...] = jnp.where(row < D_FF, rows, 0.0).astype(BF16)


def _ffn_weight_layout(w_up, w_down):
    depth, d, _ = w_up.shape
    per_tile = D_GROUP // LANES
    n_in = D_FF // LANES
    tiles = D_FF_PAD // D_GROUP

    def gate_spec(k):
        return pl.BlockSpec((None, d, LANES),
                            lambda l, j: (l, 0, jnp.minimum(n_in + per_tile * j + k, 2 * n_in - 1)))

    def row_spec(k):
        return pl.BlockSpec((None, LANES, d), lambda l, j: (l, jnp.minimum(per_tile * j + k, n_in - 1), 0))

    out_cols = pl.BlockSpec((None, d, D_GROUP), lambda l, j: (l, 0, j))
    w_a, w_g = pl.pallas_call(
        _ffn_up_prep_kernel,
        grid=(depth, tiles),
        in_specs=[pl.BlockSpec((None, d, D_GROUP), lambda l, j: (l, 0, j))] + [gate_spec(k) for k in range(per_tile)],
        out_specs=[out_cols, out_cols],
        out_shape=[jax.ShapeDtypeStruct((depth, d, D_FF_PAD), BF16)] * 2,
        compiler_params=_cparams("parallel", "parallel"),
        name="ffn_up_layout",
    )(w_up, w_up, w_up, w_up, w_up)
    w_d = pl.pallas_call(
        _ffn_down_prep_kernel,
        grid=(depth, tiles),
        in_specs=[row_spec(k) for k in range(per_tile)],
        out_specs=pl.BlockSpec((None, D_GROUP, d), lambda l, j: (l, j, 0)),
        out_shape=jax.ShapeDtypeStruct((depth, D_FF_PAD, d), BF16),
        compiler_params=_cparams("parallel", "parallel"),
        name="ffn_down_layout",
    )(w_down, w_down, w_down, w_down)
    return w_a, w_g, w_d


def _pad_last(a, width):
    return jnp.pad(a, [(0, 0)] * (a.ndim - 1) + [(0, width - a.shape[-1])])


def _layer(x2, mod, layer, batch, seq, weights, mix_pre_g, mix_post_g, dn_conv_w, dn_a_log, dn_dt_bias,
           dn_norm_g, rel_bias, sinks, sgu_norm_g, sgu_w, sgu_b, ffn_pre_g, ffn_post_g):
    t = x2.shape[0]
    p, small, small_t = _in_projection(x2, mod, mix_pre_g, weights["in_full"], weights["in_tail"],
                                       weights["in_small"], layer, seq)
    small_t3 = small_t.reshape(SUBLANES, t // CHUNK, CHUNK).transpose(1, 0, 2)

    prm = jnp.stack([dn_dt_bias, dn_a_log], axis=0)
    prm_row = jnp.zeros((SUBLANES, LANES), F32).at[0:2, HEADS:2 * HEADS].set(prm)
    prm_col = jnp.zeros((SUBLANES, LANES), F32).at[HEADS:2 * HEADS, 0:2].set(prm.T)
    out_a = _gated_deltanet(p, small, small_t3, dn_conv_w, prm_row, prm_col, dn_norm_g, batch, seq)
    out_b = _relbias_attention(p, _relbias_table(rel_bias), batch, seq)
    out_c = _swa_attention(p, sinks, jnp.asarray(_alibi_table()), batch, seq)
    out_d = _spatial_gating(p, sgu_norm_g, sgu_w, _pad_last(sgu_b.T, LANES))

    x2 = _out_projection((out_a, out_b, out_c, out_d), weights["out"], x2, mod, mix_post_g, layer, seq)
    return _conv_ffn(x2, mod, ffn_pre_g, ffn_post_g, weights["ffn_a"], weights["ffn_g"], weights["ffn_d"],
                     weights["ffn_conv_w"], weights["ffn_conv_b"], layer, seq)


def _prepare_weights(w_in, w_out, ffn_w_up, ffn_conv_w, ffn_conv_b, ffn_w_down):
    small_lo = N_LO_TILES * D_GROUP
    w_a, w_g, w_d = _ffn_weight_layout(ffn_w_up, ffn_w_down)
    w_in_t = jnp.swapaxes(w_in, 1, 2)
    w_small = w_in_t[:, small_lo:small_lo + N_SMALL, :]
    return {
        "in_full": w_in_t,
        "in_tail": w_in_t,
        "in_small": jnp.pad(w_small, ((0, 0), (0, LANES - N_SMALL), (0, 0))).astype(BF16),
        "out": w_out,
        "ffn_a": w_a,
        "ffn_g": w_g,
        "ffn_d": w_d,
        "ffn_conv_w": _pad_last(ffn_conv_w, D_FF_PAD),
        "ffn_conv_b": _pad_last(ffn_conv_b[:, None, :], D_FF_PAD),
    }


def kernel(x, c, ada_w, ada_b, mix_pre_g, mix_post_g, w_in, dn_conv_w, dn_a_log, dn_dt_bias, dn_norm_g,
           rel_bias, sinks, sgu_norm_g, sgu_w, sgu_b, w_out, ffn_pre_g, ffn_post_g, ffn_w_up, ffn_conv_w,
           ffn_conv_b, ffn_w_down):
    batch, seq, d = x.shape
    depth = ada_w.shape[0]
    assert d == D_MODEL and seq % ROWS_PROJ == 0 and batch <= SUBLANES
    c_pad = jnp.pad(c, ((0, SUBLANES - batch), (0, 0)))
    mod_all = _modulation(c_pad, ada_w, ada_b)[:, :batch].reshape(depth, batch, 6, d)
    weights = _prepare_weights(w_in, w_out, ffn_w_up, ffn_conv_w, ffn_conv_b, ffn_w_down)
    x2 = x.reshape(batch * seq, d)
    for l in range(depth):
        x2 = _layer(x2, mod_all[l], l, batch, seq, weights, mix_pre_g[l], mix_post_g[l], dn_conv_w[l],
                    dn_a_log[l], dn_dt_bias[l], dn_norm_g[l], rel_bias[l], sinks[l], sgu_norm_g[l],
                    sgu_w[l], sgu_b[l], ffn_pre_g[l], ffn_post_g[l])
    return x2.reshape(batch, seq, d)
```

```python
import functools

import numpy as np
import jax
import jax.numpy as jnp
from jax import lax
from jax.experimental import pallas as pl
from jax.experimental.pallas import tpu as pltpu

F32 = jnp.float32
BF16 = jnp.bfloat16

D_MODEL = 2048
CHUNK = 64
HEAD_DIM = 128
HEADS = 4
D_GROUP = HEADS * HEAD_DIM
DN_CONV = 4
B_PREV_CHUNKS = 8
REL_CLIP = 256
C_PREV_CHUNKS = 2
C_KV_HEADS = 2
SGU_BLOCK = 128
D_FF = 5504
FFN_CONV = 3
EPS = 1e-6

LANES = 128
SUBLANES = 8
VMEM_LIMIT_BYTES = 56 * 1024 * 1024

N_SMALL = 2 * HEADS
N_MAIN = 11 * D_GROUP
N_LO_TILES = 4
COL_AQ, COL_AK, COL_AV, COL_AG, COL_BQ, COL_BK, COL_BV, COL_CQ, COL_CKV, COL_DU, COL_DV = range(11)
D_FF_PAD = 11 * D_GROUP

ROWS_ATT = 512
Q_ROWS = 2 * CHUNK
DN_CHUNKS = 4
DN_ROWS = DN_CHUNKS * CHUNK
HALO = 16
ROWS_PROJ = 2048
ROWS_OUT = 512
MOD_COLS = 1024

ALIBI_SLOPES = tuple(float(2.0 ** (-8.0 * h / HEADS)) for h in range(1, HEADS + 1))


def _cparams(*sem):
    return pltpu.CompilerParams(dimension_semantics=sem, vmem_limit_bytes=VMEM_LIMIT_BYTES)


def _sigmoid(x):
    return 1.0 / (1.0 + jnp.exp(-x))


def _silu(x):
    return x * _sigmoid(x)


def _softplus(x):
    return jnp.maximum(x, 0.0) + jnp.log1p(jnp.exp(-jnp.abs(x)))


def _gelu_tanh(x):
    return 0.5 * x * (1.0 + jnp.tanh(float(np.sqrt(2.0 / np.pi)) * (x + 0.044715 * (x * x * x))))


def _rms(x, g):
    return x * lax.rsqrt(jnp.mean(x * x, axis=-1, keepdims=True) + EPS) * g


def _modulated_rms(x, gain, shift):
    return x * lax.rsqrt(jnp.mean(x * x, axis=-1, keepdims=True) + EPS) * gain + shift


def _dot(a, b):
    return jnp.dot(a.astype(BF16), b.astype(BF16), preferred_element_type=F32)


def _dot_nt(a, b):
    return lax.dot_general(a.astype(BF16), b.astype(BF16), (((1,), (1,)), ((), ())),
                           preferred_element_type=F32)


def _dot_f32(a, b):
    return jnp.dot(a, b, precision=lax.Precision.HIGHEST, preferred_element_type=F32)


def _mod_kernel(c_ref, w_ref, b_ref, o_ref):
    cond = _silu(c_ref[...])
    o_ref[0] = _dot(cond, w_ref[0]) + b_ref[0]


def _modulation(c_pad, ada_w, ada_b):
    depth, d, n = ada_w.shape
    return pl.pallas_call(
        _mod_kernel,
        grid=(depth, n // MOD_COLS),
        in_specs=[pl.BlockSpec((SUBLANES, d), lambda l, j: (0, 0)),
                  pl.BlockSpec((1, d, MOD_COLS), lambda l, j: (l, 0, j)),
                  pl.BlockSpec((1, 1, MOD_COLS), lambda l, j: (l, 0, j))],
        out_specs=pl.BlockSpec((1, SUBLANES, MOD_COLS), lambda l, j: (l, 0, j)),
        out_shape=jax.ShapeDtypeStruct((depth, SUBLANES, n), F32),
        compiler_params=_cparams("parallel", "parallel"),
        name="modulation",
    )(c_pad, ada_w, ada_b.reshape(depth, 1, n))


def _inproj_kernel(x_ref, mod_ref, g_ref, wlo_ref, whi_ref, ws_ref, p_ref, s_ref, st_ref, h_scr):
    n = pl.program_id(1)

    @pl.when(n == 0)
    def _():
        gain = g_ref[...] * (1.0 + mod_ref[0, 1:2, :])
        hb = _modulated_rms(x_ref[...], gain, mod_ref[0, 0:1, :]).astype(BF16)
        h_scr[...] = hb
        small_t = _dot_nt(ws_ref[...], hb)
        s_ref[...] = small_t.T
        st_ref[...] = small_t[0:SUBLANES, :]

    @pl.when(n < N_LO_TILES)
    def _():
        p_ref[...] = _dot_nt(h_scr[...], wlo_ref[...]).astype(p_ref.dtype)

    @pl.when(n >= N_LO_TILES)
    def _():
        p_ref[...] = _dot_nt(h_scr[...], whi_ref[0]).astype(p_ref.dtype)


def _in_projection(x2, mod, pre_g, w_full, w_tail, w_small, layer, seq):
    t, d = x2.shape
    tiles_per_batch = seq // ROWS_PROJ
    last_tile = N_MAIN // D_GROUP - 1
    return pl.pallas_call(
        _inproj_kernel,
        grid=(t // ROWS_PROJ, N_MAIN // D_GROUP),
        in_specs=[pl.BlockSpec((ROWS_PROJ, d), lambda m, n: (m, 0), pipeline_mode=pl.Buffered(1)),
                  pl.BlockSpec((1, 6, d), lambda m, n: (m // tiles_per_batch, 0, 0)),
                  pl.BlockSpec((1, d), lambda m, n: (0, 0)),
                  pl.BlockSpec((None, D_GROUP, d), lambda m, n: (layer, jnp.where(n < N_LO_TILES, n, 0), 0)),
                  pl.BlockSpec((pl.Element(1), pl.Element(D_GROUP), pl.Element(d)),
                               lambda m, n: (layer, pl.multiple_of(
                                   N_SMALL + D_GROUP * jnp.where(n < N_LO_TILES, last_tile, n), SUBLANES), 0)),
                  pl.BlockSpec((None, LANES, d), lambda m, n: (layer, 0, 0))],
        out_specs=[pl.BlockSpec((ROWS_PROJ, D_GROUP), lambda m, n: (m, n)),
                   pl.BlockSpec((ROWS_PROJ, LANES), lambda m, n: (m, 0)),
                   pl.BlockSpec((SUBLANES, ROWS_PROJ), lambda m, n: (0, m))],
        out_shape=[jax.ShapeDtypeStruct((t, N_MAIN), BF16),
                   jax.ShapeDtypeStruct((t, LANES), F32),
                   jax.ShapeDtypeStruct((SUBLANES, t), F32)],
        scratch_shapes=[pltpu.VMEM((ROWS_PROJ, d), BF16)],
        compiler_params=_cparams("parallel", "arbitrary"),
        name="in_projection",
    )(x2, mod, pre_g.reshape(1, d), w_full, w_tail, w_small)


def _unit_lower_inverse(mats, row, col):
    blk_r, blk_c = jnp.right_shift(row, 4), jnp.right_shift(col, 4)
    half_r, half_c = jnp.right_shift(row, 5), jnp.right_shift(col, 5)
    eye = (row == col).astype(F32)
    a_diag = [jnp.where(blk_r == blk_c, a, 0.0) for a in mats]
    a_pair = [jnp.where(half_r == half_c, a, 0.0) - d for a, d in zip(mats, a_diag)]
    a_far = [jnp.where(half_r != half_c, a, 0.0) for a in mats]
    p2 = [_dot(d, d) for d in a_diag]
    t = [eye - d for d in a_diag]
    t = [x + _dot(x, p) for x, p in zip(t, p2)]
    p4 = [_dot(p, p) for p in p2]
    t = [x + _dot(x, p) for x, p in zip(t, p4)]
    p8 = [_dot(p, p) for p in p4]
    t = [x + _dot(x, p) for x, p in zip(t, p8)]
    for off in (a_pair, a_far):
        left = [_dot(x, a) for x, a in zip(t, off)]
        t = [x - _dot(l, x) for x, l in zip(t, left)]
    return t


def _ffn_layout_step(tile, up_refs, down_refs, wa_ref, wg_ref, wd_ref):
    n_tiles = D_FF_PAD // D_GROUP

    @pl.when(tile < n_tiles)
    def _():
        col = tile * D_GROUP + lax.broadcasted_iota(jnp.int32, (1, D_GROUP), 1)
        valid = col < D_FF
        gate = jnp.concatenate([r[...] for r in up_refs[1:]], axis=1)
        wa_ref[...] = jnp.where(valid, up_refs[0][...], 0.0).astype(BF16)
        wg_ref[...] = jnp.where(valid, gate, 0.0).astype(BF16)

    @pl.when((tile >= n_tiles) & (tile < 2 * n_tiles))
    def _():
        row = (tile - n_tiles) * D_GROUP + lax.broadcasted_iota(jnp.int32, (D_GROUP, 1), 0)
        rows = jnp.concatenate([r[...] for r in down_refs], axis=0)
        wd_ref[...] = jnp.where(row < D_FF, rows, 0.0).astype(BF16)


def _delta_kernel(q_ref, k_ref, v_ref, qp_ref, kp_ref, vp_ref, gate_ref, sm_ref, smt_ref, cw_ref,
                  prow_ref, pcol_ref, ng_ref, ua_ref, ug0_ref, ug1_ref, ug2_ref, ug3_ref,
                  dr0_ref, dr1_ref, dr2_ref, dr3_ref, o_ref, wa_ref, wg_ref, wd_ref, state_scr):
    step = pl.program_id(1)
    _ffn_layout_step(pl.program_id(0) * pl.num_programs(1) + step,
                     (ua_ref, ug0_ref, ug1_ref, ug2_ref, ug3_ref),
                     (dr0_ref, dr1_ref, dr2_ref, dr3_ref), wa_ref, wg_ref, wd_ref)

    @pl.when(step == 0)
    def _():
        state_scr[...] = jnp.zeros_like(state_scr)

    keep = (step > 0).astype(F32)
    halo = jnp.concatenate([qp_ref[...], kp_ref[...], vp_ref[...]], axis=1).astype(F32) * keep
    cur = jnp.concatenate([q_ref[...], k_ref[...], v_ref[...]], axis=1).astype(F32)
    xe = jnp.concatenate([halo, cur], axis=0)
    acc = cw_ref[DN_CONV - 1:DN_CONV, :] * cur
    for back in range(1, DN_CONV):
        shifted = pltpu.roll(xe, back, 0)[HALO:, :]
        acc = acc + cw_ref[DN_CONV - 1 - back:DN_CONV - back, :] * shifted
    qkv = _silu(acc)

    row = lax.broadcasted_iota(jnp.int32, (CHUNK, CHUNK), 0)
    col = lax.broadcasted_iota(jnp.int32, (CHUNK, CHUNK), 1)
    causal = row >= col
    strict = row > col
    lower_ones = causal.astype(F32)
    upper_ones = (row <= col).astype(F32)

    sm = sm_ref[...]
    la_cols = -jnp.exp(prow_ref[1:2, :]) * _softplus(sm + prow_ref[0:1, :])
    beta_cols = _sigmoid(sm)
    neg_a_col = -jnp.exp(pcol_ref[:, 1:2])
    dt_col = pcol_ref[:, 0:1]

    scale = HEAD_DIM ** -0.5
    qs, ks, vbs, kbs = [], [], [], []
    for h in range(HEADS):
        lo, hi = h * HEAD_DIM, (h + 1) * HEAD_DIM
        q = qkv[:, lo:hi]
        k = qkv[:, D_GROUP + lo:D_GROUP + hi]
        v = qkv[:, 2 * D_GROUP + lo:2 * D_GROUP + hi]
        beta = beta_cols[:, h:h + 1]
        k = k * lax.rsqrt(jnp.sum(k * k, axis=-1, keepdims=True) + EPS)
        qs.append(q * lax.rsqrt(jnp.sum(q * q, axis=-1, keepdims=True) + EPS) * scale)
        ks.append(k)
        kbs.append(k * beta)
        vbs.append(v * beta)

    probs = [(c, h) for c in range(DN_CHUNKS) for h in range(HEADS)]
    g_cols, g_rows = [], []
    for c in range(DN_CHUNKS):
        rows = slice(c * CHUNK, (c + 1) * CHUNK)
        g_cols.append(_dot_f32(lower_ones, la_cols[rows]))
        g_rows.append(_dot_f32(neg_a_col * _softplus(smt_ref[c] + dt_col), upper_ones))

    def rows_of(c):
        return slice(c * CHUNK, (c + 1) * CHUNK)

    g_col = [g_cols[c][:, HEADS + h:HEADS + h + 1] for c, h in probs]
    g_last = [g[CHUNK - 1:CHUNK, :] for g in g_col]
    decay = [jnp.exp(jnp.where(causal, g - g_rows[c][HEADS + h:HEADS + h + 1, :], -jnp.inf))
             for g, (c, h) in zip(g_col, probs)]
    kk = [_dot_nt(kbs[h][rows_of(c)], ks[h][rows_of(c)]) for c, h in probs]
    qk = [_dot_nt(qs[h][rows_of(c)], ks[h][rows_of(c)]) for c, h in probs]
    t_inv = _unit_lower_inverse([jnp.where(strict, x * d, 0.0) for x, d in zip(kk, decay)], row, col)
    exp_g = [jnp.exp(g) for g in g_col]
    uw = [_dot(t, jnp.concatenate([vbs[h][rows_of(c)], kbs[h][rows_of(c)] * e], axis=1))
          for t, e, (c, h) in zip(t_inv, exp_g, probs)]
    attn = [(x * d).astype(BF16) for x, d in zip(qk, decay)]
    wq = [jnp.concatenate([x[:, HEAD_DIM:], qs[h][rows_of(c)] * e], axis=0).astype(BF16)
          for x, e, (c, h) in zip(uw, exp_g, probs)]
    k_dec_t = [(ks[h][rows_of(c)] * jnp.exp(gl - g)).T.astype(BF16)
               for g, gl, (c, h) in zip(g_col, g_last, probs)]
    decay_last = [jnp.exp(gl) for gl in g_last]

    state = [state_scr[h] for h in range(HEADS)]
    for c in range(DN_CHUNKS):
        idx = [c * HEADS + h for h in range(HEADS)]
        ws_qs = [jnp.dot(wq[i], state[h].astype(BF16), preferred_element_type=F32)
                 for h, i in enumerate(idx)]
        v_new = [(uw[i][:, :HEAD_DIM] - x[:CHUNK]).astype(BF16) for x, i in zip(ws_qs, idx)]
        state = [state[h] * decay_last[i] + jnp.dot(k_dec_t[i], v_new[h], preferred_element_type=F32)
                 for h, i in enumerate(idx)]
        for h, i in enumerate(idx):
            lo, hi = h * HEAD_DIM, (h + 1) * HEAD_DIM
            o = ws_qs[h][CHUNK:] + jnp.dot(attn[i], v_new[h], preferred_element_type=F32)
            o = _rms(o, ng_ref[...]) * _silu(gate_ref[rows_of(c), lo:hi].astype(F32))
            o_ref[rows_of(c), lo:hi] = o.astype(o_ref.dtype)
    for h in range(HEADS):
        state_scr[h] = state[h]


def _gated_deltanet(p, small, small_t3, conv_w, prm_row, prm_col, norm_g, w_up, w_down, layer, batch, seq):
    t = p.shape[0]
    d = w_up.shape[1]
    steps = seq // DN_ROWS
    halo_per_step = DN_ROWS // HALO
    per_tile = D_GROUP // LANES
    n_in = D_FF // LANES
    n_tiles = D_FF_PAD // D_GROUP
    assert batch * steps >= 2 * n_tiles

    def cur(j):
        return pl.BlockSpec((DN_ROWS, D_GROUP), lambda b, s: (b * steps + s, j))

    def prev(j):
        return pl.BlockSpec((HALO, D_GROUP),
                            lambda b, s: (jnp.maximum((b * steps + s) * halo_per_step - 1, 0), j))

    def up_tile(b, s):
        return jnp.minimum(b * steps + s, n_tiles - 1)

    def down_tile(b, s):
        return jnp.clip(b * steps + s - n_tiles, 0, n_tiles - 1)

    def gate_spec(k):
        return pl.BlockSpec((None, d, LANES), lambda b, s: (
            layer, 0, jnp.minimum(n_in + per_tile * up_tile(b, s) + k, 2 * n_in - 1)))

    def row_spec(k):
        return pl.BlockSpec((None, LANES, d), lambda b, s: (
            layer, jnp.minimum(per_tile * down_tile(b, s) + k, n_in - 1), 0))

    up_cols = pl.BlockSpec((d, D_GROUP), lambda b, s: (0, up_tile(b, s)))
    return pl.pallas_call(
        _delta_kernel,
        grid=(batch, steps),
        in_specs=[cur(COL_AQ), cur(COL_AK), cur(COL_AV), prev(COL_AQ), prev(COL_AK), prev(COL_AV),
                  cur(COL_AG),
                  pl.BlockSpec((DN_ROWS, LANES), lambda b, s: (b * steps + s, 0)),
                  pl.BlockSpec((DN_CHUNKS, SUBLANES, CHUNK), lambda b, s: (b * steps + s, 0, 0)),
                  pl.BlockSpec((DN_CONV, 3 * D_GROUP), lambda b, s: (0, 0)),
                  pl.BlockSpec((SUBLANES, LANES), lambda b, s: (0, 0)),
                  pl.BlockSpec((SUBLANES, LANES), lambda b, s: (0, 0)),
                  pl.BlockSpec((1, HEAD_DIM), lambda b, s: (0, 0)),
                  pl.BlockSpec((None, d, D_GROUP), lambda b, s: (layer, 0, up_tile(b, s)))]
                 + [gate_spec(k) for k in range(per_tile)] + [row_spec(k) for k in range(per_tile)],
        out_specs=[pl.BlockSpec((DN_ROWS, D_GROUP), lambda b, s: (b * steps + s, 0)),
                   up_cols, up_cols,
                   pl.BlockSpec((D_GROUP, d), lambda b, s: (down_tile(b, s), 0))],
        out_shape=[jax.ShapeDtypeStruct((t, D_GROUP), BF16),
                   jax.ShapeDtypeStruct((d, D_FF_PAD), BF16),
                   jax.ShapeDtypeStruct((d, D_FF_PAD), BF16),
                   jax.ShapeDtypeStruct((D_FF_PAD, d), BF16)],
        scratch_shapes=[pltpu.VMEM((HEADS, HEAD_DIM, HEAD_DIM), F32)],
        compiler_params=_cparams("arbitrary", "arbitrary"),
        name="gated_deltanet",
    )(p, p, p, p, p, p, p, small, small_t3, conv_w, prm_row, prm_col, norm_g.reshape(1, HEAD_DIM),
      w_up, w_up, w_up, w_up, w_up, w_down, w_down, w_down, w_down)


def _relbias_kernel(q_ref, kc_ref, kp_ref, vc_ref, vp_ref, bias_ref, o_ref, k_scr, v_scr):
    i = pl.program_id(1)
    k_scr[0:ROWS_ATT, :] = kp_ref[...]
    k_scr[ROWS_ATT:, :] = kc_ref[...]
    v_scr[0:ROWS_ATT, :] = vp_ref[...]
    v_scr[ROWS_ATT:, :] = vc_ref[...]
    span = B_PREV_CHUNKS * CHUNK + Q_ROWS
    key_pos = lax.broadcasted_iota(jnp.int32, (Q_ROWS, span), 1)
    first_valid = jnp.where(i > 0, 0, ROWS_ATT)
    scale = HEAD_DIM ** -0.5
    subs = [sub * Q_ROWS for sub in range(ROWS_ATT // Q_ROWS)]
    for h in range(HEADS):
        lo, hi = h * HEAD_DIM, (h + 1) * HEAD_DIM
        bias = bias_ref[h]
        scores = [_dot_nt(q_ref[r0:r0 + Q_ROWS, lo:hi], k_scr[r0:r0 + span, lo:hi]) for r0 in subs]
        probs, invs = [], []
        for r0, s in zip(subs, scores):
            s = jnp.where(key_pos + r0 >= first_valid, s * scale + bias, -jnp.inf)
            e = jnp.exp(s - jnp.max(s, axis=-1, keepdims=True))
            probs.append(e.astype(BF16))
            invs.append(1.0 / jnp.sum(e, axis=-1, keepdims=True))
        outs = [jnp.dot(e, v_scr[r0:r0 + span, lo:hi], preferred_element_type=F32)
                for r0, e in zip(subs, probs)]
        for r0, o, inv in zip(subs, outs, invs):
            o_ref[r0:r0 + Q_ROWS, lo:hi] = (o * inv).astype(o_ref.dtype)


def _relbias_attention(p, bias, batch, seq):
    t = p.shape[0]
    nb = seq // ROWS_ATT

    def cur(j):
        return pl.BlockSpec((ROWS_ATT, D_GROUP), lambda b, i: (b * nb + i, j))

    def prev(j):
        return pl.BlockSpec((ROWS_ATT, D_GROUP), lambda b, i: (b * nb + jnp.maximum(i - 1, 0), j))

    return pl.pallas_call(
        _relbias_kernel,
        grid=(batch, nb),
        in_specs=[cur(COL_BQ), cur(COL_BK), prev(COL_BK), cur(COL_BV), prev(COL_BV),
                  pl.BlockSpec(bias.shape, lambda b, i: (0, 0, 0))],
        out_specs=pl.BlockSpec((ROWS_ATT, D_GROUP), lambda b, i: (b * nb + i, 0)),
        out_shape=jax.ShapeDtypeStruct((t, D_GROUP), BF16),
        scratch_shapes=[pltpu.VMEM((2 * ROWS_ATT, D_GROUP), BF16),
                        pltpu.VMEM((2 * ROWS_ATT, D_GROUP), BF16)],
        compiler_params=_cparams("parallel", "arbitrary"),
        name="relbias_attention",
    )(p, p, p, p, p, bias)


def _swa_kernel(sink_ref, q_ref, kvc_ref, kvp_ref, bias_ref, o_ref, kv_scr):
    i = pl.program_id(1)
    halo = C_PREV_CHUNKS * CHUNK
    kv_scr[0:halo, :] = kvp_ref[...]
    kv_scr[halo:, :] = kvc_ref[...]
    span = halo + Q_ROWS
    group = HEADS // C_KV_HEADS
    stacked = group * Q_ROWS
    key_pos = lax.broadcasted_iota(jnp.int32, (stacked, span), 1)
    q_row = lax.broadcasted_iota(jnp.int32, (stacked, 1), 0)
    first_valid = jnp.where(i > 0, 0, halo)
    scale = HEAD_DIM ** -0.5
    kv_width = C_KV_HEADS * HEAD_DIM
    for kvh in range(C_KV_HEADS):
        klo = kvh * HEAD_DIM
        h0 = kvh * group
        sink = jnp.where(q_row < Q_ROWS, sink_ref[h0], sink_ref[h0 + 1])
        bias = bias_ref[kvh]
        subs = [sub * Q_ROWS for sub in range(ROWS_ATT // Q_ROWS)]
        scores = []
        for r0 in subs:
            q = jnp.concatenate([q_ref[r0:r0 + Q_ROWS, (h0 + g) * HEAD_DIM:(h0 + g + 1) * HEAD_DIM]
                                 for g in range(group)], axis=0)
            scores.append(_dot_nt(q, kv_scr[r0:r0 + span, klo:klo + HEAD_DIM]))
        probs, invs = [], []
        for r0, s in zip(subs, scores):
            s = jnp.where(key_pos + r0 >= first_valid, s * scale + bias, -jnp.inf)
            m = jnp.maximum(jnp.max(s, axis=-1, keepdims=True), sink)
            e = jnp.exp(s - m)
            probs.append(e.astype(BF16))
            invs.append(1.0 / (jnp.sum(e, axis=-1, keepdims=True) + jnp.exp(sink - m)))
        vlo = kv_width + klo
        outs = [jnp.dot(e, kv_scr[r0:r0 + span, vlo:vlo + HEAD_DIM], preferred_element_type=F32)
                for r0, e in zip(subs, probs)]
        for r0, o, inv in zip(subs, outs, invs):
            o = o * inv
            for g in range(group):
                o_ref[r0:r0 + Q_ROWS, (h0 + g) * HEAD_DIM:(h0 + g + 1) * HEAD_DIM] = (
                    o[g * Q_ROWS:(g + 1) * Q_ROWS].astype(o_ref.dtype))


def _swa_attention(p, sinks, bias, batch, seq):
    t = p.shape[0]
    nb = seq // ROWS_ATT
    halo = C_PREV_CHUNKS * CHUNK
    halo_per_tile = ROWS_ATT // halo
    return pl.pallas_call(
        _swa_kernel,
        grid=(batch, nb),
        in_specs=[pl.BlockSpec(memory_space=pltpu.SMEM),
                  pl.BlockSpec((ROWS_ATT, D_GROUP), lambda b, i: (b * nb + i, COL_CQ)),
                  pl.BlockSpec((ROWS_ATT, D_GROUP), lambda b, i: (b * nb + i, COL_CKV)),
                  pl.BlockSpec((halo, D_GROUP),
                               lambda b, i: (jnp.maximum((b * nb + i) * halo_per_tile - 1, 0), COL_CKV)),
                  pl.BlockSpec(bias.shape, lambda b, i: (0, 0, 0))],
        out_specs=pl.BlockSpec((ROWS_ATT, D_GROUP), lambda b, i: (b * nb + i, 0)),
        out_shape=jax.ShapeDtypeStruct((t, D_GROUP), BF16),
        scratch_shapes=[pltpu.VMEM((halo + ROWS_ATT, D_GROUP), BF16)],
        compiler_params=_cparams("parallel", "arbitrary"),
        name="swa_attention",
    )(sinks, p, p, p, bias)


def _pair_mask(n_prev):
    span = n_prev * CHUNK + Q_ROWS
    r = np.arange(Q_ROWS)[:, None]
    j = np.arange(span)[None, :]
    band_pos = j - (r // CHUNK) * CHUNK
    return np.where((band_pos >= 0) & (band_pos < (n_prev + 1) * CHUNK), 0.0, -np.inf).astype(np.float32)


def _relbias_table(rel_bias):
    span = B_PREV_CHUNKS * CHUNK + Q_ROWS
    period = Q_ROWS + span
    k = np.arange(period) - (Q_ROWS - 1)
    idx = np.clip(B_PREV_CHUNKS * CHUNK - k, -REL_CLIP, REL_CLIP) + REL_CLIP
    g = rel_bias[:, idx]
    skew = jnp.tile(g, (1, Q_ROWS))[:, :Q_ROWS * (period - 1)].reshape(HEADS, Q_ROWS, period - 1)
    return skew[:, :, Q_ROWS - 1:Q_ROWS - 1 + span] + _pair_mask(B_PREV_CHUNKS)


def _alibi_table():
    halo = C_PREV_CHUNKS * CHUNK
    span = halo + Q_ROWS
    r = np.arange(Q_ROWS)[:, None]
    j = np.arange(span)[None, :]
    dist = np.abs(halo + r - j).astype(np.float32)
    per_head = [(-np.float32(s)) * dist + _pair_mask(C_PREV_CHUNKS) for s in ALIBI_SLOPES]
    group = HEADS // C_KV_HEADS
    return np.stack([np.concatenate(per_head[kv * group:(kv + 1) * group], axis=0)
                     for kv in range(C_KV_HEADS)], axis=0)


def _sgu_kernel(u_ref, v_ref, g_ref, w_ref, b_ref, o_ref):
    u = _gelu_tanh(u_ref[...].astype(F32))
    v = _gelu_tanh(v_ref[...].astype(F32))
    mu = jnp.mean(v, axis=-1, keepdims=True)
    var = jnp.mean(jnp.square(v - mu), axis=-1, keepdims=True)
    v = ((v - mu) * lax.rsqrt(var + EPS) * g_ref[...]).astype(BF16)
    row = lax.broadcasted_iota(jnp.int32, (SGU_BLOCK, SGU_BLOCK), 0)
    col = lax.broadcasted_iota(jnp.int32, (SGU_BLOCK, SGU_BLOCK), 1)
    for g in range(HEADS):
        lo, hi = g * HEAD_DIM, (g + 1) * HEAD_DIM
        w = jnp.where(row >= col, w_ref[g], 0.0).astype(BF16)
        bias = b_ref[:, g:g + 1]
        for n in range(ROWS_ATT // SGU_BLOCK):
            r0 = n * SGU_BLOCK
            mixed = jnp.dot(w, v[r0:r0 + SGU_BLOCK, lo:hi], preferred_element_type=F32) + bias
            o_ref[r0:r0 + SGU_BLOCK, lo:hi] = (u[r0:r0 + SGU_BLOCK, lo:hi] * mixed).astype(o_ref.dtype)


def _spatial_gating(p, norm_g, w_s, b_cols):
    t = p.shape[0]
    return pl.pallas_call(
        _sgu_kernel,
        grid=(t // ROWS_ATT,),
        in_specs=[pl.BlockSpec((ROWS_ATT, D_GROUP), lambda i: (i, COL_DU)),
                  pl.BlockSpec((ROWS_ATT, D_GROUP), lambda i: (i, COL_DV)),
                  pl.BlockSpec((1, D_GROUP), lambda i: (0, 0)),
                  pl.BlockSpec((HEADS, SGU_BLOCK, SGU_BLOCK), lambda i: (0, 0, 0)),
                  pl.BlockSpec((SGU_BLOCK, LANES), lambda i: (0, 0))],
        out_specs=pl.BlockSpec((ROWS_ATT, D_GROUP), lambda i: (i, 0)),
        out_shape=jax.ShapeDtypeStruct((t, D_GROUP), BF16),
        compiler_params=_cparams("parallel"),
        name="spatial_gating",
    )(p, p, norm_g.reshape(1, D_GROUP), w_s, b_cols)


def _outproj_kernel(oa_ref, ob_ref, oc_ref, od_ref, w_ref, x_ref, mod_ref, g_ref, o_ref, w_scr):
    @pl.when(pl.program_id(0) == 0)
    def _():
        w_scr[...] = w_ref[...].astype(BF16)

    half = ROWS_OUT // 2
    for r0 in (0, half):
        rows = slice(r0, r0 + half)
        y = jnp.dot(oa_ref[rows, :], w_scr[0:D_GROUP, :], preferred_element_type=F32)
        y += jnp.dot(ob_ref[rows, :], w_scr[D_GROUP:2 * D_GROUP, :], preferred_element_type=F32)
        y += jnp.dot(oc_ref[rows, :], w_scr[2 * D_GROUP:3 * D_GROUP, :], preferred_element_type=F32)
        y += jnp.dot(od_ref[rows, :], w_scr[3 * D_GROUP:, :], preferred_element_type=F32)
        o_ref[rows, :] = x_ref[rows, :] + mod_ref[0, 2:3, :] * _rms(y, g_ref[...])


def _out_projection(outs, w_out, x2, mod, post_g, layer, seq):
    t, d = x2.shape
    tiles_per_batch = seq // ROWS_OUT
    mix = pl.BlockSpec((ROWS_OUT, D_GROUP), lambda m: (m, 0))
    return pl.pallas_call(
        _outproj_kernel,
        grid=(t // ROWS_OUT,),
        in_specs=[mix, mix, mix, mix,
                  pl.BlockSpec((None, HEADS * D_GROUP, d), lambda m: (layer, 0, 0),
                               pipeline_mode=pl.Buffered(1)),
                  pl.BlockSpec((ROWS_OUT, d), lambda m: (m, 0)),
                  pl.BlockSpec((1, 6, d), lambda m: (m // tiles_per_batch, 0, 0)),
                  pl.BlockSpec((1, d), lambda m: (0, 0))],
        out_specs=pl.BlockSpec((ROWS_OUT, d), lambda m: (m, 0)),
        out_shape=jax.ShapeDtypeStruct((t, d), F32),
        scratch_shapes=[pltpu.VMEM((HEADS * D_GROUP, d), BF16)],
        compiler_params=_cparams("arbitrary"),
        name="out_projection",
    )(*outs, w_out, x2, mod, post_g.reshape(1, d))


def _ffn_kernel(x_ref, xp_ref, mod_ref, pre_ref, post_ref, wa_ref, wg_ref, wd_ref, cw_ref, cb_ref,
                o_ref, h_scr, a_scr, *, tiles_per_batch):
    m = pl.program_id(0)
    f = pl.program_id(1)

    @pl.when(f == 0)
    def _():
        gain = pre_ref[...] * (1.0 + mod_ref[0, 4:5, :])
        shift = mod_ref[0, 3:4, :]
        h_scr[0:HALO, :] = _modulated_rms(xp_ref[...], gain, shift).astype(BF16)
        h_scr[HALO:, :] = _modulated_rms(x_ref[...], gain, shift).astype(BF16)
        o_ref[...] = jnp.zeros_like(o_ref)

    keep = (m % tiles_per_batch != 0).astype(F32)
    a = jnp.dot(h_scr[...], wa_ref[...], preferred_element_type=F32)
    a_scr[0:HALO, :] = a[0:HALO] * keep
    a_scr[HALO:, :] = a[HALO:]
    gate = jnp.dot(h_scr[HALO:, :], wg_ref[...], preferred_element_type=F32)
    acc = cb_ref[...]
    for tap in range(FFN_CONV):
        off = HALO - (FFN_CONV - 1) + tap
        acc = acc + cw_ref[tap:tap + 1, :] * a_scr[off:off + ROWS_OUT, :]
    act = (_gelu_tanh(acc) * gate).astype(BF16)
    o_ref[...] += jnp.dot(act, wd_ref[...], preferred_element_type=F32)

    @pl.when(f == pl.num_programs(1) - 1)
    def _():
        o_ref[...] = x_ref[...] + mod_ref[0, 5:6, :] * _rms(o_ref[...], post_ref[...])


def _conv_ffn(x2, mod, pre_g, post_g, w_a, w_g, w_d, conv_w, conv_b, layer, seq):
    t, d = x2.shape
    tiles_per_batch = seq // ROWS_OUT
    halo_per_tile = ROWS_OUT // HALO
    return pl.pallas_call(
        functools.partial(_ffn_kernel, tiles_per_batch=tiles_per_batch),
        grid=(t // ROWS_OUT, D_FF_PAD // D_GROUP),
        in_specs=[pl.BlockSpec((ROWS_OUT, d), lambda m, f: (m, 0)),
                  pl.BlockSpec((HALO, d), lambda m, f: (jnp.maximum(m * halo_per_tile - 1, 0), 0)),
                  pl.BlockSpec((1, 6, d), lambda m, f: (m // tiles_per_batch, 0, 0)),
                  pl.BlockSpec((1, d), lambda m, f: (0, 0)),
                  pl.BlockSpec((1, d), lambda m, f: (0, 0)),
                  pl.BlockSpec((d, D_GROUP), lambda m, f: (0, f)),
                  pl.BlockSpec((d, D_GROUP), lambda m, f: (0, f)),
                  pl.BlockSpec((D_GROUP, d), lambda m, f: (f, 0)),
                  pl.BlockSpec((None, FFN_CONV, D_GROUP), lambda m, f: (layer, 0, f)),
                  pl.BlockSpec((None, 1, D_GROUP), lambda m, f: (layer, 0, f))],
        out_specs=pl.BlockSpec((ROWS_OUT, d), lambda m, f: (m, 0)),
        out_shape=jax.ShapeDtypeStruct((t, d), F32),
        scratch_shapes=[pltpu.VMEM((HALO + ROWS_OUT, d), BF16),
                        pltpu.VMEM((HALO + ROWS_OUT, D_GROUP), F32)],
        compiler_params=_cparams("parallel", "arbitrary"),
        name="conv_ffn",
    )(x2, x2, mod, pre_g.reshape(1, d), post_g.reshape(1, d), w_a, w_g, w_d, conv_w, conv_b)


def _pad_last(a, width):
    return jnp.pad(a, [(0, 0)] * (a.ndim - 1) + [(0, width - a.shape[-1])])


def _layer(x2, mod, layer, batch, seq, weights, mix_pre_g, mix_post_g, dn_conv_w, dn_a_log, dn_dt_bias,
           dn_norm_g, rel_bias, sinks, sgu_norm_g, sgu_w, sgu_b, ffn_pre_g, ffn_post_g):
    t = x2.shape[0]
    p, small, small_t = _in_projection(x2, mod, mix_pre_g, weights["in_full"], weights["in_tail"],
                                       weights["in_small"], layer, seq)
    small_t3 = small_t.reshape(SUBLANES, t // CHUNK, CHUNK).transpose(1, 0, 2)

    prm = jnp.stack([dn_dt_bias, dn_a_log], axis=0)
    prm_row = jnp.zeros((SUBLANES, LANES), F32).at[0:2, HEADS:2 * HEADS].set(prm)
    prm_col = jnp.zeros((SUBLANES, LANES), F32).at[HEADS:2 * HEADS, 0:2].set(prm.T)
    out_a, w_a, w_g, w_d = _gated_deltanet(p, small, small_t3, dn_conv_w, prm_row, prm_col, dn_norm_g,
                                           weights["ffn_up"], weights["ffn_down"], layer, batch, seq)
    out_b = _relbias_attention(p, _relbias_table(rel_bias), batch, seq)
    out_c = _swa_attention(p, sinks, jnp.asarray(_alibi_table()), batch, seq)
    out_d = _spatial_gating(p, sgu_norm_g, sgu_w, _pad_last(sgu_b.T, LANES))

    x2 = _out_projection((out_a, out_b, out_c, out_d), weights["out"], x2, mod, mix_post_g, layer, seq)
    return _conv_ffn(x2, mod, ffn_pre_g, ffn_post_g, w_a, w_g, w_d,
                     weights["ffn_conv_w"], weights["ffn_conv_b"], layer, seq)


def _prepare_weights(w_in, w_out, ffn_w_up, ffn_conv_w, ffn_conv_b, ffn_w_down):
    small_lo = N_LO_TILES * D_GROUP
    w_in_t = jnp.swapaxes(w_in, 1, 2)
    w_small = w_in_t[:, small_lo:small_lo + N_SMALL, :]
    return {
        "in_full": w_in_t,
        "in_tail": w_in_t,
        "in_small": jnp.pad(w_small, ((0, 0), (0, LANES - N_SMALL), (0, 0))).astype(BF16),
        "out": w_out,
        "ffn_up": ffn_w_up,
        "ffn_down": ffn_w_down,
        "ffn_conv_w": _pad_last(ffn_conv_w, D_FF_PAD),
        "ffn_conv_b": _pad_last(ffn_conv_b[:, None, :], D_FF_PAD),
    }


def kernel(x, c, ada_w, ada_b, mix_pre_g, mix_post_g, w_in, dn_conv_w, dn_a_log, dn_dt_bias, dn_norm_g,
           rel_bias, sinks, sgu_norm_g, sgu_w, sgu_b, w_out, ffn_pre_g, ffn_post_g, ffn_w_up, ffn_conv_w,
           ffn_conv_b, ffn_w_down):
    batch, seq, d = x.shape
    depth = ada_w.shape[0]
    assert d == D_MODEL and seq % ROWS_PROJ == 0 and batch <= SUBLANES
    c_pad = jnp.pad(c, ((0, SUBLANES - batch), (0, 0)))
    mod_all = _modulation(c_pad, ada_w, ada_b)[:, :batch].reshape(depth, batch, 6, d)
    weights = _prepare_weights(w_in, w_out, ffn_w_up, ffn_conv_w, ffn_conv_b, ffn_w_down)
    x2 = x.reshape(batch * seq, d)
    for l in range(depth):
        x2 = _layer(x2, mod_all[l], l, batch, seq, weights, mix_pre_g[l], mix_post_g[l], dn_conv_w[l],
                    dn_a_log[l], dn_dt_bias[l], dn_norm_g[l], rel_bias[l], sinks[l], sgu_norm_g[l],
                    sgu_w[l], sgu_b[l], ffn_pre_g[l], ffn_post_g[l])
    return x2.reshape(batch, seq, d)
```

```python
import functools

import numpy as np
import jax
import jax.numpy as jnp
from jax import lax
from jax.experimental import pallas as pl
from jax.experimental.pallas import tpu as pltpu

F32 = jnp.float32
BF16 = jnp.bfloat16

D_MODEL = 2048
CHUNK = 64
HEAD_DIM = 128
HEADS = 4
D_GROUP = HEADS * HEAD_DIM
DN_CONV = 4
B_PREV_CHUNKS = 8
REL_CLIP = 256
C_PREV_CHUNKS = 2
C_KV_HEADS = 2
SGU_BLOCK = 128
D_FF = 5504
FFN_CONV = 3
EPS = 1e-6

LANES = 128
SUBLANES = 8
VMEM_LIMIT_BYTES = 56 * 1024 * 1024

N_SMALL = 2 * HEADS
N_MAIN = 11 * D_GROUP
N_LO_TILES = 4
COL_AQ, COL_AK, COL_AV, COL_AG, COL_BQ, COL_BK, COL_BV, COL_CQ, COL_CKV, COL_DU, COL_DV = range(11)
D_FF_PAD = 11 * D_GROUP

ROWS_ATT = 512
Q_ROWS = 2 * CHUNK
DN_CHUNKS = 4
DN_ROWS = DN_CHUNKS * CHUNK
HALO = 16
ROW_CHUNK = 16
ROWS_PROJ = 2048
ROWS_OUT = 512
MOD_COLS = 1024

ALIBI_SLOPES = tuple(float(2.0 ** (-8.0 * h / HEADS)) for h in range(1, HEADS + 1))


def _cparams(*sem):
    return pltpu.CompilerParams(dimension_semantics=sem, vmem_limit_bytes=VMEM_LIMIT_BYTES)


def _sigmoid(x):
    return 1.0 / (1.0 + jnp.exp(-x))


def _silu(x):
    return x * _sigmoid(x)


def _softplus(x):
    return jnp.maximum(x, 0.0) + jnp.log1p(jnp.exp(-jnp.abs(x)))


def _gelu_tanh(x):
    return 0.5 * x * (1.0 + jnp.tanh(float(np.sqrt(2.0 / np.pi)) * (x + 0.044715 * (x * x * x))))


def _rms(x, g):
    return x * lax.rsqrt(jnp.mean(x * x, axis=-1, keepdims=True) + EPS) * g


def _modulated_rms(x, gain, shift):
    return x * lax.rsqrt(jnp.mean(x * x, axis=-1, keepdims=True) + EPS) * gain + shift


def _for_row_chunks(n_rows, body):
    for r0 in range(0, n_rows, ROW_CHUNK):
        body(slice(r0, r0 + ROW_CHUNK))


def _dot(a, b):
    return jnp.dot(a.astype(BF16), b.astype(BF16), preferred_element_type=F32)


def _dot_nt(a, b):
    return lax.dot_general(a.astype(BF16), b.astype(BF16), (((1,), (1,)), ((), ())),
                           preferred_element_type=F32)


def _dot_f32(a, b):
    return jnp.dot(a, b, precision=lax.Precision.HIGHEST, preferred_element_type=F32)


def _mod_kernel(c_ref, w_ref, b_ref, o_ref):
    cond = _silu(c_ref[...])
    o_ref[0] = _dot(cond, w_ref[0]) + b_ref[0]


def _modulation(c_pad, ada_w, ada_b):
    depth, d, n = ada_w.shape
    return pl.pallas_call(
        _mod_kernel,
        grid=(depth, n // MOD_COLS),
        in_specs=[pl.BlockSpec((SUBLANES, d), lambda l, j: (0, 0)),
                  pl.BlockSpec((1, d, MOD_COLS), lambda l, j: (l, 0, j)),
                  pl.BlockSpec((1, 1, MOD_COLS), lambda l, j: (l, 0, j))],
        out_specs=pl.BlockSpec((1, SUBLANES, MOD_COLS), lambda l, j: (l, 0, j)),
        out_shape=jax.ShapeDtypeStruct((depth, SUBLANES, n), F32),
        compiler_params=_cparams("parallel", "parallel"),
        name="modulation",
    )(c_pad, ada_w, ada_b.reshape(depth, 1, n))


def _inproj_kernel(x_ref, mod_ref, g_ref, wlo_ref, whi_ref, ws_ref, p_ref, s_ref, st_ref, h_scr):
    n = pl.program_id(1)

    @pl.when(n == 0)
    def _():
        gain = g_ref[...] * (1.0 + mod_ref[0, 1:2, :])
        shift = mod_ref[0, 0:1, :]

        def fill(rows):
            h_scr[rows, :] = _modulated_rms(x_ref[rows, :], gain, shift).astype(BF16)

        _for_row_chunks(ROWS_PROJ, fill)
        small_t = _dot_nt(ws_ref[...], h_scr[...])
        s_ref[...] = small_t.T
        st_ref[...] = small_t[0:SUBLANES, :]

    @pl.when(n < N_LO_TILES)
    def _():
        p_ref[...] = _dot_nt(h_scr[...], wlo_ref[...]).astype(p_ref.dtype)

    @pl.when(n >= N_LO_TILES)
    def _():
        p_ref[...] = _dot_nt(h_scr[...], whi_ref[0]).astype(p_ref.dtype)


def _in_projection(x2, mod, pre_g, w_full, w_tail, w_small, layer, seq):
    t, d = x2.shape
    tiles_per_batch = seq // ROWS_PROJ
    last_tile = N_MAIN // D_GROUP - 1
    return pl.pallas_call(
        _inproj_kernel,
        grid=(t // ROWS_PROJ, N_MAIN // D_GROUP),
        in_specs=[pl.BlockSpec((ROWS_PROJ, d), lambda m, n: (m, 0), pipeline_mode=pl.Buffered(1)),
                  pl.BlockSpec((1, 6, d), lambda m, n: (m // tiles_per_batch, 0, 0)),
                  pl.BlockSpec((1, d), lambda m, n: (0, 0)),
                  pl.BlockSpec((None, D_GROUP, d), lambda m, n: (layer, jnp.where(n < N_LO_TILES, n, 0), 0)),
                  pl.BlockSpec((pl.Element(1), pl.Element(D_GROUP), pl.Element(d)),
                               lambda m, n: (layer, pl.multiple_of(
                                   N_SMALL + D_GROUP * jnp.where(n < N_LO_TILES, last_tile, n), SUBLANES), 0)),
                  pl.BlockSpec((None, LANES, d), lambda m, n: (layer, 0, 0))],
        out_specs=[pl.BlockSpec((ROWS_PROJ, D_GROUP), lambda m, n: (m, n)),
                   pl.BlockSpec((ROWS_PROJ, LANES), lambda m, n: (m, 0)),
                   pl.BlockSpec((SUBLANES, ROWS_PROJ), lambda m, n: (0, m))],
        out_shape=[jax.ShapeDtypeStruct((t, N_MAIN), BF16),
                   jax.ShapeDtypeStruct((t, LANES), F32),
                   jax.ShapeDtypeStruct((SUBLANES, t), F32)],
        scratch_shapes=[pltpu.VMEM((ROWS_PROJ, d), BF16)],
        compiler_params=_cparams("parallel", "arbitrary"),
        name="in_projection",
    )(x2, mod, pre_g.reshape(1, d), w_full, w_tail, w_small)


def _unit_lower_inverse(mats, row, col):
    blk_r, blk_c = jnp.right_shift(row, 4), jnp.right_shift(col, 4)
    half_r, half_c = jnp.right_shift(row, 5), jnp.right_shift(col, 5)
    eye = (row == col).astype(F32)
    a_diag = [jnp.where(blk_r == blk_c, a, 0.0) for a in mats]
    a_pair = [jnp.where(half_r == half_c, a, 0.0) - d for a, d in zip(mats, a_diag)]
    a_far = [jnp.where(half_r != half_c, a, 0.0) for a in mats]
    p2 = [_dot(d, d) for d in a_diag]
    t = [eye - d for d in a_diag]
    t = [x + _dot(x, p) for x, p in zip(t, p2)]
    p4 = [_dot(p, p) for p in p2]
    t = [x + _dot(x, p) for x, p in zip(t, p4)]
    p8 = [_dot(p, p) for p in p4]
    t = [x + _dot(x, p) for x, p in zip(t, p8)]
    for off in (a_pair, a_far):
        left = [_dot(x, a) for x, a in zip(t, off)]
        t = [x - _dot(l, x) for x, l in zip(t, left)]
    return t


def _ffn_layout_step(tile, up_refs, down_refs, wa_ref, wg_ref, wd_ref):
    n_tiles = D_FF_PAD // D_GROUP

    @pl.when(tile < n_tiles)
    def _():
        col = tile * D_GROUP + lax.broadcasted_iota(jnp.int32, (1, D_GROUP), 1)
        valid = col < D_FF
        gate = jnp.concatenate([r[...] for r in up_refs[1:]], axis=1)
        wa_ref[...] = jnp.where(valid, up_refs[0][...], 0.0).astype(BF16)
        wg_ref[...] = jnp.where(valid, gate, 0.0).astype(BF16)

    @pl.when((tile >= n_tiles) & (tile < 2 * n_tiles))
    def _():
        row = (tile - n_tiles) * D_GROUP + lax.broadcasted_iota(jnp.int32, (D_GROUP, 1), 0)
        rows = jnp.concatenate([r[...] for r in down_refs], axis=0)
        wd_ref[...] = jnp.where(row < D_FF, rows, 0.0).astype(BF16)


def _delta_kernel(q_ref, k_ref, v_ref, qp_ref, kp_ref, vp_ref, gate_ref, sm_ref, smt_ref, cw_ref,
                  prow_ref, pcol_ref, ng_ref, ua_ref, ug0_ref, ug1_ref, ug2_ref, ug3_ref,
                  dr0_ref, dr1_ref, dr2_ref, dr3_ref, o_ref, wa_ref, wg_ref, wd_ref, state_scr):
    step = pl.program_id(1)
    _ffn_layout_step(pl.program_id(0) * pl.num_programs(1) + step,
                     (ua_ref, ug0_ref, ug1_ref, ug2_ref, ug3_ref),
                     (dr0_ref, dr1_ref, dr2_ref, dr3_ref), wa_ref, wg_ref, wd_ref)

    @pl.when(step == 0)
    def _():
        state_scr[...] = jnp.zeros_like(state_scr)

    keep = (step > 0).astype(F32)
    halo = jnp.concatenate([qp_ref[...], kp_ref[...], vp_ref[...]], axis=1).astype(F32) * keep
    cur = jnp.concatenate([q_ref[...], k_ref[...], v_ref[...]], axis=1).astype(F32)
    xe = jnp.concatenate([halo, cur], axis=0)
    acc = cw_ref[DN_CONV - 1:DN_CONV, :] * cur
    for back in range(1, DN_CONV):
        shifted = pltpu.roll(xe, back, 0)[HALO:, :]
        acc = acc + cw_ref[DN_CONV - 1 - back:DN_CONV - back, :] * shifted
    qkv = _silu(acc)

    row = lax.broadcasted_iota(jnp.int32, (CHUNK, CHUNK), 0)
    col = lax.broadcasted_iota(jnp.int32, (CHUNK, CHUNK), 1)
    causal = row >= col
    strict = row > col
    lower_ones = causal.astype(F32)
    upper_ones = (row <= col).astype(F32)

    sm = sm_ref[...]
    la_cols = -jnp.exp(prow_ref[1:2, :]) * _softplus(sm + prow_ref[0:1, :])
    beta_cols = _sigmoid(sm)
    neg_a_col = -jnp.exp(pcol_ref[:, 1:2])
    dt_col = pcol_ref[:, 0:1]

    scale = HEAD_DIM ** -0.5
    qs, ks, vbs, kbs = [], [], [], []
    for h in range(HEADS):
        lo, hi = h * HEAD_DIM, (h + 1) * HEAD_DIM
        q = qkv[:, lo:hi]
        k = qkv[:, D_GROUP + lo:D_GROUP + hi]
        v = qkv[:, 2 * D_GROUP + lo:2 * D_GROUP + hi]
        beta = beta_cols[:, h:h + 1]
        k = k * lax.rsqrt(jnp.sum(k * k, axis=-1, keepdims=True) + EPS)
        qs.append(q * lax.rsqrt(jnp.sum(q * q, axis=-1, keepdims=True) + EPS) * scale)
        ks.append(k)
        kbs.append(k * beta)
        vbs.append(v * beta)

    probs = [(c, h) for c in range(DN_CHUNKS) for h in range(HEADS)]
    g_cols, g_rows = [], []
    for c in range(DN_CHUNKS):
        rows = slice(c * CHUNK, (c + 1) * CHUNK)
        g_cols.append(_dot_f32(lower_ones, la_cols[rows]))
        g_rows.append(_dot_f32(neg_a_col * _softplus(smt_ref[c] + dt_col), upper_ones))

    def rows_of(c):
        return slice(c * CHUNK, (c + 1) * CHUNK)

    g_col = [g_cols[c][:, HEADS + h:HEADS + h + 1] for c, h in probs]
    g_last = [g[CHUNK - 1:CHUNK, :] for g in g_col]
    decay = [jnp.exp(jnp.where(causal, g - g_rows[c][HEADS + h:HEADS + h + 1, :], -jnp.inf))
             for g, (c, h) in zip(g_col, probs)]
    kk = [_dot_nt(kbs[h][rows_of(c)], ks[h][rows_of(c)]) for c, h in probs]
    qk = [_dot_nt(qs[h][rows_of(c)], ks[h][rows_of(c)]) for c, h in probs]
    t_inv = _unit_lower_inverse([jnp.where(strict, x * d, 0.0) for x, d in zip(kk, decay)], row, col)
    exp_g = [jnp.exp(g) for g in g_col]
    uw = [_dot(t, jnp.concatenate([vbs[h][rows_of(c)], kbs[h][rows_of(c)] * e], axis=1))
          for t, e, (c, h) in zip(t_inv, exp_g, probs)]
    attn = [(x * d).astype(BF16) for x, d in zip(qk, decay)]
    wq = [jnp.concatenate([x[:, HEAD_DIM:], qs[h][rows_of(c)] * e], axis=0).astype(BF16)
          for x, e, (c, h) in zip(uw, exp_g, probs)]
    k_dec_t = [(ks[h][rows_of(c)] * jnp.exp(gl - g)).T.astype(BF16)
               for g, gl, (c, h) in zip(g_col, g_last, probs)]
    decay_last = [jnp.exp(gl) for gl in g_last]

    state = [state_scr[h] for h in range(HEADS)]
    for c in range(DN_CHUNKS):
        idx = [c * HEADS + h for h in range(HEADS)]
        ws_qs = [jnp.dot(wq[i], state[h].astype(BF16), preferred_element_type=F32)
                 for h, i in enumerate(idx)]
        v_new = [(uw[i][:, :HEAD_DIM] - x[:CHUNK]).astype(BF16) for x, i in zip(ws_qs, idx)]
        state = [state[h] * decay_last[i] + jnp.dot(k_dec_t[i], v_new[h], preferred_element_type=F32)
                 for h, i in enumerate(idx)]
        for h, i in enumerate(idx):
            lo, hi = h * HEAD_DIM, (h + 1) * HEAD_DIM
            o = ws_qs[h][CHUNK:] + jnp.dot(attn[i], v_new[h], preferred_element_type=F32)
            o = _rms(o, ng_ref[...]) * _silu(gate_ref[rows_of(c), lo:hi].astype(F32))
            o_ref[rows_of(c), lo:hi] = o.astype(o_ref.dtype)
    for h in range(HEADS):
        state_scr[h] = state[h]


def _gated_deltanet(p, small, small_t3, conv_w, prm_row, prm_col, norm_g, w_up, w_down, layer, batch, seq):
    t = p.shape[0]
    d = w_up.shape[1]
    steps = seq // DN_ROWS
    halo_per_step = DN_ROWS // HALO
    per_tile = D_GROUP // LANES
    n_in = D_FF // LANES
    n_tiles = D_FF_PAD // D_GROUP
    assert batch * steps >= 2 * n_tiles

    def cur(j):
        return pl.BlockSpec((DN_ROWS, D_GROUP), lambda b, s: (b * steps + s, j))

    def prev(j):
        return pl.BlockSpec((HALO, D_GROUP),
                            lambda b, s: (jnp.maximum((b * steps + s) * halo_per_step - 1, 0), j))

    def up_tile(b, s):
        return jnp.minimum(b * steps + s, n_tiles - 1)

    def down_tile(b, s):
        return jnp.clip(b * steps + s - n_tiles, 0, n_tiles - 1)

    def gate_spec(k):
        return pl.BlockSpec((None, d, LANES), lambda b, s: (
            layer, 0, jnp.minimum(n_in + per_tile * up_tile(b, s) + k, 2 * n_in - 1)))

    def row_spec(k):
        return pl.BlockSpec((None, LANES, d), lambda b, s: (
            layer, jnp.minimum(per_tile * down_tile(b, s) + k, n_in - 1), 0))

    up_cols = pl.BlockSpec((d, D_GROUP), lambda b, s: (0, up_tile(b, s)))
    return pl.pallas_call(
        _delta_kernel,
        grid=(batch, steps),
        in_specs=[cur(COL_AQ), cur(COL_AK), cur(COL_AV), prev(COL_AQ), prev(COL_AK), prev(COL_AV),
                  cur(COL_AG),
                  pl.BlockSpec((DN_ROWS, LANES), lambda b, s: (b * steps + s, 0)),
                  pl.BlockSpec((DN_CHUNKS, SUBLANES, CHUNK), lambda b, s: (b * steps + s, 0, 0)),
                  pl.BlockSpec((DN_CONV, 3 * D_GROUP), lambda b, s: (0, 0)),
                  pl.BlockSpec((SUBLANES, LANES), lambda b, s: (0, 0)),
                  pl.BlockSpec((SUBLANES, LANES), lambda b, s: (0, 0)),
                  pl.BlockSpec((1, HEAD_DIM), lambda b, s: (0, 0)),
                  pl.BlockSpec((None, d, D_GROUP), lambda b, s: (layer, 0, up_tile(b, s)))]
                 + [gate_spec(k) for k in range(per_tile)] + [row_spec(k) for k in range(per_tile)],
        out_specs=[pl.BlockSpec((DN_ROWS, D_GROUP), lambda b, s: (b * steps + s, 0)),
                   up_cols, up_cols,
                   pl.BlockSpec((D_GROUP, d), lambda b, s: (down_tile(b, s), 0))],
        out_shape=[jax.ShapeDtypeStruct((t, D_GROUP), BF16),
                   jax.ShapeDtypeStruct((d, D_FF_PAD), BF16),
                   jax.ShapeDtypeStruct((d, D_FF_PAD), BF16),
                   jax.ShapeDtypeStruct((D_FF_PAD, d), BF16)],
        scratch_shapes=[pltpu.VMEM((HEADS, HEAD_DIM, HEAD_DIM), F32)],
        compiler_params=_cparams("arbitrary", "arbitrary"),
        name="gated_deltanet",
    )(p, p, p, p, p, p, p, small, small_t3, conv_w, prm_row, prm_col, norm_g.reshape(1, HEAD_DIM),
      w_up, w_up, w_up, w_up, w_up, w_down, w_down, w_down, w_down)


def _relbias_kernel(q_ref, kc_ref, kp_ref, vc_ref, vp_ref, bias_ref, o_ref, k_scr, v_scr):
    i = pl.program_id(1)
    k_scr[0:ROWS_ATT, :] = kp_ref[...]
    k_scr[ROWS_ATT:, :] = kc_ref[...]
    v_scr[0:ROWS_ATT, :] = vp_ref[...]
    v_scr[ROWS_ATT:, :] = vc_ref[...]
    span = B_PREV_CHUNKS * CHUNK + Q_ROWS
    key_pos = lax.broadcasted_iota(jnp.int32, (Q_ROWS, span), 1)
    first_valid = jnp.where(i > 0, 0, ROWS_ATT)
    scale = HEAD_DIM ** -0.5
    subs = [sub * Q_ROWS for sub in range(ROWS_ATT // Q_ROWS)]
    for h in range(HEADS):
        lo, hi = h * HEAD_DIM, (h + 1) * HEAD_DIM
        bias = bias_ref[h]
        scores = [_dot_nt(q_ref[r0:r0 + Q_ROWS, lo:hi], k_scr[r0:r0 + span, lo:hi]) for r0 in subs]
        probs, invs = [], []
        for r0, s in zip(subs, scores):
            s = jnp.where(key_pos + r0 >= first_valid, s * scale + bias, -jnp.inf)
            e = jnp.exp(s - jnp.max(s, axis=-1, keepdims=True))
            probs.append(e.astype(BF16))
            invs.append(1.0 / jnp.sum(e, axis=-1, keepdims=True))
        outs = [jnp.dot(e, v_scr[r0:r0 + span, lo:hi], preferred_element_type=F32)
                for r0, e in zip(subs, probs)]
        for r0, o, inv in zip(subs, outs, invs):
            o_ref[r0:r0 + Q_ROWS, lo:hi] = (o * inv).astype(o_ref.dtype)


def _relbias_attention(p, bias, batch, seq):
    t = p.shape[0]
    nb = seq // ROWS_ATT

    def cur(j):
        return pl.BlockSpec((ROWS_ATT, D_GROUP), lambda b, i: (b * nb + i, j))

    def prev(j):
        return pl.BlockSpec((ROWS_ATT, D_GROUP), lambda b, i: (b * nb + jnp.maximum(i - 1, 0), j))

    return pl.pallas_call(
        _relbias_kernel,
        grid=(batch, nb),
        in_specs=[cur(COL_BQ), cur(COL_BK), prev(COL_BK), cur(COL_BV), prev(COL_BV),
                  pl.BlockSpec(bias.shape, lambda b, i: (0, 0, 0))],
        out_specs=pl.BlockSpec((ROWS_ATT, D_GROUP), lambda b, i: (b * nb + i, 0)),
        out_shape=jax.ShapeDtypeStruct((t, D_GROUP), BF16),
        scratch_shapes=[pltpu.VMEM((2 * ROWS_ATT, D_GROUP), BF16),
                        pltpu.VMEM((2 * ROWS_ATT, D_GROUP), BF16)],
        compiler_params=_cparams("parallel", "arbitrary"),
        name="relbias_attention",
    )(p, p, p, p, p, bias)


def _swa_kernel(sink_ref, q_ref, kvc_ref, kvp_ref, bias_ref, o_ref, kv_scr):
    i = pl.program_id(1)
    halo = C_PREV_CHUNKS * CHUNK
    kv_scr[0:halo, :] = kvp_ref[...]
    kv_scr[halo:, :] = kvc_ref[...]
    span = halo + Q_ROWS
    group = HEADS // C_KV_HEADS
    stacked = group * Q_ROWS
    key_pos = lax.broadcasted_iota(jnp.int32, (stacked, span), 1)
    q_row = lax.broadcasted_iota(jnp.int32, (stacked, 1), 0)
    first_valid = jnp.where(i > 0, 0, halo)
    scale = HEAD_DIM ** -0.5
    kv_width = C_KV_HEADS * HEAD_DIM
    for kvh in range(C_KV_HEADS):
        klo = kvh * HEAD_DIM
        h0 = kvh * group
        sink = jnp.where(q_row < Q_ROWS, sink_ref[h0], sink_ref[h0 + 1])
        bias = bias_ref[kvh]
        subs = [sub * Q_ROWS for sub in range(ROWS_ATT // Q_ROWS)]
        scores = []
        for r0 in subs:
            q = jnp.concatenate([q_ref[r0:r0 + Q_ROWS, (h0 + g) * HEAD_DIM:(h0 + g + 1) * HEAD_DIM]
                                 for g in range(group)], axis=0)
            scores.append(_dot_nt(q, kv_scr[r0:r0 + span, klo:klo + HEAD_DIM]))
        probs, invs = [], []
        for r0, s in zip(subs, scores):
            s = jnp.where(key_pos + r0 >= first_valid, s * scale + bias, -jnp.inf)
            m = jnp.maximum(jnp.max(s, axis=-1, keepdims=True), sink)
            e = jnp.exp(s - m)
            probs.append(e.astype(BF16))
            invs.append(1.0 / (jnp.sum(e, axis=-1, keepdims=True) + jnp.exp(sink - m)))
        vlo = kv_width + klo
        outs = [jnp.dot(e, kv_scr[r0:r0 + span, vlo:vlo + HEAD_DIM], preferred_element_type=F32)
                for r0, e in zip(subs, probs)]
        for r0, o, inv in zip(subs, outs, invs):
            o = o * inv
            for g in range(group):
                o_ref[r0:r0 + Q_ROWS, (h0 + g) * HEAD_DIM:(h0 + g + 1) * HEAD_DIM] = (
                    o[g * Q_ROWS:(g + 1) * Q_ROWS].astype(o_ref.dtype))


def _swa_attention(p, sinks, bias, batch, seq):
    t = p.shape[0]
    nb = seq // ROWS_ATT
    halo = C_PREV_CHUNKS * CHUNK
    halo_per_tile = ROWS_ATT // halo
    return pl.pallas_call(
        _swa_kernel,
        grid=(batch, nb),
        in_specs=[pl.BlockSpec(memory_space=pltpu.SMEM),
                  pl.BlockSpec((ROWS_ATT, D_GROUP), lambda b, i: (b * nb + i, COL_CQ)),
                  pl.BlockSpec((ROWS_ATT, D_GROUP), lambda b, i: (b * nb + i, COL_CKV)),
                  pl.BlockSpec((halo, D_GROUP),
                               lambda b, i: (jnp.maximum((b * nb + i) * halo_per_tile - 1, 0), COL_CKV)),
                  pl.BlockSpec(bias.shape, lambda b, i: (0, 0, 0))],
        out_specs=pl.BlockSpec((ROWS_ATT, D_GROUP), lambda b, i: (b * nb + i, 0)),
        out_shape=jax.ShapeDtypeStruct((t, D_GROUP), BF16),
        scratch_shapes=[pltpu.VMEM((halo + ROWS_ATT, D_GROUP), BF16)],
        compiler_params=_cparams("parallel", "arbitrary"),
        name="swa_attention",
    )(sinks, p, p, p, bias)


def _pair_mask(n_prev):
    span = n_prev * CHUNK + Q_ROWS
    r = np.arange(Q_ROWS)[:, None]
    j = np.arange(span)[None, :]
    band_pos = j - (r // CHUNK) * CHUNK
    return np.where((band_pos >= 0) & (band_pos < (n_prev + 1) * CHUNK), 0.0, -np.inf).astype(np.float32)


def _relbias_table(rel_bias):
    span = B_PREV_CHUNKS * CHUNK + Q_ROWS
    period = Q_ROWS + span
    k = np.arange(period) - (Q_ROWS - 1)
    idx = np.clip(B_PREV_CHUNKS * CHUNK - k, -REL_CLIP, REL_CLIP) + REL_CLIP
    g = rel_bias[:, idx]
    skew = jnp.tile(g, (1, Q_ROWS))[:, :Q_ROWS * (period - 1)].reshape(HEADS, Q_ROWS, period - 1)
    return skew[:, :, Q_ROWS - 1:Q_ROWS - 1 + span] + _pair_mask(B_PREV_CHUNKS)


def _alibi_table():
    halo = C_PREV_CHUNKS * CHUNK
    span = halo + Q_ROWS
    r = np.arange(Q_ROWS)[:, None]
    j = np.arange(span)[None, :]
    dist = np.abs(halo + r - j).astype(np.float32)
    per_head = [(-np.float32(s)) * dist + _pair_mask(C_PREV_CHUNKS) for s in ALIBI_SLOPES]
    group = HEADS // C_KV_HEADS
    return np.stack([np.concatenate(per_head[kv * group:(kv + 1) * group], axis=0)
                     for kv in range(C_KV_HEADS)], axis=0)


def _sgu_kernel(u_ref, v_ref, g_ref, w_ref, b_ref, o_ref):
    u = _gelu_tanh(u_ref[...].astype(F32))
    v = _gelu_tanh(v_ref[...].astype(F32))
    mu = jnp.mean(v, axis=-1, keepdims=True)
    var = jnp.mean(jnp.square(v - mu), axis=-1, keepdims=True)
    v = ((v - mu) * lax.rsqrt(var + EPS) * g_ref[...]).astype(BF16)
    row = lax.broadcasted_iota(jnp.int32, (SGU_BLOCK, SGU_BLOCK), 0)
    col = lax.broadcasted_iota(jnp.int32, (SGU_BLOCK, SGU_BLOCK), 1)
    for g in range(HEADS):
        lo, hi = g * HEAD_DIM, (g + 1) * HEAD_DIM
        w = jnp.where(row >= col, w_ref[g], 0.0).astype(BF16)
        bias = b_ref[:, g:g + 1]
        for n in range(ROWS_ATT // SGU_BLOCK):
            r0 = n * SGU_BLOCK
            mixed = jnp.dot(w, v[r0:r0 + SGU_BLOCK, lo:hi], preferred_element_type=F32) + bias
            o_ref[r0:r0 + SGU_BLOCK, lo:hi] = (u[r0:r0 + SGU_BLOCK, lo:hi] * mixed).astype(o_ref.dtype)


def _spatial_gating(p, norm_g, w_s, b_cols):
    t = p.shape[0]
    return pl.pallas_call(
        _sgu_kernel,
        grid=(t // ROWS_ATT,),
        in_specs=[pl.BlockSpec((ROWS_ATT, D_GROUP), lambda i: (i, COL_DU)),
                  pl.BlockSpec((ROWS_ATT, D_GROUP), lambda i: (i, COL_DV)),
                  pl.BlockSpec((1, D_GROUP), lambda i: (0, 0)),
                  pl.BlockSpec((HEADS, SGU_BLOCK, SGU_BLOCK), lambda i: (0, 0, 0)),
                  pl.BlockSpec((SGU_BLOCK, LANES), lambda i: (0, 0))],
        out_specs=pl.BlockSpec((ROWS_ATT, D_GROUP), lambda i: (i, 0)),
        out_shape=jax.ShapeDtypeStruct((t, D_GROUP), BF16),
        compiler_params=_cparams("parallel"),
        name="spatial_gating",
    )(p, p, norm_g.reshape(1, D_GROUP), w_s, b_cols)


def _outproj_kernel(oa_ref, ob_ref, oc_ref, od_ref, w_ref, x_ref, mod_ref, g_ref, o_ref, w_scr):
    @pl.when(pl.program_id(0) == 0)
    def _():
        w_scr[...] = w_ref[...].astype(BF16)

    half = ROWS_OUT // 2
    scale = mod_ref[0, 2:3, :] * g_ref[...]
    for r0 in (0, half):
        rows = slice(r0, r0 + half)
        y = jnp.dot(oa_ref[rows, :], w_scr[0:D_GROUP, :], preferred_element_type=F32)
        y += jnp.dot(ob_ref[rows, :], w_scr[D_GROUP:2 * D_GROUP, :], preferred_element_type=F32)
        y += jnp.dot(oc_ref[rows, :], w_scr[2 * D_GROUP:3 * D_GROUP, :], preferred_element_type=F32)
        y += jnp.dot(od_ref[rows, :], w_scr[3 * D_GROUP:, :], preferred_element_type=F32)

        def finish(sub, y=y, r0=r0):
            part = y[sub, :]
            part = part * lax.rsqrt(jnp.mean(part * part, axis=-1, keepdims=True) + EPS)
            o_ref[r0 + sub.start:r0 + sub.stop, :] = x_ref[r0 + sub.start:r0 + sub.stop, :] + part * scale

        _for_row_chunks(half, finish)


def _out_projection(outs, w_out, x2, mod, post_g, layer, seq):
    t, d = x2.shape
    tiles_per_batch = seq // ROWS_OUT
    mix = pl.BlockSpec((ROWS_OUT, D_GROUP), lambda m: (m, 0))
    return pl.pallas_call(
        _outproj_kernel,
        grid=(t // ROWS_OUT,),
        in_specs=[mix, mix, mix, mix,
                  pl.BlockSpec((None, HEADS * D_GROUP, d), lambda m: (layer, 0, 0),
                               pipeline_mode=pl.Buffered(1)),
                  pl.BlockSpec((ROWS_OUT, d), lambda m: (m, 0)),
                  pl.BlockSpec((1, 6, d), lambda m: (m // tiles_per_batch, 0, 0)),
                  pl.BlockSpec((1, d), lambda m: (0, 0))],
        out_specs=pl.BlockSpec((ROWS_OUT, d), lambda m: (m, 0)),
        out_shape=jax.ShapeDtypeStruct((t, d), F32),
        scratch_shapes=[pltpu.VMEM((HEADS * D_GROUP, d), BF16)],
        compiler_params=_cparams("arbitrary"),
        name="out_projection",
    )(*outs, w_out, x2, mod, post_g.reshape(1, d))


def _ffn_kernel(x_ref, xp_ref, mod_ref, pre_ref, post_ref, wa_ref, wg_ref, wd_ref, cw_ref, cb_ref,
                o_ref, h_scr, a_scr, *, tiles_per_batch):
    m = pl.program_id(0)
    f = pl.program_id(1)

    @pl.when(f == 0)
    def _():
        gain = pre_ref[...] * (1.0 + mod_ref[0, 4:5, :])
        shift = mod_ref[0, 3:4, :]
        h_scr[0:HALO, :] = _modulated_rms(xp_ref[...], gain, shift).astype(BF16)

        def fill(rows):
            h_scr[HALO + rows.start:HALO + rows.stop, :] = (
                _modulated_rms(x_ref[rows, :], gain, shift).astype(BF16))
            o_ref[rows, :] = jnp.zeros((ROW_CHUNK, o_ref.shape[1]), F32)

        _for_row_chunks(ROWS_OUT, fill)

    keep = (m % tiles_per_batch != 0).astype(F32)
    a = jnp.dot(h_scr[...], wa_ref[...], preferred_element_type=F32)
    a_scr[0:HALO, :] = a[0:HALO] * keep
    a_scr[HALO:, :] = a[HALO:]
    gate = jnp.dot(h_scr[HALO:, :], wg_ref[...], preferred_element_type=F32)
    acc = cb_ref[...]
    for tap in range(FFN_CONV):
        off = HALO - (FFN_CONV - 1) + tap
        acc = acc + cw_ref[tap:tap + 1, :] * a_scr[off:off + ROWS_OUT, :]
    act = (_gelu_tanh(acc) * gate).astype(BF16)
    o_ref[...] += jnp.dot(act, wd_ref[...], preferred_element_type=F32)

    @pl.when(f == pl.num_programs(1) - 1)
    def _():
        scale = mod_ref[0, 5:6, :] * post_ref[...]

        def finish(rows):
            y = o_ref[rows, :]
            y = y * lax.rsqrt(jnp.mean(y * y, axis=-1, keepdims=True) + EPS)
            o_ref[rows, :] = x_ref[rows, :] + y * scale

        _for_row_chunks(ROWS_OUT, finish)


def _conv_ffn(x2, mod, pre_g, post_g, w_a, w_g, w_d, conv_w, conv_b, layer, seq):
    t, d = x2.shape
    tiles_per_batch = seq // ROWS_OUT
    halo_per_tile = ROWS_OUT // HALO
    return pl.pallas_call(
        functools.partial(_ffn_kernel, tiles_per_batch=tiles_per_batch),
        grid=(t // ROWS_OUT, D_FF_PAD // D_GROUP),
        in_specs=[pl.BlockSpec((ROWS_OUT, d), lambda m, f: (m, 0)),
                  pl.BlockSpec((HALO, d), lambda m, f: (jnp.maximum(m * halo_per_tile - 1, 0), 0)),
                  pl.BlockSpec((1, 6, d), lambda m, f: (m // tiles_per_batch, 0, 0)),
                  pl.BlockSpec((1, d), lambda m, f: (0, 0)),
                  pl.BlockSpec((1, d), lambda m, f: (0, 0)),
                  pl.BlockSpec((d, D_GROUP), lambda m, f: (0, f)),
                  pl.BlockSpec((d, D_GROUP), lambda m, f: (0, f)),
                  pl.BlockSpec((D_GROUP, d), lambda m, f: (f, 0)),
                  pl.BlockSpec((None, FFN_CONV, D_GROUP), lambda m, f: (layer, 0, f)),
                  pl.BlockSpec((None, 1, D_GROUP), lambda m, f: (layer, 0, f))],
        out_specs=pl.BlockSpec((ROWS_OUT, d), lambda m, f: (m, 0)),
        out_shape=jax.ShapeDtypeStruct((t, d), F32),
        scratch_shapes=[pltpu.VMEM((HALO + ROWS_OUT, d), BF16),
                        pltpu.VMEM((HALO + ROWS_OUT, D_GROUP), F32)],
        compiler_params=_cparams("parallel", "arbitrary"),
        name="conv_ffn",
    )(x2, x2, mod, pre_g.reshape(1, d), post_g.reshape(1, d), w_a, w_g, w_d, conv_w, conv_b)


def _pad_last(a, width):
    return jnp.pad(a, [(0, 0)] * (a.ndim - 1) + [(0, width - a.shape[-1])])


def _layer(x2, mod, layer, batch, seq, weights, mix_pre_g, mix_post_g, dn_conv_w, dn_a_log, dn_dt_bias,
           dn_norm_g, rel_bias, sinks, sgu_norm_g, sgu_w, sgu_b, ffn_pre_g, ffn_post_g):
    t = x2.shape[0]
    p, small, small_t = _in_projection(x2, mod, mix_pre_g, weights["in_full"], weights["in_tail"],
                                       weights["in_small"], layer, seq)
    small_t3 = small_t.reshape(SUBLANES, t // CHUNK, CHUNK).transpose(1, 0, 2)

    prm = jnp.stack([dn_dt_bias, dn_a_log], axis=0)
    prm_row = jnp.zeros((SUBLANES, LANES), F32).at[0:2, HEADS:2 * HEADS].set(prm)
    prm_col = jnp.zeros((SUBLANES, LANES), F32).at[HEADS:2 * HEADS, 0:2].set(prm.T)
    out_a, w_a, w_g, w_d = _gated_deltanet(p, small, small_t3, dn_conv_w, prm_row, prm_col, dn_norm_g,
                                           weights["ffn_up"], weights["ffn_down"], layer, batch, seq)
    out_b = _relbias_attention(p, _relbias_table(rel_bias), batch, seq)
    out_c = _swa_attention(p, sinks, jnp.asarray(_alibi_table()), batch, seq)
    out_d = _spatial_gating(p, sgu_norm_g, sgu_w, _pad_last(sgu_b.T, LANES))

    x2 = _out_projection((out_a, out_b, out_c, out_d), weights["out"], x2, mod, mix_post_g, layer, seq)
    return _conv_ffn(x2, mod, ffn_pre_g, ffn_post_g, w_a, w_g, w_d,
                     weights["ffn_conv_w"], weights["ffn_conv_b"], layer, seq)


def _prepare_weights(w_in, w_out, ffn_w_up, ffn_conv_w, ffn_conv_b, ffn_w_down):
    small_lo = N_LO_TILES * D_GROUP
    w_in_t = jnp.swapaxes(w_in, 1, 2)
    w_small = w_in_t[:, small_lo:small_lo + N_SMALL, :]
    return {
        "in_full": w_in_t,
        "in_tail": w_in_t,
        "in_small": jnp.pad(w_small, ((0, 0), (0, LANES - N_SMALL), (0, 0))).astype(BF16),
        "out": w_out,
        "ffn_up": ffn_w_up,
        "ffn_down": ffn_w_down,
        "ffn_conv_w": _pad_last(ffn_conv_w, D_FF_PAD),
        "ffn_conv_b": _pad_last(ffn_conv_b[:, None, :], D_FF_PAD),
    }


def kernel(x, c, ada_w, ada_b, mix_pre_g, mix_post_g, w_in, dn_conv_w, dn_a_log, dn_dt_bias, dn_norm_g,
           rel_bias, sinks, sgu_norm_g, sgu_w, sgu_b, w_out, ffn_pre_g, ffn_post_g, ffn_w_up, ffn_conv_w,
           ffn_conv_b, ffn_w_down):
    batch, seq, d = x.shape
    depth = ada_w.shape[0]
    assert d == D_MODEL and seq % ROWS_PROJ == 0 and batch <= SUBLANES
    c_pad = jnp.pad(c, ((0, SUBLANES - batch), (0, 0)))
    mod_all = _modulation(c_pad, ada_w, ada_b)[:, :batch].reshape(depth, batch, 6, d)
    weights = _prepare_weights(w_in, w_out, ffn_w_up, ffn_conv_w, ffn_conv_b, ffn_w_down)
    x2 = x.reshape(batch * seq, d)
    for l in range(depth):
        x2 = _layer(x2, mod_all[l], l, batch, seq, weights, mix_pre_g[l], mix_post_g[l], dn_conv_w[l],
                    dn_a_log[l], dn_dt_bias[l], dn_norm_g[l], rel_bias[l], sinks[l], sgu_norm_g[l],
                    sgu_w[l], sgu_b[l], ffn_pre_g[l], ffn_post_g[l])
    return x2.reshape(batch, seq, d)
```

```python
import functools

import numpy as np
import jax
import jax.numpy as jnp
from jax import lax
from jax.experimental import pallas as pl
from jax.experimental.pallas import tpu as pltpu

F32 = jnp.float32
BF16 = jnp.bfloat16

D_MODEL = 2048
CHUNK = 64
HEAD_DIM = 128
HEADS = 4
D_GROUP = HEADS * HEAD_DIM
DN_CONV = 4
B_PREV_CHUNKS = 8
REL_CLIP = 256
C_PREV_CHUNKS = 2
C_KV_HEADS = 2
SGU_BLOCK = 128
D_FF = 5504
FFN_CONV = 3
EPS = 1e-6

LANES = 128
SUBLANES = 8
VMEM_LIMIT_BYTES = 56 * 1024 * 1024

N_SMALL = 2 * HEADS
N_MAIN = 11 * D_GROUP
N_LO_TILES = 4
COL_AQ, COL_AK, COL_AV, COL_AG, COL_BQ, COL_BK, COL_BV, COL_CQ, COL_CKV, COL_DU, COL_DV = range(11)
D_FF_PAD = 11 * D_GROUP

ROWS_ATT = 1024
Q_ROWS = 2 * CHUNK
DN_CHUNKS = 4
DN_ROWS = DN_CHUNKS * CHUNK
HALO = 16
ROW_CHUNK = 16
ROWS_PROJ = 2048
ROWS_OUT = 512
MOD_COLS = 1024

ALIBI_SLOPES = tuple(float(2.0 ** (-8.0 * h / HEADS)) for h in range(1, HEADS + 1))


def _cparams(*sem):
    return pltpu.CompilerParams(dimension_semantics=sem, vmem_limit_bytes=VMEM_LIMIT_BYTES)


def _sigmoid(x):
    return 1.0 / (1.0 + jnp.exp(-x))


def _silu(x):
    return x * _sigmoid(x)


def _softplus(x):
    return jnp.maximum(x, 0.0) + jnp.log1p(jnp.exp(-jnp.abs(x)))


def _gelu_tanh(x):
    return 0.5 * x * (1.0 + jnp.tanh(float(np.sqrt(2.0 / np.pi)) * (x + 0.044715 * (x * x * x))))


def _rms(x, g):
    return x * lax.rsqrt(jnp.mean(x * x, axis=-1, keepdims=True) + EPS) * g


def _modulated_rms(x, gain, shift):
    return x * lax.rsqrt(jnp.mean(x * x, axis=-1, keepdims=True) + EPS) * gain + shift


def _for_row_chunks(n_rows, body):
    for r0 in range(0, n_rows, ROW_CHUNK):
        body(slice(r0, r0 + ROW_CHUNK))


def _dot(a, b):
    return jnp.dot(a.astype(BF16), b.astype(BF16), preferred_element_type=F32)


def _dot_nt(a, b):
    return lax.dot_general(a.astype(BF16), b.astype(BF16), (((1,), (1,)), ((), ())),
                           preferred_element_type=F32)


def _dot_f32(a, b):
    return jnp.dot(a, b, precision=lax.Precision.HIGHEST, preferred_element_type=F32)


def _mod_kernel(c_ref, w_ref, b_ref, o_ref):
    cond = _silu(c_ref[...])
    o_ref[0] = _dot(cond, w_ref[0]) + b_ref[0]


def _modulation(c_pad, ada_w, ada_b):
    depth, d, n = ada_w.shape
    return pl.pallas_call(
        _mod_kernel,
        grid=(depth, n // MOD_COLS),
        in_specs=[pl.BlockSpec((SUBLANES, d), lambda l, j: (0, 0)),
                  pl.BlockSpec((1, d, MOD_COLS), lambda l, j: (l, 0, j)),
                  pl.BlockSpec((1, 1, MOD_COLS), lambda l, j: (l, 0, j))],
        out_specs=pl.BlockSpec((1, SUBLANES, MOD_COLS), lambda l, j: (l, 0, j)),
        out_shape=jax.ShapeDtypeStruct((depth, SUBLANES, n), F32),
        compiler_params=_cparams("parallel", "parallel"),
        name="modulation",
    )(c_pad, ada_w, ada_b.reshape(depth, 1, n))


def _inproj_kernel(x_hbm, mod_ref, g_ref, wlo_ref, whi_ref, ws_ref, p_ref, s_ref, st_ref, h_scr, x_ref, x_sem):
    m = pl.program_id(0)
    n = pl.program_id(1)

    def x_copy(tile):
        rows = pl.ds(pl.multiple_of(tile * ROWS_PROJ, ROWS_PROJ), ROWS_PROJ)
        return pltpu.make_async_copy(x_hbm.at[rows, :], x_ref, x_sem)

    @pl.when((m == 0) & (n == 0))
    def _():
        x_copy(0).start()

    @pl.when((n == 1) & (m + 1 < pl.num_programs(0)))
    def _():
        x_copy(m + 1).start()

    @pl.when(n == 0)
    def _():
        x_copy(m).wait()
        gain = g_ref[...] * (1.0 + mod_ref[0, 1:2, :])
        shift = mod_ref[0, 0:1, :]

        def fill(rows):
            h_scr[rows, :] = _modulated_rms(x_ref[rows, :], gain, shift).astype(BF16)

        _for_row_chunks(ROWS_PROJ, fill)
        small_t = _dot_nt(ws_ref[...], h_scr[...])
        s_ref[...] = small_t.T
        st_ref[...] = small_t[0:SUBLANES, :]

    @pl.when(n < N_LO_TILES)
    def _():
        p_ref[...] = _dot_nt(h_scr[...], wlo_ref[...]).astype(p_ref.dtype)

    @pl.when(n >= N_LO_TILES)
    def _():
        p_ref[...] = _dot_nt(h_scr[...], whi_ref[0]).astype(p_ref.dtype)


def _in_projection(x2, mod, pre_g, w_full, w_tail, w_small, layer, seq):
    t, d = x2.shape
    tiles_per_batch = seq // ROWS_PROJ
    last_tile = N_MAIN // D_GROUP - 1
    return pl.pallas_call(
        _inproj_kernel,
        grid=(t // ROWS_PROJ, N_MAIN // D_GROUP),
        in_specs=[pl.BlockSpec(memory_space=pl.ANY),
                  pl.BlockSpec((1, 6, d), lambda m, n: (m // tiles_per_batch, 0, 0)),
                  pl.BlockSpec((1, d), lambda m, n: (0, 0)),
                  pl.BlockSpec((None, D_GROUP, d), lambda m, n: (layer, jnp.where(n < N_LO_TILES, n, 0), 0)),
                  pl.BlockSpec((pl.Element(1), pl.Element(D_GROUP), pl.Element(d)),
                               lambda m, n: (layer, pl.multiple_of(
                                   N_SMALL + D_GROUP * jnp.where(n < N_LO_TILES, last_tile, n), SUBLANES), 0)),
                  pl.BlockSpec((None, LANES, d), lambda m, n: (layer, 0, 0))],
        out_specs=[pl.BlockSpec((ROWS_PROJ, D_GROUP), lambda m, n: (m, n)),
                   pl.BlockSpec((ROWS_PROJ, LANES), lambda m, n: (m, 0)),
                   pl.BlockSpec((SUBLANES, ROWS_PROJ), lambda m, n: (0, m))],
        out_shape=[jax.ShapeDtypeStruct((t, N_MAIN), BF16),
                   jax.ShapeDtypeStruct((t, LANES), F32),
                   jax.ShapeDtypeStruct((SUBLANES, t), F32)],
        scratch_shapes=[pltpu.VMEM((ROWS_PROJ, d), BF16),
                        pltpu.VMEM((ROWS_PROJ, d), F32),
                        pltpu.SemaphoreType.DMA(())],
        compiler_params=_cparams("arbitrary", "arbitrary"),
        name="in_projection",
    )(x2, mod, pre_g.reshape(1, d), w_full, w_tail, w_small)


def _unit_lower_inverse(mats, row, col):
    blk_r, blk_c = jnp.right_shift(row, 4), jnp.right_shift(col, 4)
    half_r, half_c = jnp.right_shift(row, 5), jnp.right_shift(col, 5)
    eye = (row == col).astype(F32)
    a_diag = [jnp.where(blk_r == blk_c, a, 0.0) for a in mats]
    a_pair = [jnp.where(half_r == half_c, a, 0.0) - d for a, d in zip(mats, a_diag)]
    a_far = [jnp.where(half_r != half_c, a, 0.0) for a in mats]
    p2 = [_dot(d, d) for d in a_diag]
    t = [eye - d for d in a_diag]
    t = [x + _dot(x, p) for x, p in zip(t, p2)]
    p4 = [_dot(p, p) for p in p2]
    t = [x + _dot(x, p) for x, p in zip(t, p4)]
    p8 = [_dot(p, p) for p in p4]
    t = [x + _dot(x, p) for x, p in zip(t, p8)]
    for off in (a_pair, a_far):
        left = [_dot(x, a) for x, a in zip(t, off)]
        t = [x - _dot(l, x) for x, l in zip(t, left)]
    return t


def _ffn_layout_step(tile, up_refs, down_refs, wa_ref, wg_ref, wd_ref):
    n_tiles = D_FF_PAD // D_GROUP

    @pl.when(tile < n_tiles)
    def _():
        col = tile * D_GROUP + lax.broadcasted_iota(jnp.int32, (1, D_GROUP), 1)
        valid = col < D_FF
        gate = jnp.concatenate([r[...] for r in up_refs[1:]], axis=1)
        wa_ref[...] = jnp.where(valid, up_refs[0][...], 0.0).astype(BF16)
        wg_ref[...] = jnp.where(valid, gate, 0.0).astype(BF16)

    @pl.when((tile >= n_tiles) & (tile < 2 * n_tiles))
    def _():
        row = (tile - n_tiles) * D_GROUP + lax.broadcasted_iota(jnp.int32, (D_GROUP, 1), 0)
        rows = jnp.concatenate([r[...] for r in down_refs], axis=0)
        wd_ref[...] = jnp.where(row < D_FF, rows, 0.0).astype(BF16)


def _delta_kernel(q_ref, k_ref, v_ref, qp_ref, kp_ref, vp_ref, gate_ref, sm_ref, smt_ref, cw_ref,
                  prow_ref, pcol_ref, ng_ref, ua_ref, ug0_ref, ug1_ref, ug2_ref, ug3_ref,
                  dr0_ref, dr1_ref, dr2_ref, dr3_ref, o_ref, wa_ref, wg_ref, wd_ref, state_scr):
    step = pl.program_id(1)
    _ffn_layout_step(pl.program_id(0) * pl.num_programs(1) + step,
                     (ua_ref, ug0_ref, ug1_ref, ug2_ref, ug3_ref),
                     (dr0_ref, dr1_ref, dr2_ref, dr3_ref), wa_ref, wg_ref, wd_ref)

    @pl.when(step == 0)
    def _():
        state_scr[...] = jnp.zeros_like(state_scr)

    keep = (step > 0).astype(F32)
    halo = jnp.concatenate([qp_ref[...], kp_ref[...], vp_ref[...]], axis=1).astype(F32) * keep
    cur = jnp.concatenate([q_ref[...], k_ref[...], v_ref[...]], axis=1).astype(F32)
    xe = jnp.concatenate([halo, cur], axis=0)
    acc = cw_ref[DN_CONV - 1:DN_CONV, :] * cur
    for back in range(1, DN_CONV):
        shifted = pltpu.roll(xe, back, 0)[HALO:, :]
        acc = acc + cw_ref[DN_CONV - 1 - back:DN_CONV - back, :] * shifted
    qkv = _silu(acc)

    row = lax.broadcasted_iota(jnp.int32, (CHUNK, CHUNK), 0)
    col = lax.broadcasted_iota(jnp.int32, (CHUNK, CHUNK), 1)
    causal = row >= col
    strict = row > col
    lower_ones = causal.astype(F32)
    upper_ones = (row <= col).astype(F32)

    sm = sm_ref[...]
    la_cols = -jnp.exp(prow_ref[1:2, :]) * _softplus(sm + prow_ref[0:1, :])
    beta_cols = _sigmoid(sm)
    neg_a_col = -jnp.exp(pcol_ref[:, 1:2])
    dt_col = pcol_ref[:, 0:1]

    scale = HEAD_DIM ** -0.5
    qs, ks, vbs, kbs = [], [], [], []
    for h in range(HEADS):
        lo, hi = h * HEAD_DIM, (h + 1) * HEAD_DIM
        q = qkv[:, lo:hi]
        k = qkv[:, D_GROUP + lo:D_GROUP + hi]
        v = qkv[:, 2 * D_GROUP + lo:2 * D_GROUP + hi]
        beta = beta_cols[:, h:h + 1]
        k = k * lax.rsqrt(jnp.sum(k * k, axis=-1, keepdims=True) + EPS)
        qs.append(q * lax.rsqrt(jnp.sum(q * q, axis=-1, keepdims=True) + EPS) * scale)
        ks.append(k)
        kbs.append(k * beta)
        vbs.append(v * beta)

    probs = [(c, h) for c in range(DN_CHUNKS) for h in range(HEADS)]
    g_cols, g_rows = [], []
    for c in range(DN_CHUNKS):
        rows = slice(c * CHUNK, (c + 1) * CHUNK)
        g_cols.append(_dot_f32(lower_ones, la_cols[rows]))
        g_rows.append(_dot_f32(neg_a_col * _softplus(smt_ref[c] + dt_col), upper_ones))

    def rows_of(c):
        return slice(c * CHUNK, (c + 1) * CHUNK)

    g_col = [g_cols[c][:, HEADS + h:HEADS + h + 1] for c, h in probs]
    g_last = [g[CHUNK - 1:CHUNK, :] for g in g_col]
    decay = [jnp.exp(jnp.where(causal, g - g_rows[c][HEADS + h:HEADS + h + 1, :], -jnp.inf))
             for g, (c, h) in zip(g_col, probs)]
    kk = [_dot_nt(kbs[h][rows_of(c)], ks[h][rows_of(c)]) for c, h in probs]
    qk = [_dot_nt(qs[h][rows_of(c)], ks[h][rows_of(c)]) for c, h in probs]
    t_inv = _unit_lower_inverse([jnp.where(strict, x * d, 0.0) for x, d in zip(kk, decay)], row, col)
    exp_g = [jnp.exp(g) for g in g_col]
    uw = [_dot(t, jnp.concatenate([vbs[h][rows_of(c)], kbs[h][rows_of(c)] * e], axis=1))
          for t, e, (c, h) in zip(t_inv, exp_g, probs)]
    attn = [(x * d).astype(BF16) for x, d in zip(qk, decay)]
    wq = [jnp.concatenate([x[:, HEAD_DIM:], qs[h][rows_of(c)] * e], axis=0).astype(BF16)
          for x, e, (c, h) in zip(uw, exp_g, probs)]
    k_dec_t = [(ks[h][rows_of(c)] * jnp.exp(gl - g)).T.astype(BF16)
               for g, gl, (c, h) in zip(g_col, g_last, probs)]
    decay_last = [jnp.exp(gl) for gl in g_last]

    state = [state_scr[h] for h in range(HEADS)]
    for c in range(DN_CHUNKS):
        idx = [c * HEADS + h for h in range(HEADS)]
        ws_qs = [jnp.dot(wq[i], state[h].astype(BF16), preferred_element_type=F32)
                 for h, i in enumerate(idx)]
        v_new = [(uw[i][:, :HEAD_DIM] - x[:CHUNK]).astype(BF16) for x, i in zip(ws_qs, idx)]
        state = [state[h] * decay_last[i] + jnp.dot(k_dec_t[i], v_new[h], preferred_element_type=F32)
                 for h, i in enumerate(idx)]
        for h, i in enumerate(idx):
            lo, hi = h * HEAD_DIM, (h + 1) * HEAD_DIM
            o = ws_qs[h][CHUNK:] + jnp.dot(attn[i], v_new[h], preferred_element_type=F32)
            o = _rms(o, ng_ref[...]) * _silu(gate_ref[rows_of(c), lo:hi].astype(F32))
            o_ref[rows_of(c), lo:hi] = o.astype(o_ref.dtype)
    for h in range(HEADS):
        state_scr[h] = state[h]


def _gated_deltanet(p, small, small_t3, conv_w, prm_row, prm_col, norm_g, w_up, w_down, layer, batch, seq):
    t = p.shape[0]
    d = w_up.shape[1]
    steps = seq // DN_ROWS
    halo_per_step = DN_ROWS // HALO
    per_tile = D_GROUP // LANES
    n_in = D_FF // LANES
    n_tiles = D_FF_PAD // D_GROUP
    assert batch * steps >= 2 * n_tiles

    def cur(j):
        return pl.BlockSpec((DN_ROWS, D_GROUP), lambda b, s: (b * steps + s, j))

    def prev(j):
        return pl.BlockSpec((HALO, D_GROUP),
                            lambda b, s: (jnp.maximum((b * steps + s) * halo_per_step - 1, 0), j))

    def up_tile(b, s):
        return jnp.minimum(b * steps + s, n_tiles - 1)

    def down_tile(b, s):
        return jnp.clip(b * steps + s - n_tiles, 0, n_tiles - 1)

    def gate_spec(k):
        return pl.BlockSpec((None, d, LANES), lambda b, s: (
            layer, 0, jnp.minimum(n_in + per_tile * up_tile(b, s) + k, 2 * n_in - 1)))

    def row_spec(k):
        return pl.BlockSpec((None, LANES, d), lambda b, s: (
            layer, jnp.minimum(per_tile * down_tile(b, s) + k, n_in - 1), 0))

    up_cols = pl.BlockSpec((d, D_GROUP), lambda b, s: (0, up_tile(b, s)))
    return pl.pallas_call(
        _delta_kernel,
        grid=(batch, steps),
        in_specs=[cur(COL_AQ), cur(COL_AK), cur(COL_AV), prev(COL_AQ), prev(COL_AK), prev(COL_AV),
                  cur(COL_AG),
                  pl.BlockSpec((DN_ROWS, LANES), lambda b, s: (b * steps + s, 0)),
                  pl.BlockSpec((DN_CHUNKS, SUBLANES, CHUNK), lambda b, s: (b * steps + s, 0, 0)),
                  pl.BlockSpec((DN_CONV, 3 * D_GROUP), lambda b, s: (0, 0)),
                  pl.BlockSpec((SUBLANES, LANES), lambda b, s: (0, 0)),
                  pl.BlockSpec((SUBLANES, LANES), lambda b, s: (0, 0)),
                  pl.BlockSpec((1, HEAD_DIM), lambda b, s: (0, 0)),
                  pl.BlockSpec((None, d, D_GROUP), lambda b, s: (layer, 0, up_tile(b, s)))]
                 + [gate_spec(k) for k in range(per_tile)] + [row_spec(k) for k in range(per_tile)],
        out_specs=[pl.BlockSpec((DN_ROWS, D_GROUP), lambda b, s: (b * steps + s, 0)),
                   up_cols, up_cols,
                   pl.BlockSpec((D_GROUP, d), lambda b, s: (down_tile(b, s), 0))],
        out_shape=[jax.ShapeDtypeStruct((t, D_GROUP), BF16),
                   jax.ShapeDtypeStruct((d, D_FF_PAD), BF16),
                   jax.ShapeDtypeStruct((d, D_FF_PAD), BF16),
                   jax.ShapeDtypeStruct((D_FF_PAD, d), BF16)],
        scratch_shapes=[pltpu.VMEM((HEADS, HEAD_DIM, HEAD_DIM), F32)],
        compiler_params=_cparams("arbitrary", "arbitrary"),
        name="gated_deltanet",
    )(p, p, p, p, p, p, p, small, small_t3, conv_w, prm_row, prm_col, norm_g.reshape(1, HEAD_DIM),
      w_up, w_up, w_up, w_up, w_up, w_down, w_down, w_down, w_down)


def _relbias_kernel(q_ref, kc_ref, kp_ref, vc_ref, vp_ref, bias_ref, o_ref, k_scr, v_scr):
    i = pl.program_id(1)
    halo = B_PREV_CHUNKS * CHUNK
    k_scr[0:halo, :] = kp_ref[...]
    k_scr[halo:, :] = kc_ref[...]
    v_scr[0:halo, :] = vp_ref[...]
    v_scr[halo:, :] = vc_ref[...]
    span = halo + Q_ROWS
    key_pos = lax.broadcasted_iota(jnp.int32, (Q_ROWS, span), 1)
    first_valid = jnp.where(i > 0, 0, halo)
    scale = HEAD_DIM ** -0.5
    subs = [sub * Q_ROWS for sub in range(ROWS_ATT // Q_ROWS)]
    for h in range(HEADS):
        lo, hi = h * HEAD_DIM, (h + 1) * HEAD_DIM
        bias = bias_ref[h]
        scores = [_dot_nt(q_ref[r0:r0 + Q_ROWS, lo:hi], k_scr[r0:r0 + span, lo:hi]) for r0 in subs]
        probs, invs = [], []
        for r0, s in zip(subs, scores):
            s = jnp.where(key_pos + r0 >= first_valid, s * scale + bias, -jnp.inf)
            e = jnp.exp(s - jnp.max(s, axis=-1, keepdims=True))
            probs.append(e.astype(BF16))
            invs.append(1.0 / jnp.sum(e, axis=-1, keepdims=True))
        outs = [jnp.dot(e, v_scr[r0:r0 + span, lo:hi], preferred_element_type=F32)
                for r0, e in zip(subs, probs)]
        for r0, o, inv in zip(subs, outs, invs):
            o_ref[r0:r0 + Q_ROWS, lo:hi] = (o * inv).astype(o_ref.dtype)


def _relbias_attention(p, bias, batch, seq):
    t = p.shape[0]
    nb = seq // ROWS_ATT

    def cur(j):
        return pl.BlockSpec((ROWS_ATT, D_GROUP), lambda b, i: (b * nb + i, j))

    halo = B_PREV_CHUNKS * CHUNK
    halo_per_tile = ROWS_ATT // halo

    def prev(j):
        return pl.BlockSpec((halo, D_GROUP),
                            lambda b, i: (jnp.maximum((b * nb + i) * halo_per_tile - 1, 0), j))

    return pl.pallas_call(
        _relbias_kernel,
        grid=(batch, nb),
        in_specs=[cur(COL_BQ), cur(COL_BK), prev(COL_BK), cur(COL_BV), prev(COL_BV),
                  pl.BlockSpec(bias.shape, lambda b, i: (0, 0, 0))],
        out_specs=pl.BlockSpec((ROWS_ATT, D_GROUP), lambda b, i: (b * nb + i, 0)),
        out_shape=jax.ShapeDtypeStruct((t, D_GROUP), BF16),
        scratch_shapes=[pltpu.VMEM((halo + ROWS_ATT, D_GROUP), BF16),
                        pltpu.VMEM((halo + ROWS_ATT, D_GROUP), BF16)],
        compiler_params=_cparams("parallel", "arbitrary"),
        name="relbias_attention",
    )(p, p, p, p, p, bias)


def _swa_kernel(sink_ref, q_ref, kvc_ref, kvp_ref, bias_ref, o_ref, kv_scr):
    i = pl.program_id(1)
    halo = C_PREV_CHUNKS * CHUNK
    kv_scr[0:halo, :] = kvp_ref[...]
    kv_scr[halo:, :] = kvc_ref[...]
    span = halo + Q_ROWS
    group = HEADS // C_KV_HEADS
    stacked = group * Q_ROWS
    key_pos = lax.broadcasted_iota(jnp.int32, (stacked, span), 1)
    q_row = lax.broadcasted_iota(jnp.int32, (stacked, 1), 0)
    first_valid = jnp.where(i > 0, 0, halo)
    scale = HEAD_DIM ** -0.5
    kv_width = C_KV_HEADS * HEAD_DIM
    for kvh in range(C_KV_HEADS):
        klo = kvh * HEAD_DIM
        h0 = kvh * group
        sink = jnp.where(q_row < Q_ROWS, sink_ref[h0], sink_ref[h0 + 1])
        bias = bias_ref[kvh]
        subs = [sub * Q_ROWS for sub in range(ROWS_ATT // Q_ROWS)]
        scores = []
        for r0 in subs:
            q = jnp.concatenate([q_ref[r0:r0 + Q_ROWS, (h0 + g) * HEAD_DIM:(h0 + g + 1) * HEAD_DIM]
                                 for g in range(group)], axis=0)
            scores.append(_dot_nt(q, kv_scr[r0:r0 + span, klo:klo + HEAD_DIM]))
        probs, invs = [], []
        for r0, s in zip(subs, scores):
            s = jnp.where(key_pos + r0 >= first_valid, s * scale + bias, -jnp.inf)
            m = jnp.maximum(jnp.max(s, axis=-1, keepdims=True), sink)
            e = jnp.exp(s - m)
            probs.append(e.astype(BF16))
            invs.append(1.0 / (jnp.sum(e, axis=-1, keepdims=True) + jnp.exp(sink - m)))
        vlo = kv_width + klo
        outs = [jnp.dot(e, kv_scr[r0:r0 + span, vlo:vlo + HEAD_DIM], preferred_element_type=F32)
                for r0, e in zip(subs, probs)]
        for r0, o, inv in zip(subs, outs, invs):
            o = o * inv
            for g in range(group):
                o_ref[r0:r0 + Q_ROWS, (h0 + g) * HEAD_DIM:(h0 + g + 1) * HEAD_DIM] = (
                    o[g * Q_ROWS:(g + 1) * Q_ROWS].astype(o_ref.dtype))


def _swa_attention(p, sinks, bias, batch, seq):
    t = p.shape[0]
    nb = seq // ROWS_ATT
    halo = C_PREV_CHUNKS * CHUNK
    halo_per_tile = ROWS_ATT // halo
    return pl.pallas_call(
        _swa_kernel,
        grid=(batch, nb),
        in_specs=[pl.BlockSpec(memory_space=pltpu.SMEM),
                  pl.BlockSpec((ROWS_ATT, D_GROUP), lambda b, i: (b * nb + i, COL_CQ)),
                  pl.BlockSpec((ROWS_ATT, D_GROUP), lambda b, i: (b * nb + i, COL_CKV)),
                  pl.BlockSpec((halo, D_GROUP),
                               lambda b, i: (jnp.maximum((b * nb + i) * halo_per_tile - 1, 0), COL_CKV)),
                  pl.BlockSpec(bias.shape, lambda b, i: (0, 0, 0))],
        out_specs=pl.BlockSpec((ROWS_ATT, D_GROUP), lambda b, i: (b * nb + i, 0)),
        out_shape=jax.ShapeDtypeStruct((t, D_GROUP), BF16),
        scratch_shapes=[pltpu.VMEM((halo + ROWS_ATT, D_GROUP), BF16)],
        compiler_params=_cparams("parallel", "arbitrary"),
        name="swa_attention",
    )(sinks, p, p, p, bias)


def _pair_mask(n_prev):
    span = n_prev * CHUNK + Q_ROWS
    r = np.arange(Q_ROWS)[:, None]
    j = np.arange(span)[None, :]
    band_pos = j - (r // CHUNK) * CHUNK
    return np.where((band_pos >= 0) & (band_pos < (n_prev + 1) * CHUNK), 0.0, -np.inf).astype(np.float32)


def _relbias_table(rel_bias):
    span = B_PREV_CHUNKS * CHUNK + Q_ROWS
    period = Q_ROWS + span
    k = np.arange(period) - (Q_ROWS - 1)
    idx = np.clip(B_PREV_CHUNKS * CHUNK - k, -REL_CLIP, REL_CLIP) + REL_CLIP
    g = rel_bias[:, idx]
    skew = jnp.tile(g, (1, Q_ROWS))[:, :Q_ROWS * (period - 1)].reshape(HEADS, Q_ROWS, period - 1)
    return skew[:, :, Q_ROWS - 1:Q_ROWS - 1 + span] + _pair_mask(B_PREV_CHUNKS)


def _alibi_table():
    halo = C_PREV_CHUNKS * CHUNK
    span = halo + Q_ROWS
    r = np.arange(Q_ROWS)[:, None]
    j = np.arange(span)[None, :]
    dist = np.abs(halo + r - j).astype(np.float32)
    per_head = [(-np.float32(s)) * dist + _pair_mask(C_PREV_CHUNKS) for s in ALIBI_SLOPES]
    group = HEADS // C_KV_HEADS
    return np.stack([np.concatenate(per_head[kv * group:(kv + 1) * group], axis=0)
                     for kv in range(C_KV_HEADS)], axis=0)


def _sgu_kernel(u_ref, v_ref, g_ref, w_ref, b_ref, o_ref):
    u = _gelu_tanh(u_ref[...].astype(F32))
    v = _gelu_tanh(v_ref[...].astype(F32))
    mu = jnp.mean(v, axis=-1, keepdims=True)
    var = jnp.mean(jnp.square(v - mu), axis=-1, keepdims=True)
    v = ((v - mu) * lax.rsqrt(var + EPS) * g_ref[...]).astype(BF16)
    row = lax.broadcasted_iota(jnp.int32, (SGU_BLOCK, SGU_BLOCK), 0)
    col = lax.broadcasted_iota(jnp.int32, (SGU_BLOCK, SGU_BLOCK), 1)
    for g in range(HEADS):
        lo, hi = g * HEAD_DIM, (g + 1) * HEAD_DIM
        w = jnp.where(row >= col, w_ref[g], 0.0).astype(BF16)
        bias = b_ref[:, g:g + 1]
        for n in range(ROWS_ATT // SGU_BLOCK):
            r0 = n * SGU_BLOCK
            mixed = jnp.dot(w, v[r0:r0 + SGU_BLOCK, lo:hi], preferred_element_type=F32) + bias
            o_ref[r0:r0 + SGU_BLOCK, lo:hi] = (u[r0:r0 + SGU_BLOCK, lo:hi] * mixed).astype(o_ref.dtype)


def _spatial_gating(p, norm_g, w_s, b_cols):
    t = p.shape[0]
    return pl.pallas_call(
        _sgu_kernel,
        grid=(t // ROWS_ATT,),
        in_specs=[pl.BlockSpec((ROWS_ATT, D_GROUP), lambda i: (i, COL_DU)),
                  pl.BlockSpec((ROWS_ATT, D_GROUP), lambda i: (i, COL_DV)),
                  pl.BlockSpec((1, D_GROUP), lambda i: (0, 0)),
                  pl.BlockSpec((HEADS, SGU_BLOCK, SGU_BLOCK), lambda i: (0, 0, 0)),
                  pl.BlockSpec((SGU_BLOCK, LANES), lambda i: (0, 0))],
        out_specs=pl.BlockSpec((ROWS_ATT, D_GROUP), lambda i: (i, 0)),
        out_shape=jax.ShapeDtypeStruct((t, D_GROUP), BF16),
        compiler_params=_cparams("parallel"),
        name="spatial_gating",
    )(p, p, norm_g.reshape(1, D_GROUP), w_s, b_cols)


def _outproj_kernel(oa_ref, ob_ref, oc_ref, od_ref, w_ref, x_ref, mod_ref, g_ref, o_ref, w_scr):
    @pl.when(pl.program_id(0) == 0)
    def _():
        w_scr[...] = w_ref[...].astype(BF16)

    half = ROWS_OUT // 2
    scale = mod_ref[0, 2:3, :] * g_ref[...]
    for r0 in (0, half):
        rows = slice(r0, r0 + half)
        y = jnp.dot(oa_ref[rows, :], w_scr[0:D_GROUP, :], preferred_element_type=F32)
        y += jnp.dot(ob_ref[rows, :], w_scr[D_GROUP:2 * D_GROUP, :], preferred_element_type=F32)
        y += jnp.dot(oc_ref[rows, :], w_scr[2 * D_GROUP:3 * D_GROUP, :], preferred_element_type=F32)
        y += jnp.dot(od_ref[rows, :], w_scr[3 * D_GROUP:, :], preferred_element_type=F32)

        def finish(sub, y=y, r0=r0):
            part = y[sub, :]
            part = part * lax.rsqrt(jnp.mean(part * part, axis=-1, keepdims=True) + EPS)
            o_ref[r0 + sub.start:r0 + sub.stop, :] = x_ref[r0 + sub.start:r0 + sub.stop, :] + part * scale

        _for_row_chunks(half, finish)


def _out_projection(outs, w_out, x2, mod, post_g, layer, seq):
    t, d = x2.shape
    tiles_per_batch = seq // ROWS_OUT
    mix = pl.BlockSpec((ROWS_OUT, D_GROUP), lambda m: (m, 0))
    return pl.pallas_call(
        _outproj_kernel,
        grid=(t // ROWS_OUT,),
        in_specs=[mix, mix, mix, mix,
                  pl.BlockSpec((None, HEADS * D_GROUP, d), lambda m: (layer, 0, 0),
                               pipeline_mode=pl.Buffered(1)),
                  pl.BlockSpec((ROWS_OUT, d), lambda m: (m, 0)),
                  pl.BlockSpec((1, 6, d), lambda m: (m // tiles_per_batch, 0, 0)),
                  pl.BlockSpec((1, d), lambda m: (0, 0))],
        out_specs=pl.BlockSpec((ROWS_OUT, d), lambda m: (m, 0)),
        out_shape=jax.ShapeDtypeStruct((t, d), F32),
        scratch_shapes=[pltpu.VMEM((HEADS * D_GROUP, d), BF16)],
        compiler_params=_cparams("arbitrary"),
        name="out_projection",
    )(*outs, w_out, x2, mod, post_g.reshape(1, d))


def _ffn_kernel(x_ref, xp_ref, mod_ref, pre_ref, post_ref, wa_ref, wg_ref, wd_ref, cw_ref, cb_ref,
                o_ref, h_scr, a_scr, *, tiles_per_batch):
    m = pl.program_id(0)
    f = pl.program_id(1)

    @pl.when(f == 0)
    def _():
        gain = pre_ref[...] * (1.0 + mod_ref[0, 4:5, :])
        shift = mod_ref[0, 3:4, :]
        h_scr[0:HALO, :] = _modulated_rms(xp_ref[...], gain, shift).astype(BF16)

        def fill(rows):
            h_scr[HALO + rows.start:HALO + rows.stop, :] = (
                _modulated_rms(x_ref[rows, :], gain, shift).astype(BF16))
            o_ref[rows, :] = jnp.zeros((ROW_CHUNK, o_ref.shape[1]), F32)

        _for_row_chunks(ROWS_OUT, fill)

    keep = (m % tiles_per_batch != 0).astype(F32)
    a = jnp.dot(h_scr[...], wa_ref[...], preferred_element_type=F32)
    a_scr[0:HALO, :] = a[0:HALO] * keep
    a_scr[HALO:, :] = a[HALO:]
    gate = jnp.dot(h_scr[HALO:, :], wg_ref[...], preferred_element_type=F32)
    acc = cb_ref[...]
    for tap in range(FFN_CONV):
        off = HALO - (FFN_CONV - 1) + tap
        acc = acc + cw_ref[tap:tap + 1, :] * a_scr[off:off + ROWS_OUT, :]
    act = (_gelu_tanh(acc) * gate).astype(BF16)
    o_ref[...] += jnp.dot(act, wd_ref[...], preferred_element_type=F32)

    @pl.when(f == pl.num_programs(1) - 1)
    def _():
        scale = mod_ref[0, 5:6, :] * post_ref[...]

        def finish(rows):
            y = o_ref[rows, :]
            y = y * lax.rsqrt(jnp.mean(y * y, axis=-1, keepdims=True) + EPS)
            o_ref[rows, :] = x_ref[rows, :] + y * scale

        _for_row_chunks(ROWS_OUT, finish)


def _conv_ffn(x2, mod, pre_g, post_g, w_a, w_g, w_d, conv_w, conv_b, layer, seq):
    t, d = x2.shape
    tiles_per_batch = seq // ROWS_OUT
    halo_per_tile = ROWS_OUT // HALO
    return pl.pallas_call(
        functools.partial(_ffn_kernel, tiles_per_batch=tiles_per_batch),
        grid=(t // ROWS_OUT, D_FF_PAD // D_GROUP),
        in_specs=[pl.BlockSpec((ROWS_OUT, d), lambda m, f: (m, 0)),
                  pl.BlockSpec((HALO, d), lambda m, f: (jnp.maximum(m * halo_per_tile - 1, 0), 0)),
                  pl.BlockSpec((1, 6, d), lambda m, f: (m // tiles_per_batch, 0, 0)),
                  pl.BlockSpec((1, d), lambda m, f: (0, 0)),
                  pl.BlockSpec((1, d), lambda m, f: (0, 0)),
                  pl.BlockSpec((d, D_GROUP), lambda m, f: (0, f)),
                  pl.BlockSpec((d, D_GROUP), lambda m, f: (0, f)),
                  pl.BlockSpec((D_GROUP, d), lambda m, f: (f, 0)),
                  pl.BlockSpec((None, FFN_CONV, D_GROUP), lambda m, f: (layer, 0, f)),
                  pl.BlockSpec((None, 1, D_GROUP), lambda m, f: (layer, 0, f))],
        out_specs=pl.BlockSpec((ROWS_OUT, d), lambda m, f: (m, 0)),
        out_shape=jax.ShapeDtypeStruct((t, d), F32),
        scratch_shapes=[pltpu.VMEM((HALO + ROWS_OUT, d), BF16),
                        pltpu.VMEM((HALO + ROWS_OUT, D_GROUP), F32)],
        compiler_params=_cparams("parallel", "arbitrary"),
        name="conv_ffn",
    )(x2, x2, mod, pre_g.reshape(1, d), post_g.reshape(1, d), w_a, w_g, w_d, conv_w, conv_b)


def _pad_last(a, width):
    return jnp.pad(a, [(0, 0)] * (a.ndim - 1) + [(0, width - a.shape[-1])])


def _layer(x2, mod, layer, batch, seq, weights, mix_pre_g, mix_post_g, dn_conv_w, dn_a_log, dn_dt_bias,
           dn_norm_g, rel_bias, sinks, sgu_norm_g, sgu_w, sgu_b, ffn_pre_g, ffn_post_g):
    t = x2.shape[0]
    p, small, small_t = _in_projection(x2, mod, mix_pre_g, weights["in_full"], weights["in_tail"],
                                       weights["in_small"], layer, seq)
    small_t3 = small_t.reshape(SUBLANES, t // CHUNK, CHUNK).transpose(1, 0, 2)

    prm = jnp.stack([dn_dt_bias, dn_a_log], axis=0)
    prm_row = jnp.zeros((SUBLANES, LANES), F32).at[0:2, HEADS:2 * HEADS].set(prm)
    prm_col = jnp.zeros((SUBLANES, LANES), F32).at[HEADS:2 * HEADS, 0:2].set(prm.T)
    out_a, w_a, w_g, w_d = _gated_deltanet(p, small, small_t3, dn_conv_w, prm_row, prm_col, dn_norm_g,
                                           weights["ffn_up"], weights["ffn_down"], layer, batch, seq)
    out_b = _relbias_attention(p, _relbias_table(rel_bias), batch, seq)
    out_c = _swa_attention(p, sinks, jnp.asarray(_alibi_table()), batch, seq)
    out_d = _spatial_gating(p, sgu_norm_g, sgu_w, _pad_last(sgu_b.T, LANES))

    x2 = _out_projection((out_a, out_b, out_c, out_d), weights["out"], x2, mod, mix_post_g, layer, seq)
    return _conv_ffn(x2, mod, ffn_pre_g, ffn_post_g, w_a, w_g, w_d,
                     weights["ffn_conv_w"], weights["ffn_conv_b"], layer, seq)


def _prepare_weights(w_in, w_out, ffn_w_up, ffn_conv_w, ffn_conv_b, ffn_w_down):
    small_lo = N_LO_TILES * D_GROUP
    w_in_t = jnp.swapaxes(w_in, 1, 2)
    w_small = w_in_t[:, small_lo:small_lo + N_SMALL, :]
    return {
        "in_full": w_in_t,
        "in_tail": w_in_t,
        "in_small": jnp.pad(w_small, ((0, 0), (0, LANES - N_SMALL), (0, 0))).astype(BF16),
        "out": w_out,
        "ffn_up": ffn_w_up,
        "ffn_down": ffn_w_down,
        "ffn_conv_w": _pad_last(ffn_conv_w, D_FF_PAD),
        "ffn_conv_b": _pad_last(ffn_conv_b[:, None, :], D_FF_PAD),
    }


def kernel(x, c, ada_w, ada_b, mix_pre_g, mix_post_g, w_in, dn_conv_w, dn_a_log, dn_dt_bias, dn_norm_g,
           rel_bias, sinks, sgu_norm_g, sgu_w, sgu_b, w_out, ffn_pre_g, ffn_post_g, ffn_w_up, ffn_conv_w,
           ffn_conv_b, ffn_w_down):
    batch, seq, d = x.shape
    depth = ada_w.shape[0]
    assert d == D_MODEL and seq % ROWS_PROJ == 0 and batch <= SUBLANES
    c_pad = jnp.pad(c, ((0, SUBLANES - batch), (0, 0)))
    mod_all = _modulation(c_pad, ada_w, ada_b)[:, :batch].reshape(depth, batch, 6, d)
    weights = _prepare_weights(w_in, w_out, ffn_w_up, ffn_conv_w, ffn_conv_b, ffn_w_down)
    x2 = x.reshape(batch * seq, d)
    for l in range(depth):
        x2 = _layer(x2, mod_all[l], l, batch, seq, weights, mix_pre_g[l], mix_post_g[l], dn_conv_w[l],
                    dn_a_log[l], dn_dt_bias[l], dn_norm_g[l], rel_bias[l], sinks[l], sgu_norm_g[l],
                    sgu_w[l], sgu_b[l], ffn_pre_g[l], ffn_post_g[l])
    return x2.reshape(batch, seq, d)
```

```python
import functools

import numpy as np
import jax
import jax.numpy as jnp
from jax import lax
from jax.experimental import pallas as pl
from jax.experimental.pallas import tpu as pltpu

F32 = jnp.float32
BF16 = jnp.bfloat16

D_MODEL = 2048
CHUNK = 64
HEAD_DIM = 128
HEADS = 4
D_GROUP = HEADS * HEAD_DIM
DN_CONV = 4
B_PREV_CHUNKS = 8
REL_CLIP = 256
C_PREV_CHUNKS = 2
C_KV_HEADS = 2
SGU_BLOCK = 128
D_FF = 5504
FFN_CONV = 3
EPS = 1e-6

LANES = 128
SUBLANES = 8
VMEM_LIMIT_BYTES = 56 * 1024 * 1024

N_SMALL = 2 * HEADS
N_MAIN = 11 * D_GROUP
N_LO_TILES = 4
COL_AQ, COL_AK, COL_AV, COL_AG, COL_BQ, COL_BK, COL_BV, COL_CQ, COL_CKV, COL_DU, COL_DV = range(11)
D_FF_PAD = 11 * D_GROUP

ROWS_ATT = 1024
Q_ROWS = 2 * CHUNK
DN_CHUNKS = 4
DN_ROWS = DN_CHUNKS * CHUNK
HALO = 16
ROW_CHUNK = 16
ROWS_PROJ = 2048
ROWS_OUT = 512
MOD_COLS = 1024

ALIBI_SLOPES = tuple(float(2.0 ** (-8.0 * h / HEADS)) for h in range(1, HEADS + 1))


def _cparams(*sem):
    return pltpu.CompilerParams(dimension_semantics=sem, vmem_limit_bytes=VMEM_LIMIT_BYTES)


def _sigmoid(x):
    return 1.0 / (1.0 + jnp.exp(-x))


def _silu(x):
    return x * _sigmoid(x)


def _softplus(x):
    return jnp.maximum(x, 0.0) + jnp.log1p(jnp.exp(-jnp.abs(x)))


def _gelu_tanh(x):
    return 0.5 * x * (1.0 + jnp.tanh(float(np.sqrt(2.0 / np.pi)) * (x + 0.044715 * (x * x * x))))


def _rms(x, g):
    return x * lax.rsqrt(jnp.mean(x * x, axis=-1, keepdims=True) + EPS) * g


def _modulated_rms(x, gain, shift):
    return x * lax.rsqrt(jnp.mean(x * x, axis=-1, keepdims=True) + EPS) * gain + shift


def _for_row_chunks(n_rows, body):
    for r0 in range(0, n_rows, ROW_CHUNK):
        body(slice(r0, r0 + ROW_CHUNK))


def _dot(a, b):
    return jnp.dot(a.astype(BF16), b.astype(BF16), preferred_element_type=F32)


def _dot_nt(a, b):
    return lax.dot_general(a.astype(BF16), b.astype(BF16), (((1,), (1,)), ((), ())),
                           preferred_element_type=F32)


def _dot_f32(a, b):
    return jnp.dot(a, b, precision=lax.Precision.HIGHEST, preferred_element_type=F32)


def _mod_kernel(c_ref, w_ref, b_ref, o_ref):
    cond = _silu(c_ref[...])
    o_ref[0] = _dot(cond, w_ref[0]) + b_ref[0]


def _modulation(c_pad, ada_w, ada_b):
    depth, d, n = ada_w.shape
    return pl.pallas_call(
        _mod_kernel,
        grid=(depth, n // MOD_COLS),
        in_specs=[pl.BlockSpec((SUBLANES, d), lambda l, j: (0, 0)),
                  pl.BlockSpec((1, d, MOD_COLS), lambda l, j: (l, 0, j)),
                  pl.BlockSpec((1, 1, MOD_COLS), lambda l, j: (l, 0, j))],
        out_specs=pl.BlockSpec((1, SUBLANES, MOD_COLS), lambda l, j: (l, 0, j)),
        out_shape=jax.ShapeDtypeStruct((depth, SUBLANES, n), F32),
        compiler_params=_cparams("parallel", "parallel"),
        name="modulation",
    )(c_pad, ada_w, ada_b.reshape(depth, 1, n))


def _inproj_kernel(x_hbm, mod_ref, g_ref, wlo_ref, whi_ref, ws_ref, p_ref, s_ref, st_ref, h_scr, x_ref, x_sem):
    m = pl.program_id(0)
    n = pl.program_id(1)

    def x_copy(tile):
        rows = pl.ds(pl.multiple_of(tile * ROWS_PROJ, ROWS_PROJ), ROWS_PROJ)
        return pltpu.make_async_copy(x_hbm.at[rows, :], x_ref, x_sem)

    @pl.when((m == 0) & (n == 0))
    def _():
        x_copy(0).start()

    @pl.when((n == 1) & (m + 1 < pl.num_programs(0)))
    def _():
        x_copy(m + 1).start()

    @pl.when(n == 0)
    def _():
        x_copy(m).wait()
        gain = g_ref[...] * (1.0 + mod_ref[0, 1:2, :])
        shift = mod_ref[0, 0:1, :]

        def fill(rows):
            h_scr[rows, :] = _modulated_rms(x_ref[rows, :], gain, shift).astype(BF16)

        _for_row_chunks(ROWS_PROJ, fill)
        small_t = _dot_nt(ws_ref[...], h_scr[...])
        s_ref[...] = small_t.T
        st_ref[...] = small_t[0:SUBLANES, :]

    @pl.when(n < N_LO_TILES)
    def _():
        p_ref[...] = _dot_nt(h_scr[...], wlo_ref[...]).astype(p_ref.dtype)

    @pl.when(n >= N_LO_TILES)
    def _():
        p_ref[...] = _dot_nt(h_scr[...], whi_ref[0]).astype(p_ref.dtype)


def _in_projection(x2, mod, pre_g, w_full, w_tail, w_small, layer, seq):
    t, d = x2.shape
    tiles_per_batch = seq // ROWS_PROJ
    last_tile = N_MAIN // D_GROUP - 1
    return pl.pallas_call(
        _inproj_kernel,
        grid=(t // ROWS_PROJ, N_MAIN // D_GROUP),
        in_specs=[pl.BlockSpec(memory_space=pl.ANY),
                  pl.BlockSpec((1, 6, d), lambda m, n: (m // tiles_per_batch, 0, 0)),
                  pl.BlockSpec((1, d), lambda m, n: (0, 0)),
                  pl.BlockSpec((None, D_GROUP, d), lambda m, n: (layer, jnp.where(n < N_LO_TILES, n, 0), 0)),
                  pl.BlockSpec((pl.Element(1), pl.Element(D_GROUP), pl.Element(d)),
                               lambda m, n: (layer, pl.multiple_of(
                                   N_SMALL + D_GROUP * jnp.where(n < N_LO_TILES, last_tile, n), SUBLANES), 0)),
                  pl.BlockSpec((None, LANES, d), lambda m, n: (layer, 0, 0))],
        out_specs=[pl.BlockSpec((ROWS_PROJ, D_GROUP), lambda m, n: (m, n)),
                   pl.BlockSpec((ROWS_PROJ, LANES), lambda m, n: (m, 0)),
                   pl.BlockSpec((SUBLANES, ROWS_PROJ), lambda m, n: (0, m))],
        out_shape=[jax.ShapeDtypeStruct((t, N_MAIN), BF16),
                   jax.ShapeDtypeStruct((t, LANES), F32),
                   jax.ShapeDtypeStruct((SUBLANES, t), F32)],
        scratch_shapes=[pltpu.VMEM((ROWS_PROJ, d), BF16),
                        pltpu.VMEM((ROWS_PROJ, d), F32),
                        pltpu.SemaphoreType.DMA(())],
        compiler_params=_cparams("arbitrary", "arbitrary"),
        name="in_projection",
    )(x2, mod, pre_g.reshape(1, d), w_full, w_tail, w_small)


def _unit_lower_inverse(mats, row, col):
    blk_r, blk_c = jnp.right_shift(row, 4), jnp.right_shift(col, 4)
    half_r, half_c = jnp.right_shift(row, 5), jnp.right_shift(col, 5)
    eye = (row == col).astype(F32)
    a_diag = [jnp.where(blk_r == blk_c, a, 0.0) for a in mats]
    a_pair = [jnp.where(half_r == half_c, a, 0.0) - d for a, d in zip(mats, a_diag)]
    a_far = [jnp.where(half_r != half_c, a, 0.0) for a in mats]
    p2 = [_dot(d, d) for d in a_diag]
    t = [eye - d for d in a_diag]
    t = [x + _dot(x, p) for x, p in zip(t, p2)]
    p4 = [_dot(p, p) for p in p2]
    t = [x + _dot(x, p) for x, p in zip(t, p4)]
    p8 = [_dot(p, p) for p in p4]
    t = [x + _dot(x, p) for x, p in zip(t, p8)]
    for off in (a_pair, a_far):
        left = [_dot(x, a) for x, a in zip(t, off)]
        t = [x - _dot(l, x) for x, l in zip(t, left)]
    return t


def _ffn_layout_step(tile, up_refs, down_refs, wa_ref, wg_ref, wd_ref):
    n_tiles = D_FF_PAD // D_GROUP

    @pl.when(tile < n_tiles)
    def _():
        col = tile * D_GROUP + lax.broadcasted_iota(jnp.int32, (1, D_GROUP), 1)
        valid = col < D_FF
        gate = jnp.concatenate([r[...] for r in up_refs[1:]], axis=1)
        wa_ref[...] = jnp.where(valid, up_refs[0][...], 0.0).astype(BF16)
        wg_ref[...] = jnp.where(valid, gate, 0.0).astype(BF16)

    @pl.when((tile >= n_tiles) & (tile < 2 * n_tiles))
    def _():
        row = (tile - n_tiles) * D_GROUP + lax.broadcasted_iota(jnp.int32, (D_GROUP, 1), 0)
        rows = jnp.concatenate([r[...] for r in down_refs], axis=0)
        wd_ref[...] = jnp.where(row < D_FF, rows, 0.0).astype(BF16)


def _delta_kernel(q_ref, k_ref, v_ref, qp_ref, kp_ref, vp_ref, gate_ref, sm_ref, smt_ref, cw_ref,
                  prow_ref, pcol_ref, ng_ref, ua_ref, ug0_ref, ug1_ref, ug2_ref, ug3_ref,
                  dr0_ref, dr1_ref, dr2_ref, dr3_ref, o_ref, wa_ref, wg_ref, wd_ref, state_scr):
    step = pl.program_id(1)
    _ffn_layout_step(pl.program_id(0) * pl.num_programs(1) + step,
                     (ua_ref, ug0_ref, ug1_ref, ug2_ref, ug3_ref),
                     (dr0_ref, dr1_ref, dr2_ref, dr3_ref), wa_ref, wg_ref, wd_ref)

    @pl.when(step == 0)
    def _():
        state_scr[...] = jnp.zeros_like(state_scr)

    keep = (step > 0).astype(F32)
    halo = jnp.concatenate([qp_ref[...], kp_ref[...], vp_ref[...]], axis=1).astype(F32) * keep
    cur = jnp.concatenate([q_ref[...], k_ref[...], v_ref[...]], axis=1).astype(F32)
    xe = jnp.concatenate([halo, cur], axis=0)
    acc = cw_ref[DN_CONV - 1:DN_CONV, :] * cur
    for back in range(1, DN_CONV):
        shifted = pltpu.roll(xe, back, 0)[HALO:, :]
        acc = acc + cw_ref[DN_CONV - 1 - back:DN_CONV - back, :] * shifted
    qkv = _silu(acc)

    row = lax.broadcasted_iota(jnp.int32, (CHUNK, CHUNK), 0)
    col = lax.broadcasted_iota(jnp.int32, (CHUNK, CHUNK), 1)
    causal = row >= col
    strict = row > col
    lower_ones = causal.astype(F32)
    upper_ones = (row <= col).astype(F32)

    sm = sm_ref[...]
    la_cols = -jnp.exp(prow_ref[1:2, :]) * _softplus(sm + prow_ref[0:1, :])
    beta_cols = _sigmoid(sm)
    neg_a_col = -jnp.exp(pcol_ref[:, 1:2])
    dt_col = pcol_ref[:, 0:1]

    scale = HEAD_DIM ** -0.5
    qs, ks, vbs, kbs = [], [], [], []
    for h in range(HEADS):
        lo, hi = h * HEAD_DIM, (h + 1) * HEAD_DIM
        q = qkv[:, lo:hi]
        k = qkv[:, D_GROUP + lo:D_GROUP + hi]
        v = qkv[:, 2 * D_GROUP + lo:2 * D_GROUP + hi]
        beta = beta_cols[:, h:h + 1]
        k = k * lax.rsqrt(jnp.sum(k * k, axis=-1, keepdims=True) + EPS)
        qs.append(q * lax.rsqrt(jnp.sum(q * q, axis=-1, keepdims=True) + EPS) * scale)
        ks.append(k)
        kbs.append(k * beta)
        vbs.append(v * beta)

    probs = [(c, h) for c in range(DN_CHUNKS) for h in range(HEADS)]
    g_cols, g_rows = [], []
    for c in range(DN_CHUNKS):
        rows = slice(c * CHUNK, (c + 1) * CHUNK)
        g_cols.append(_dot_f32(lower_ones, la_cols[rows]))
        g_rows.append(_dot_f32(neg_a_col * _softplus(smt_ref[c] + dt_col), upper_ones))

    def rows_of(c):
        return slice(c * CHUNK, (c + 1) * CHUNK)

    g_col = [g_cols[c][:, HEADS + h:HEADS + h + 1] for c, h in probs]
    g_last = [g[CHUNK - 1:CHUNK, :] for g in g_col]
    decay = [jnp.exp(jnp.where(causal, g - g_rows[c][HEADS + h:HEADS + h + 1, :], -jnp.inf))
             for g, (c, h) in zip(g_col, probs)]
    kk = [_dot_nt(kbs[h][rows_of(c)], ks[h][rows_of(c)]) for c, h in probs]
    qk = [_dot_nt(qs[h][rows_of(c)], ks[h][rows_of(c)]) for c, h in probs]
    t_inv = _unit_lower_inverse([jnp.where(strict, x * d, 0.0) for x, d in zip(kk, decay)], row, col)
    exp_g = [jnp.exp(g) for g in g_col]
    uw = [_dot(t, jnp.concatenate([vbs[h][rows_of(c)], kbs[h][rows_of(c)] * e], axis=1))
          for t, e, (c, h) in zip(t_inv, exp_g, probs)]
    attn = [(x * d).astype(BF16) for x, d in zip(qk, decay)]
    wq = [jnp.concatenate([x[:, HEAD_DIM:], qs[h][rows_of(c)] * e], axis=0).astype(BF16)
          for x, e, (c, h) in zip(uw, exp_g, probs)]
    k_dec_t = [(ks[h][rows_of(c)] * jnp.exp(gl - g)).T.astype(BF16)
               for g, gl, (c, h) in zip(g_col, g_last, probs)]
    decay_last = [jnp.exp(gl) for gl in g_last]

    state = [state_scr[h] for h in range(HEADS)]
    for c in range(DN_CHUNKS):
        idx = [c * HEADS + h for h in range(HEADS)]
        ws_qs = [jnp.dot(wq[i], state[h].astype(BF16), preferred_element_type=F32)
                 for h, i in enumerate(idx)]
        v_new = [(uw[i][:, :HEAD_DIM] - x[:CHUNK]).astype(BF16) for x, i in zip(ws_qs, idx)]
        state = [state[h] * decay_last[i] + jnp.dot(k_dec_t[i], v_new[h], preferred_element_type=F32)
                 for h, i in enumerate(idx)]
        for h, i in enumerate(idx):
            lo, hi = h * HEAD_DIM, (h + 1) * HEAD_DIM
            o = ws_qs[h][CHUNK:] + jnp.dot(attn[i], v_new[h], preferred_element_type=F32)
            o = _rms(o, ng_ref[...]) * _silu(gate_ref[rows_of(c), lo:hi].astype(F32))
            o_ref[rows_of(c), lo:hi] = o.astype(o_ref.dtype)
    for h in range(HEADS):
        state_scr[h] = state[h]


def _gated_deltanet(p, small, small_t3, conv_w, prm_row, prm_col, norm_g, w_up, w_down, layer, batch, seq):
    t = p.shape[0]
    d = w_up.shape[1]
    steps = seq // DN_ROWS
    halo_per_step = DN_ROWS // HALO
    per_tile = D_GROUP // LANES
    n_in = D_FF // LANES
    n_tiles = D_FF_PAD // D_GROUP
    assert batch * steps >= 2 * n_tiles

    def cur(j):
        return pl.BlockSpec((DN_ROWS, D_GROUP), lambda b, s: (b * steps + s, j))

    def prev(j):
        return pl.BlockSpec((HALO, D_GROUP),
                            lambda b, s: (jnp.maximum((b * steps + s) * halo_per_step - 1, 0), j))

    def up_tile(b, s):
        return jnp.minimum(b * steps + s, n_tiles - 1)

    def down_tile(b, s):
        return jnp.clip(b * steps + s - n_tiles, 0, n_tiles - 1)

    def gate_spec(k):
        return pl.BlockSpec((None, d, LANES), lambda b, s: (
            layer, 0, jnp.minimum(n_in + per_tile * up_tile(b, s) + k, 2 * n_in - 1)))

    def row_spec(k):
        return pl.BlockSpec((None, LANES, d), lambda b, s: (
            layer, jnp.minimum(per_tile * down_tile(b, s) + k, n_in - 1), 0))

    up_cols = pl.BlockSpec((d, D_GROUP), lambda b, s: (0, up_tile(b, s)))
    return pl.pallas_call(
        _delta_kernel,
        grid=(batch, steps),
        in_specs=[cur(COL_AQ), cur(COL_AK), cur(COL_AV), prev(COL_AQ), prev(COL_AK), prev(COL_AV),
                  cur(COL_AG),
                  pl.BlockSpec((DN_ROWS, LANES), lambda b, s: (b * steps + s, 0)),
                  pl.BlockSpec((DN_CHUNKS, SUBLANES, CHUNK), lambda b, s: (b * steps + s, 0, 0)),
                  pl.BlockSpec((DN_CONV, 3 * D_GROUP), lambda b, s: (0, 0)),
                  pl.BlockSpec((SUBLANES, LANES), lambda b, s: (0, 0)),
                  pl.BlockSpec((SUBLANES, LANES), lambda b, s: (0, 0)),
                  pl.BlockSpec((1, HEAD_DIM), lambda b, s: (0, 0)),
                  pl.BlockSpec((None, d, D_GROUP), lambda b, s: (layer, 0, up_tile(b, s)))]
                 + [gate_spec(k) for k in range(per_tile)] + [row_spec(k) for k in range(per_tile)],
        out_specs=[pl.BlockSpec((DN_ROWS, D_GROUP), lambda b, s: (b * steps + s, 0)),
                   up_cols, up_cols,
                   pl.BlockSpec((D_GROUP, d), lambda b, s: (down_tile(b, s), 0))],
        out_shape=[jax.ShapeDtypeStruct((t, D_GROUP), BF16),
                   jax.ShapeDtypeStruct((d, D_FF_PAD), BF16),
                   jax.ShapeDtypeStruct((d, D_FF_PAD), BF16),
                   jax.ShapeDtypeStruct((D_FF_PAD, d), BF16)],
        scratch_shapes=[pltpu.VMEM((HEADS, HEAD_DIM, HEAD_DIM), F32)],
        compiler_params=_cparams("arbitrary", "arbitrary"),
        name="gated_deltanet",
    )(p, p, p, p, p, p, p, small, small_t3, conv_w, prm_row, prm_col, norm_g.reshape(1, HEAD_DIM),
      w_up, w_up, w_up, w_up, w_up, w_down, w_down, w_down, w_down)


def _relbias_kernel(q_ref, kc_ref, kp_ref, vc_ref, vp_ref, bias_ref, o_ref, k_scr, v_scr):
    i = pl.program_id(1)
    halo = B_PREV_CHUNKS * CHUNK
    k_scr[0:halo, :] = kp_ref[...]
    k_scr[halo:, :] = kc_ref[...]
    v_scr[0:halo, :] = vp_ref[...]
    v_scr[halo:, :] = vc_ref[...]
    span = halo + Q_ROWS
    key_pos = lax.broadcasted_iota(jnp.int32, (Q_ROWS, span), 1)
    first_valid = jnp.where(i > 0, 0, halo)
    scale = HEAD_DIM ** -0.5
    subs = [sub * Q_ROWS for sub in range(ROWS_ATT // Q_ROWS)]
    for h in range(HEADS):
        lo, hi = h * HEAD_DIM, (h + 1) * HEAD_DIM
        bias = bias_ref[h]
        scores = [_dot_nt(q_ref[r0:r0 + Q_ROWS, lo:hi], k_scr[r0:r0 + span, lo:hi]) for r0 in subs]
        probs, invs = [], []
        for r0, s in zip(subs, scores):
            s = jnp.where(key_pos + r0 >= first_valid, s * scale + bias, -jnp.inf)
            e = jnp.exp(s - jnp.max(s, axis=-1, keepdims=True))
            probs.append(e.astype(BF16))
            invs.append(1.0 / jnp.sum(e, axis=-1, keepdims=True))
        outs = [jnp.dot(e, v_scr[r0:r0 + span, lo:hi], preferred_element_type=F32)
                for r0, e in zip(subs, probs)]
        for r0, o, inv in zip(subs, outs, invs):
            o_ref[r0:r0 + Q_ROWS, lo:hi] = (o * inv).astype(o_ref.dtype)


def _relbias_attention(p, bias, batch, seq):
    t = p.shape[0]
    nb = seq // ROWS_ATT

    def cur(j):
        return pl.BlockSpec((ROWS_ATT, D_GROUP), lambda b, i: (b * nb + i, j))

    halo = B_PREV_CHUNKS * CHUNK
    halo_per_tile = ROWS_ATT // halo

    def prev(j):
        return pl.BlockSpec((halo, D_GROUP),
                            lambda b, i: (jnp.maximum((b * nb + i) * halo_per_tile - 1, 0), j))

    return pl.pallas_call(
        _relbias_kernel,
        grid=(batch, nb),
        in_specs=[cur(COL_BQ), cur(COL_BK), prev(COL_BK), cur(COL_BV), prev(COL_BV),
                  pl.BlockSpec(bias.shape, lambda b, i: (0, 0, 0))],
        out_specs=pl.BlockSpec((ROWS_ATT, D_GROUP), lambda b, i: (b * nb + i, 0)),
        out_shape=jax.ShapeDtypeStruct((t, D_GROUP), BF16),
        scratch_shapes=[pltpu.VMEM((halo + ROWS_ATT, D_GROUP), BF16),
                        pltpu.VMEM((halo + ROWS_ATT, D_GROUP), BF16)],
        compiler_params=_cparams("parallel", "arbitrary"),
        name="relbias_attention",
    )(p, p, p, p, p, bias)


def _swa_kernel(sink_ref, q_ref, kvc_ref, kvp_ref, bias_ref, o_ref, kv_scr):
    i = pl.program_id(1)
    halo = C_PREV_CHUNKS * CHUNK
    kv_scr[0:halo, :] = kvp_ref[...]
    kv_scr[halo:, :] = kvc_ref[...]
    span = halo + Q_ROWS
    group = HEADS // C_KV_HEADS
    stacked = group * Q_ROWS
    key_pos = lax.broadcasted_iota(jnp.int32, (stacked, span), 1)
    q_row = lax.broadcasted_iota(jnp.int32, (stacked, 1), 0)
    first_valid = jnp.where(i > 0, 0, halo)
    scale = HEAD_DIM ** -0.5
    kv_width = C_KV_HEADS * HEAD_DIM
    for kvh in range(C_KV_HEADS):
        klo = kvh * HEAD_DIM
        h0 = kvh * group
        sink = jnp.where(q_row < Q_ROWS, sink_ref[h0], sink_ref[h0 + 1])
        bias = bias_ref[kvh]
        subs = [sub * Q_ROWS for sub in range(ROWS_ATT // Q_ROWS)]
        scores = []
        for r0 in subs:
            q = jnp.concatenate([q_ref[r0:r0 + Q_ROWS, (h0 + g) * HEAD_DIM:(h0 + g + 1) * HEAD_DIM]
                                 for g in range(group)], axis=0)
            scores.append(_dot_nt(q, kv_scr[r0:r0 + span, klo:klo + HEAD_DIM]))
        probs, invs = [], []
        for r0, s in zip(subs, scores):
            s = jnp.where(key_pos + r0 >= first_valid, s * scale + bias, -jnp.inf)
            m = jnp.maximum(jnp.max(s, axis=-1, keepdims=True), sink)
            e = jnp.exp(s - m)
            probs.append(e.astype(BF16))
            invs.append(1.0 / (jnp.sum(e, axis=-1, keepdims=True) + jnp.exp(sink - m)))
        vlo = kv_width + klo
        outs = [jnp.dot(e, kv_scr[r0:r0 + span, vlo:vlo + HEAD_DIM], preferred_element_type=F32)
                for r0, e in zip(subs, probs)]
        for r0, o, inv in zip(subs, outs, invs):
            o = o * inv
            for g in range(group):
                o_ref[r0:r0 + Q_ROWS, (h0 + g) * HEAD_DIM:(h0 + g + 1) * HEAD_DIM] = (
                    o[g * Q_ROWS:(g + 1) * Q_ROWS].astype(o_ref.dtype))


def _swa_attention(p, sinks, bias, batch, seq):
    t = p.shape[0]
    nb = seq // ROWS_ATT
    halo = C_PREV_CHUNKS * CHUNK
    halo_per_tile = ROWS_ATT // halo
    return pl.pallas_call(
        _swa_kernel,
        grid=(batch, nb),
        in_specs=[pl.BlockSpec(memory_space=pltpu.SMEM),
                  pl.BlockSpec((ROWS_ATT, D_GROUP), lambda b, i: (b * nb + i, COL_CQ)),
                  pl.BlockSpec((ROWS_ATT, D_GROUP), lambda b, i: (b * nb + i, COL_CKV)),
                  pl.BlockSpec((halo, D_GROUP),
                               lambda b, i: (jnp.maximum((b * nb + i) * halo_per_tile - 1, 0), COL_CKV)),
                  pl.BlockSpec(bias.shape, lambda b, i: (0, 0, 0))],
        out_specs=pl.BlockSpec((ROWS_ATT, D_GROUP), lambda b, i: (b * nb + i, 0)),
        out_shape=jax.ShapeDtypeStruct((t, D_GROUP), BF16),
        scratch_shapes=[pltpu.VMEM((halo + ROWS_ATT, D_GROUP), BF16)],
        compiler_params=_cparams("parallel", "arbitrary"),
        name="swa_attention",
    )(sinks, p, p, p, bias)


def _pair_mask(n_prev):
    span = n_prev * CHUNK + Q_ROWS
    r = np.arange(Q_ROWS)[:, None]
    j = np.arange(span)[None, :]
    band_pos = j - (r // CHUNK) * CHUNK
    return np.where((band_pos >= 0) & (band_pos < (n_prev + 1) * CHUNK), 0.0, -np.inf).astype(np.float32)


def _relbias_table(rel_bias):
    span = B_PREV_CHUNKS * CHUNK + Q_ROWS
    period = Q_ROWS + span
    k = np.arange(period) - (Q_ROWS - 1)
    idx = np.clip(B_PREV_CHUNKS * CHUNK - k, -REL_CLIP, REL_CLIP) + REL_CLIP
    g = rel_bias[:, idx]
    skew = jnp.tile(g, (1, Q_ROWS))[:, :Q_ROWS * (period - 1)].reshape(HEADS, Q_ROWS, period - 1)
    return skew[:, :, Q_ROWS - 1:Q_ROWS - 1 + span] + _pair_mask(B_PREV_CHUNKS)


def _alibi_table():
    halo = C_PREV_CHUNKS * CHUNK
    span = halo + Q_ROWS
    r = np.arange(Q_ROWS)[:, None]
    j = np.arange(span)[None, :]
    dist = np.abs(halo + r - j).astype(np.float32)
    per_head = [(-np.float32(s)) * dist + _pair_mask(C_PREV_CHUNKS) for s in ALIBI_SLOPES]
    group = HEADS // C_KV_HEADS
    return np.stack([np.concatenate(per_head[kv * group:(kv + 1) * group], axis=0)
                     for kv in range(C_KV_HEADS)], axis=0)


def _sgu_kernel(u_ref, v_ref, g_ref, w_ref, b_ref, o_ref):
    u = _gelu_tanh(u_ref[...].astype(F32))
    v = _gelu_tanh(v_ref[...].astype(F32))
    mu = jnp.mean(v, axis=-1, keepdims=True)
    var = jnp.mean(jnp.square(v - mu), axis=-1, keepdims=True)
    v = ((v - mu) * lax.rsqrt(var + EPS) * g_ref[...]).astype(BF16)
    row = lax.broadcasted_iota(jnp.int32, (SGU_BLOCK, SGU_BLOCK), 0)
    col = lax.broadcasted_iota(jnp.int32, (SGU_BLOCK, SGU_BLOCK), 1)
    for g in range(HEADS):
        lo, hi = g * HEAD_DIM, (g + 1) * HEAD_DIM
        w = jnp.where(row >= col, w_ref[g], 0.0).astype(BF16)
        bias = b_ref[:, g:g + 1]
        for n in range(ROWS_ATT // SGU_BLOCK):
            r0 = n * SGU_BLOCK
            mixed = jnp.dot(w, v[r0:r0 + SGU_BLOCK, lo:hi], preferred_element_type=F32) + bias
            o_ref[r0:r0 + SGU_BLOCK, lo:hi] = (u[r0:r0 + SGU_BLOCK, lo:hi] * mixed).astype(o_ref.dtype)


def _spatial_gating(p, norm_g, w_s, b_cols):
    t = p.shape[0]
    return pl.pallas_call(
        _sgu_kernel,
        grid=(t // ROWS_ATT,),
        in_specs=[pl.BlockSpec((ROWS_ATT, D_GROUP), lambda i: (i, COL_DU)),
                  pl.BlockSpec((ROWS_ATT, D_GROUP), lambda i: (i, COL_DV)),
                  pl.BlockSpec((1, D_GROUP), lambda i: (0, 0)),
                  pl.BlockSpec((HEADS, SGU_BLOCK, SGU_BLOCK), lambda i: (0, 0, 0)),
                  pl.BlockSpec((SGU_BLOCK, LANES), lambda i: (0, 0))],
        out_specs=pl.BlockSpec((ROWS_ATT, D_GROUP), lambda i: (i, 0)),
        out_shape=jax.ShapeDtypeStruct((t, D_GROUP), BF16),
        compiler_params=_cparams("parallel"),
        name="spatial_gating",
    )(p, p, norm_g.reshape(1, D_GROUP), w_s, b_cols)


def _outproj_kernel(oa_ref, ob_ref, oc_ref, od_ref, w_ref, x_ref, mod_ref, g_ref, o_ref, w_scr):
    @pl.when(pl.program_id(0) == 0)
    def _():
        w_scr[...] = w_ref[...].astype(BF16)

    half = ROWS_OUT // 2
    scale = mod_ref[0, 2:3, :] * g_ref[...]
    for r0 in (0, half):
        rows = slice(r0, r0 + half)
        y = jnp.dot(oa_ref[rows, :], w_scr[0:D_GROUP, :], preferred_element_type=F32)
        y += jnp.dot(ob_ref[rows, :], w_scr[D_GROUP:2 * D_GROUP, :], preferred_element_type=F32)
        y += jnp.dot(oc_ref[rows, :], w_scr[2 * D_GROUP:3 * D_GROUP, :], preferred_element_type=F32)
        y += jnp.dot(od_ref[rows, :], w_scr[3 * D_GROUP:, :], preferred_element_type=F32)

        def finish(sub, y=y, r0=r0):
            part = y[sub, :]
            part = part * lax.rsqrt(jnp.mean(part * part, axis=-1, keepdims=True) + EPS)
            o_ref[r0 + sub.start:r0 + sub.stop, :] = x_ref[r0 + sub.start:r0 + sub.stop, :] + part * scale

        _for_row_chunks(half, finish)


def _out_projection(outs, w_out, x2, mod, post_g, layer, seq):
    t, d = x2.shape
    tiles_per_batch = seq // ROWS_OUT
    mix = pl.BlockSpec((ROWS_OUT, D_GROUP), lambda m: (m, 0))
    return pl.pallas_call(
        _outproj_kernel,
        grid=(t // ROWS_OUT,),
        in_specs=[mix, mix, mix, mix,
                  pl.BlockSpec((None, HEADS * D_GROUP, d), lambda m: (layer, 0, 0),
                               pipeline_mode=pl.Buffered(1)),
                  pl.BlockSpec((ROWS_OUT, d), lambda m: (m, 0)),
                  pl.BlockSpec((1, 6, d), lambda m: (m // tiles_per_batch, 0, 0)),
                  pl.BlockSpec((1, d), lambda m: (0, 0))],
        out_specs=pl.BlockSpec((ROWS_OUT, d), lambda m: (m, 0)),
        out_shape=jax.ShapeDtypeStruct((t, d), F32),
        scratch_shapes=[pltpu.VMEM((HEADS * D_GROUP, d), BF16)],
        compiler_params=_cparams("arbitrary"),
        name="out_projection",
    )(*outs, w_out, x2, mod, post_g.reshape(1, d))


def _ffn_kernel(x_ref, xp_ref, mod_ref, pre_ref, post_ref, wa_ref, wg_ref, wd_ref, cw_ref, cb_ref,
                o_ref, h_scr, a_scr, *, tiles_per_batch):
    m = pl.program_id(0)
    f = pl.program_id(1)

    def prenorm():
        gain = pre_ref[...] * (1.0 + mod_ref[0, 4:5, :])
        shift = mod_ref[0, 3:4, :]
        h_scr[0:HALO, :] = _modulated_rms(xp_ref[...], gain, shift).astype(BF16)

        def fill(rows):
            h_scr[HALO + rows.start:HALO + rows.stop, :] = (
                _modulated_rms(x_ref[rows, :], gain, shift).astype(BF16))

        _for_row_chunks(ROWS_OUT, fill)

    def channel_tile():
        keep = (m % tiles_per_batch != 0).astype(F32)
        a = jnp.dot(h_scr[...], wa_ref[...], preferred_element_type=F32)
        a_scr[0:HALO, :] = a[0:HALO] * keep
        a_scr[HALO:, :] = a[HALO:]
        gate = jnp.dot(h_scr[HALO:, :], wg_ref[...], preferred_element_type=F32)
        acc = cb_ref[...]
        for tap in range(FFN_CONV):
            off = HALO - (FFN_CONV - 1) + tap
            acc = acc + cw_ref[tap:tap + 1, :] * a_scr[off:off + ROWS_OUT, :]
        act = (_gelu_tanh(acc) * gate).astype(BF16)
        return jnp.dot(act, wd_ref[...], preferred_element_type=F32)

    @pl.when(f == 0)
    def _():
        prenorm()
        o_ref[...] = channel_tile()

    @pl.when(f > 0)
    def _():
        o_ref[...] += channel_tile()

    @pl.when(f == pl.num_programs(1) - 1)
    def _():
        scale = mod_ref[0, 5:6, :] * post_ref[...]

        def finish(rows):
            y = o_ref[rows, :]
            y = y * lax.rsqrt(jnp.mean(y * y, axis=-1, keepdims=True) + EPS)
            o_ref[rows, :] = x_ref[rows, :] + y * scale

        _for_row_chunks(ROWS_OUT, finish)


def _conv_ffn(x2, mod, pre_g, post_g, w_a, w_g, w_d, conv_w, conv_b, layer, seq):
    t, d = x2.shape
    tiles_per_batch = seq // ROWS_OUT
    halo_per_tile = ROWS_OUT // HALO
    return pl.pallas_call(
        functools.partial(_ffn_kernel, tiles_per_batch=tiles_per_batch),
        grid=(t // ROWS_OUT, D_FF_PAD // D_GROUP),
        in_specs=[pl.BlockSpec((ROWS_OUT, d), lambda m, f: (m, 0)),
                  pl.BlockSpec((HALO, d), lambda m, f: (jnp.maximum(m * halo_per_tile - 1, 0), 0)),
                  pl.BlockSpec((1, 6, d), lambda m, f: (m // tiles_per_batch, 0, 0)),
                  pl.BlockSpec((1, d), lambda m, f: (0, 0)),
                  pl.BlockSpec((1, d), lambda m, f: (0, 0)),
                  pl.BlockSpec((d, D_GROUP), lambda m, f: (0, f)),
                  pl.BlockSpec((d, D_GROUP), lambda m, f: (0, f)),
                  pl.BlockSpec((D_GROUP, d), lambda m, f: (f, 0)),
                  pl.BlockSpec((None, FFN_CONV, D_GROUP), lambda m, f: (layer, 0, f)),
                  pl.BlockSpec((None, 1, D_GROUP), lambda m, f: (layer, 0, f))],
        out_specs=pl.BlockSpec((ROWS_OUT, d), lambda m, f: (m, 0)),
        out_shape=jax.ShapeDtypeStruct((t, d), F32),
        scratch_shapes=[pltpu.VMEM((HALO + ROWS_OUT, d), BF16),
                        pltpu.VMEM((HALO + ROWS_OUT, D_GROUP), F32)],
        compiler_params=_cparams("parallel", "arbitrary"),
        name="conv_ffn",
    )(x2, x2, mod, pre_g.reshape(1, d), post_g.reshape(1, d), w_a, w_g, w_d, conv_w, conv_b)


def _pad_last(a, width):
    return jnp.pad(a, [(0, 0)] * (a.ndim - 1) + [(0, width - a.shape[-1])])


def _layer(x2, mod, layer, batch, seq, weights, mix_pre_g, mix_post_g, dn_conv_w, dn_a_log, dn_dt_bias,
           dn_norm_g, rel_bias, sinks, sgu_norm_g, sgu_w, sgu_b, ffn_pre_g, ffn_post_g):
    t = x2.shape[0]
    p, small, small_t = _in_projection(x2, mod, mix_pre_g, weights["in_full"], weights["in_tail"],
                                       weights["in_small"], layer, seq)
    small_t3 = small_t.reshape(SUBLANES, t // CHUNK, CHUNK).transpose(1, 0, 2)

    prm = jnp.stack([dn_dt_bias, dn_a_log], axis=0)
    prm_row = jnp.zeros((SUBLANES, LANES), F32).at[0:2, HEADS:2 * HEADS].set(prm)
    prm_col = jnp.zeros((SUBLANES, LANES), F32).at[HEADS:2 * HEADS, 0:2].set(prm.T)
    out_a, w_a, w_g, w_d = _gated_deltanet(p, small, small_t3, dn_conv_w, prm_row, prm_col, dn_norm_g,
                                           weights["ffn_up"], weights["ffn_down"], layer, batch, seq)
    out_b = _relbias_attention(p, _relbias_table(rel_bias), batch, seq)
    out_c = _swa_attention(p, sinks, jnp.asarray(_alibi_table()), batch, seq)
    out_d = _spatial_gating(p, sgu_norm_g, sgu_w, _pad_last(sgu_b.T, LANES))

    x2 = _out_projection((out_a, out_b, out_c, out_d), weights["out"], x2, mod, mix_post_g, layer, seq)
    return _conv_ffn(x2, mod, ffn_pre_g, ffn_post_g, w_a, w_g, w_d,
                     weights["ffn_conv_w"], weights["ffn_conv_b"], layer, seq)


def _prepare_weights(w_in, w_out, ffn_w_up, ffn_conv_w, ffn_conv_b, ffn_w_down):
    small_lo = N_LO_TILES * D_GROUP
    w_in_t = jnp.swapaxes(w_in, 1, 2)
    w_small = w_in_t[:, small_lo:small_lo + N_SMALL, :]
    return {
        "in_full": w_in_t,
        "in_tail": w_in_t,
        "in_small": jnp.pad(w_small, ((0, 0), (0, LANES - N_SMALL), (0, 0))).astype(BF16),
        "out": w_out,
        "ffn_up": ffn_w_up,
        "ffn_down": ffn_w_down,
        "ffn_conv_w": _pad_last(ffn_conv_w, D_FF_PAD),
        "ffn_conv_b": _pad_last(ffn_conv_b[:, None, :], D_FF_PAD),
    }


def kernel(x, c, ada_w, ada_b, mix_pre_g, mix_post_g, w_in, dn_conv_w, dn_a_log, dn_dt_bias, dn_norm_g,
           rel_bias, sinks, sgu_norm_g, sgu_w, sgu_b, w_out, ffn_pre_g, ffn_post_g, ffn_w_up, ffn_conv_w,
           ffn_conv_b, ffn_w_down):
    batch, seq, d = x.shape
    depth = ada_w.shape[0]
    assert d == D_MODEL and seq % ROWS_PROJ == 0 and batch <= SUBLANES
    c_pad = jnp.pad(c, ((0, SUBLANES - batch), (0, 0)))
    mod_all = _modulation(c_pad, ada_w, ada_b)[:, :batch].reshape(depth, batch, 6, d)
    weights = _prepare_weights(w_in, w_out, ffn_w_up, ffn_conv_w, ffn_conv_b, ffn_w_down)
    x2 = x.reshape(batch * seq, d)
    for l in range(depth):
        x2 = _layer(x2, mod_all[l], l, batch, seq, weights, mix_pre_g[l], mix_post_g[l], dn_conv_w[l],
                    dn_a_log[l], dn_dt_bias[l], dn_norm_g[l], rel_bias[l], sinks[l], sgu_norm_g[l],
                    sgu_w[l], sgu_b[l], ffn_pre_g[l], ffn_post_g[l])
    return x2.reshape(batch, seq, d)
```

```python
import functools

import numpy as np
import jax
import jax.numpy as jnp
from jax import lax
from jax.experimental import pallas as pl
from jax.experimental.pallas import tpu as pltpu

F32 = jnp.float32
BF16 = jnp.bfloat16

D_MODEL = 2048
CHUNK = 64
HEAD_DIM = 128
HEADS = 4
D_GROUP = HEADS * HEAD_DIM
DN_CONV = 4
B_PREV_CHUNKS = 8
REL_CLIP = 256
C_PREV_CHUNKS = 2
C_KV_HEADS = 2
SGU_BLOCK = 128
D_FF = 5504
FFN_CONV = 3
EPS = 1e-6

LANES = 128
SUBLANES = 8
VMEM_LIMIT_BYTES = 56 * 1024 * 1024

N_SMALL = 2 * HEADS
N_MAIN = 11 * D_GROUP
N_LO_TILES = 4
COL_AQ, COL_AK, COL_AV, COL_AG, COL_BQ, COL_BK, COL_BV, COL_CQ, COL_CKV, COL_DU, COL_DV = range(11)
D_FF_PAD = 11 * D_GROUP

ROWS_ATT = 1024
Q_ROWS = 2 * CHUNK
DN_CHUNKS = 4
DN_ROWS = DN_CHUNKS * CHUNK
HALO = 16
ROW_CHUNK = 16
ROWS_PROJ = 2048
ROWS_OUT = 512
MOD_COLS = 1024

ALIBI_SLOPES = tuple(float(2.0 ** (-8.0 * h / HEADS)) for h in range(1, HEADS + 1))


def _cparams(*sem):
    return pltpu.CompilerParams(dimension_semantics=sem, vmem_limit_bytes=VMEM_LIMIT_BYTES)


def _sigmoid(x):
    return 1.0 / (1.0 + jnp.exp(-x))


def _silu(x):
    return x * _sigmoid(x)


def _softplus(x):
    return jnp.maximum(x, 0.0) + jnp.log1p(jnp.exp(-jnp.abs(x)))


def _gelu_tanh(x):
    return 0.5 * x * (1.0 + jnp.tanh(float(np.sqrt(2.0 / np.pi)) * (x + 0.044715 * (x * x * x))))


def _rms(x, g):
    return x * lax.rsqrt(jnp.mean(x * x, axis=-1, keepdims=True) + EPS) * g


def _modulated_rms(x, gain, shift):
    return x * lax.rsqrt(jnp.mean(x * x, axis=-1, keepdims=True) + EPS) * gain + shift


def _for_row_chunks(n_rows, body):
    for r0 in range(0, n_rows, ROW_CHUNK):
        body(slice(r0, r0 + ROW_CHUNK))


def _dot(a, b):
    return jnp.dot(a.astype(BF16), b.astype(BF16), preferred_element_type=F32)


def _dot_nt(a, b):
    return lax.dot_general(a.astype(BF16), b.astype(BF16), (((1,), (1,)), ((), ())),
                           preferred_element_type=F32)


def _dot_f32(a, b):
    return jnp.dot(a, b, precision=lax.Precision.HIGHEST, preferred_element_type=F32)


def _mod_kernel(c_ref, w_ref, b_ref, o_ref):
    cond = _silu(c_ref[...])
    o_ref[0] = _dot(cond, w_ref[0]) + b_ref[0]


def _modulation(c_pad, ada_w, ada_b):
    depth, d, n = ada_w.shape
    return pl.pallas_call(
        _mod_kernel,
        grid=(depth, n // MOD_COLS),
        in_specs=[pl.BlockSpec((SUBLANES, d), lambda l, j: (0, 0)),
                  pl.BlockSpec((1, d, MOD_COLS), lambda l, j: (l, 0, j)),
                  pl.BlockSpec((1, 1, MOD_COLS), lambda l, j: (l, 0, j))],
        out_specs=pl.BlockSpec((1, SUBLANES, MOD_COLS), lambda l, j: (l, 0, j)),
        out_shape=jax.ShapeDtypeStruct((depth, SUBLANES, n), F32),
        compiler_params=_cparams("parallel", "parallel"),
        name="modulation",
    )(c_pad, ada_w, ada_b.reshape(depth, 1, n))


def _inproj_kernel(x_hbm, mod_ref, g_ref, wlo_ref, whi_ref, ws_ref, p_ref, s_ref, st_ref, h_scr, x_ref, x_sem):
    m = pl.program_id(0)
    n = pl.program_id(1)

    def x_copy(tile):
        rows = pl.ds(pl.multiple_of(tile * ROWS_PROJ, ROWS_PROJ), ROWS_PROJ)
        return pltpu.make_async_copy(x_hbm.at[rows, :], x_ref, x_sem)

    @pl.when((m == 0) & (n == 0))
    def _():
        x_copy(0).start()

    @pl.when((n == 1) & (m + 1 < pl.num_programs(0)))
    def _():
        x_copy(m + 1).start()

    @pl.when(n == 0)
    def _():
        x_copy(m).wait()
        gain = g_ref[...] * (1.0 + mod_ref[0, 1:2, :])
        shift = mod_ref[0, 0:1, :]

        def fill(rows):
            h_scr[rows, :] = _modulated_rms(x_ref[rows, :], gain, shift).astype(BF16)

        _for_row_chunks(ROWS_PROJ, fill)
        small_t = _dot_nt(ws_ref[...], h_scr[...])
        s_ref[...] = small_t.T
        st_ref[...] = small_t[0:SUBLANES, :]

    @pl.when(n < N_LO_TILES)
    def _():
        p_ref[...] = _dot_nt(h_scr[...], wlo_ref[...]).astype(p_ref.dtype)

    @pl.when(n >= N_LO_TILES)
    def _():
        p_ref[...] = _dot_nt(h_scr[...], whi_ref[0]).astype(p_ref.dtype)


def _in_projection(x2, mod, pre_g, w_full, w_tail, w_small, layer, seq):
    t, d = x2.shape
    tiles_per_batch = seq // ROWS_PROJ
    last_tile = N_MAIN // D_GROUP - 1
    return pl.pallas_call(
        _inproj_kernel,
        grid=(t // ROWS_PROJ, N_MAIN // D_GROUP),
        in_specs=[pl.BlockSpec(memory_space=pl.ANY),
                  pl.BlockSpec((1, 6, d), lambda m, n: (m // tiles_per_batch, 0, 0)),
                  pl.BlockSpec((1, d), lambda m, n: (0, 0)),
                  pl.BlockSpec((None, D_GROUP, d), lambda m, n: (layer, jnp.where(n < N_LO_TILES, n, 0), 0)),
                  pl.BlockSpec((pl.Element(1), pl.Element(D_GROUP), pl.Element(d)),
                               lambda m, n: (layer, pl.multiple_of(
                                   N_SMALL + D_GROUP * jnp.where(n < N_LO_TILES, last_tile, n), SUBLANES), 0)),
                  pl.BlockSpec((None, LANES, d), lambda m, n: (layer, 0, 0))],
        out_specs=[pl.BlockSpec((ROWS_PROJ, D_GROUP), lambda m, n: (m, n)),
                   pl.BlockSpec((ROWS_PROJ, LANES), lambda m, n: (m, 0)),
                   pl.BlockSpec((SUBLANES, ROWS_PROJ), lambda m, n: (0, m))],
        out_shape=[jax.ShapeDtypeStruct((t, N_MAIN), BF16),
                   jax.ShapeDtypeStruct((t, LANES), F32),
                   jax.ShapeDtypeStruct((SUBLANES, t), F32)],
        scratch_shapes=[pltpu.VMEM((ROWS_PROJ, d), BF16),
                        pltpu.VMEM((ROWS_PROJ, d), F32),
                        pltpu.SemaphoreType.DMA(())],
        compiler_params=_cparams("arbitrary", "arbitrary"),
        name="in_projection",
    )(x2, mod, pre_g.reshape(1, d), w_full, w_tail, w_small)


def _unit_lower_inverse(mats, row, col):
    blk_r, blk_c = jnp.right_shift(row, 4), jnp.right_shift(col, 4)
    half_r, half_c = jnp.right_shift(row, 5), jnp.right_shift(col, 5)
    eye = (row == col).astype(F32)
    a_diag = [jnp.where(blk_r == blk_c, a, 0.0) for a in mats]
    a_pair = [jnp.where(half_r == half_c, a, 0.0) - d for a, d in zip(mats, a_diag)]
    a_far = [jnp.where(half_r != half_c, a, 0.0) for a in mats]
    p2 = [_dot(d, d) for d in a_diag]
    t = [eye - d for d in a_diag]
    t = [x + _dot(x, p) for x, p in zip(t, p2)]
    p4 = [_dot(p, p) for p in p2]
    t = [x + _dot(x, p) for x, p in zip(t, p4)]
    p8 = [_dot(p, p) for p in p4]
    t = [x + _dot(x, p) for x, p in zip(t, p8)]
    for off in (a_pair, a_far):
        left = [_dot(x, a) for x, a in zip(t, off)]
        t = [x - _dot(l, x) for x, l in zip(t, left)]
    return t


def _ffn_layout_step(tile, up_refs, down_refs, wa_ref, wg_ref, wd_ref):
    n_tiles = D_FF_PAD // D_GROUP

    @pl.when(tile < n_tiles)
    def _():
        col = tile * D_GROUP + lax.broadcasted_iota(jnp.int32, (1, D_GROUP), 1)
        valid = col < D_FF
        gate = jnp.concatenate([r[...] for r in up_refs[1:]], axis=1)
        wa_ref[...] = jnp.where(valid, up_refs[0][...], 0.0).astype(BF16)
        wg_ref[...] = jnp.where(valid, gate, 0.0).astype(BF16)

    @pl.when((tile >= n_tiles) & (tile < 2 * n_tiles))
    def _():
        row = (tile - n_tiles) * D_GROUP + lax.broadcasted_iota(jnp.int32, (D_GROUP, 1), 0)
        rows = jnp.concatenate([r[...] for r in down_refs], axis=0)
        wd_ref[...] = jnp.where(row < D_FF, rows, 0.0).astype(BF16)


def _delta_kernel(q_ref, k_ref, v_ref, qp_ref, kp_ref, vp_ref, gate_ref, sm_ref, smt_ref, cw_ref,
                  prow_ref, pcol_ref, ng_ref, ua_ref, ug0_ref, ug1_ref, ug2_ref, ug3_ref,
                  dr0_ref, dr1_ref, dr2_ref, dr3_ref, o_ref, wa_ref, wg_ref, wd_ref, state_scr):
    step = pl.program_id(1)
    _ffn_layout_step(pl.program_id(0) * pl.num_programs(1) + step,
                     (ua_ref, ug0_ref, ug1_ref, ug2_ref, ug3_ref),
                     (dr0_ref, dr1_ref, dr2_ref, dr3_ref), wa_ref, wg_ref, wd_ref)

    @pl.when(step == 0)
    def _():
        state_scr[...] = jnp.zeros_like(state_scr)

    keep = (step > 0).astype(F32)
    halo = jnp.concatenate([qp_ref[...], kp_ref[...], vp_ref[...]], axis=1).astype(F32) * keep
    cur = jnp.concatenate([q_ref[...], k_ref[...], v_ref[...]], axis=1).astype(F32)
    xe = jnp.concatenate([halo, cur], axis=0)
    acc = cw_ref[DN_CONV - 1:DN_CONV, :] * cur
    for back in range(1, DN_CONV):
        shifted = pltpu.roll(xe, back, 0)[HALO:, :]
        acc = acc + cw_ref[DN_CONV - 1 - back:DN_CONV - back, :] * shifted
    qkv = _silu(acc)

    row = lax.broadcasted_iota(jnp.int32, (CHUNK, CHUNK), 0)
    col = lax.broadcasted_iota(jnp.int32, (CHUNK, CHUNK), 1)
    causal = row >= col
    strict = row > col
    lower_ones = causal.astype(F32)
    upper_ones = (row <= col).astype(F32)

    sm = sm_ref[...]
    la_cols = -jnp.exp(prow_ref[1:2, :]) * _softplus(sm + prow_ref[0:1, :])
    beta_cols = _sigmoid(sm)
    neg_a_col = -jnp.exp(pcol_ref[:, 1:2])
    dt_col = pcol_ref[:, 0:1]

    scale = HEAD_DIM ** -0.5
    qs, ks, vbs, kbs = [], [], [], []
    for h in range(HEADS):
        lo, hi = h * HEAD_DIM, (h + 1) * HEAD_DIM
        q = qkv[:, lo:hi]
        k = qkv[:, D_GROUP + lo:D_GROUP + hi]
        v = qkv[:, 2 * D_GROUP + lo:2 * D_GROUP + hi]
        beta = beta_cols[:, h:h + 1]
        k = k * lax.rsqrt(jnp.sum(k * k, axis=-1, keepdims=True) + EPS)
        qs.append(q * lax.rsqrt(jnp.sum(q * q, axis=-1, keepdims=True) + EPS) * scale)
        ks.append(k)
        kbs.append(k * beta)
        vbs.append(v * beta)

    probs = [(c, h) for c in range(DN_CHUNKS) for h in range(HEADS)]
    g_cols, g_rows = [], []
    for c in range(DN_CHUNKS):
        rows = slice(c * CHUNK, (c + 1) * CHUNK)
        g_cols.append(_dot_f32(lower_ones, la_cols[rows]))
        g_rows.append(_dot_f32(neg_a_col * _softplus(smt_ref[c] + dt_col), upper_ones))

    def rows_of(c):
        return slice(c * CHUNK, (c + 1) * CHUNK)

    g_col = [g_cols[c][:, HEADS + h:HEADS + h + 1] for c, h in probs]
    g_last = [g[CHUNK - 1:CHUNK, :] for g in g_col]
    decay = [jnp.exp(jnp.where(causal, g - g_rows[c][HEADS + h:HEADS + h + 1, :], -jnp.inf))
             for g, (c, h) in zip(g_col, probs)]
    kk = [_dot_nt(kbs[h][rows_of(c)], ks[h][rows_of(c)]) for c, h in probs]
    qk = [_dot_nt(qs[h][rows_of(c)], ks[h][rows_of(c)]) for c, h in probs]
    t_inv = _unit_lower_inverse([jnp.where(strict, x * d, 0.0) for x, d in zip(kk, decay)], row, col)
    exp_g = [jnp.exp(g) for g in g_col]
    uw = [_dot(t, jnp.concatenate([vbs[h][rows_of(c)], kbs[h][rows_of(c)] * e], axis=1))
          for t, e, (c, h) in zip(t_inv, exp_g, probs)]
    attn = [(x * d).astype(BF16) for x, d in zip(qk, decay)]
    wq = [jnp.concatenate([x[:, HEAD_DIM:], qs[h][rows_of(c)] * e], axis=0).astype(BF16)
          for x, e, (c, h) in zip(uw, exp_g, probs)]
    k_dec_t = [(ks[h][rows_of(c)] * jnp.exp(gl - g)).T.astype(BF16)
               for g, gl, (c, h) in zip(g_col, g_last, probs)]
    decay_last = [jnp.exp(gl) for gl in g_last]

    state = [state_scr[h] for h in range(HEADS)]
    for c in range(DN_CHUNKS):
        idx = [c * HEADS + h for h in range(HEADS)]
        ws_qs = [jnp.dot(wq[i], state[h].astype(BF16), preferred_element_type=F32)
                 for h, i in enumerate(idx)]
        v_new = [(uw[i][:, :HEAD_DIM] - x[:CHUNK]).astype(BF16) for x, i in zip(ws_qs, idx)]
        state = [state[h] * decay_last[i] + jnp.dot(k_dec_t[i], v_new[h], preferred_element_type=F32)
                 for h, i in enumerate(idx)]
        for h, i in enumerate(idx):
            lo, hi = h * HEAD_DIM, (h + 1) * HEAD_DIM
            o = ws_qs[h][CHUNK:] + jnp.dot(attn[i], v_new[h], preferred_element_type=F32)
            o = _rms(o, ng_ref[...]) * _silu(gate_ref[rows_of(c), lo:hi].astype(F32))
            o_ref[rows_of(c), lo:hi] = o.astype(o_ref.dtype)
    for h in range(HEADS):
        state_scr[h] = state[h]


def _gated_deltanet(p, small, small_t3, conv_w, prm_row, prm_col, norm_g, w_up, w_down, layer, batch, seq):
    t = p.shape[0]
    d = w_up.shape[1]
    steps = seq // DN_ROWS
    halo_per_step = DN_ROWS // HALO
    per_tile = D_GROUP // LANES
    n_in = D_FF // LANES
    n_tiles = D_FF_PAD // D_GROUP
    assert batch * steps >= 2 * n_tiles

    def cur(j):
        return pl.BlockSpec((DN_ROWS, D_GROUP), lambda b, s: (b * steps + s, j))

    def prev(j):
        return pl.BlockSpec((HALO, D_GROUP),
                            lambda b, s: (jnp.maximum((b * steps + s) * halo_per_step - 1, 0), j))

    def up_tile(b, s):
        return jnp.minimum(b * steps + s, n_tiles - 1)

    def down_tile(b, s):
        return jnp.clip(b * steps + s - n_tiles, 0, n_tiles - 1)

    def gate_spec(k):
        return pl.BlockSpec((None, d, LANES), lambda b, s: (
            layer, 0, jnp.minimum(n_in + per_tile * up_tile(b, s) + k, 2 * n_in - 1)))

    def row_spec(k):
        return pl.BlockSpec((None, LANES, d), lambda b, s: (
            layer, jnp.minimum(per_tile * down_tile(b, s) + k, n_in - 1), 0))

    up_cols = pl.BlockSpec((d, D_GROUP), lambda b, s: (0, up_tile(b, s)))
    return pl.pallas_call(
        _delta_kernel,
        grid=(batch, steps),
        in_specs=[cur(COL_AQ), cur(COL_AK), cur(COL_AV), prev(COL_AQ), prev(COL_AK), prev(COL_AV),
                  cur(COL_AG),
                  pl.BlockSpec((DN_ROWS, LANES), lambda b, s: (b * steps + s, 0)),
                  pl.BlockSpec((DN_CHUNKS, SUBLANES, CHUNK), lambda b, s: (b * steps + s, 0, 0)),
                  pl.BlockSpec((DN_CONV, 3 * D_GROUP), lambda b, s: (0, 0)),
                  pl.BlockSpec((SUBLANES, LANES), lambda b, s: (0, 0)),
                  pl.BlockSpec((SUBLANES, LANES), lambda b, s: (0, 0)),
                  pl.BlockSpec((1, HEAD_DIM), lambda b, s: (0, 0)),
                  pl.BlockSpec((None, d, D_GROUP), lambda b, s: (layer, 0, up_tile(b, s)))]
                 + [gate_spec(k) for k in range(per_tile)] + [row_spec(k) for k in range(per_tile)],
        out_specs=[pl.BlockSpec((DN_ROWS, D_GROUP), lambda b, s: (b * steps + s, 0)),
                   up_cols, up_cols,
                   pl.BlockSpec((D_GROUP, d), lambda b, s: (down_tile(b, s), 0))],
        out_shape=[jax.ShapeDtypeStruct((t, D_GROUP), BF16),
                   jax.ShapeDtypeStruct((d, D_FF_PAD), BF16),
                   jax.ShapeDtypeStruct((d, D_FF_PAD), BF16),
                   jax.ShapeDtypeStruct((D_FF_PAD, d), BF16)],
        scratch_shapes=[pltpu.VMEM((HEADS, HEAD_DIM, HEAD_DIM), F32)],
        compiler_params=_cparams("arbitrary", "arbitrary"),
        name="gated_deltanet",
    )(p, p, p, p, p, p, p, small, small_t3, conv_w, prm_row, prm_col, norm_g.reshape(1, HEAD_DIM),
      w_up, w_up, w_up, w_up, w_up, w_down, w_down, w_down, w_down)


def _relbias_kernel(q_ref, kc_ref, kp_ref, vc_ref, vp_ref, bias_ref, o_ref, k_scr, v_scr):
    i = pl.program_id(1)
    halo = B_PREV_CHUNKS * CHUNK
    k_scr[0:halo, :] = kp_ref[...]
    k_scr[halo:, :] = kc_ref[...]
    v_scr[0:halo, :] = vp_ref[...]
    v_scr[halo:, :] = vc_ref[...]
    span = halo + Q_ROWS
    key_pos = lax.broadcasted_iota(jnp.int32, (Q_ROWS, span), 1)
    first_valid = jnp.where(i > 0, 0, halo)
    scale = HEAD_DIM ** -0.5
    subs = [sub * Q_ROWS for sub in range(ROWS_ATT // Q_ROWS)]
    for h in range(HEADS):
        lo, hi = h * HEAD_DIM, (h + 1) * HEAD_DIM
        bias = bias_ref[h]
        scores = [_dot_nt(q_ref[r0:r0 + Q_ROWS, lo:hi], k_scr[r0:r0 + span, lo:hi]) for r0 in subs]
        probs, invs = [], []
        for r0, s in zip(subs, scores):
            s = jnp.where(key_pos + r0 >= first_valid, s * scale + bias, -jnp.inf)
            e = jnp.exp(s - jnp.max(s, axis=-1, keepdims=True))
            probs.append(e.astype(BF16))
            invs.append(1.0 / jnp.sum(e, axis=-1, keepdims=True))
        outs = [jnp.dot(e, v_scr[r0:r0 + span, lo:hi], preferred_element_type=F32)
                for r0, e in zip(subs, probs)]
        for r0, o, inv in zip(subs, outs, invs):
            o_ref[r0:r0 + Q_ROWS, lo:hi] = (o * inv).astype(o_ref.dtype)


def _swa_kernel(sink_ref, q_ref, kvc_ref, kvp_ref, bias_ref, o_ref, kv_scr):
    i = pl.program_id(1)
    halo = C_PREV_CHUNKS * CHUNK
    kv_scr[0:halo, :] = kvp_ref[...]
    kv_scr[halo:, :] = kvc_ref[...]
    span = halo + Q_ROWS
    group = HEADS // C_KV_HEADS
    stacked = group * Q_ROWS
    key_pos = lax.broadcasted_iota(jnp.int32, (stacked, span), 1)
    q_row = lax.broadcasted_iota(jnp.int32, (stacked, 1), 0)
    first_valid = jnp.where(i > 0, 0, halo)
    scale = HEAD_DIM ** -0.5
    kv_width = C_KV_HEADS * HEAD_DIM
    for kvh in range(C_KV_HEADS):
        klo = kvh * HEAD_DIM
        h0 = kvh * group
        sink = jnp.where(q_row < Q_ROWS, sink_ref[h0], sink_ref[h0 + 1])
        bias = bias_ref[kvh]
        subs = [sub * Q_ROWS for sub in range(ROWS_ATT // Q_ROWS)]
        scores = []
        for r0 in subs:
            q = jnp.concatenate([q_ref[r0:r0 + Q_ROWS, (h0 + g) * HEAD_DIM:(h0 + g + 1) * HEAD_DIM]
                                 for g in range(group)], axis=0)
            scores.append(_dot_nt(q, kv_scr[r0:r0 + span, klo:klo + HEAD_DIM]))
        probs, invs = [], []
        for r0, s in zip(subs, scores):
            s = jnp.where(key_pos + r0 >= first_valid, s * scale + bias, -jnp.inf)
            m = jnp.maximum(jnp.max(s, axis=-1, keepdims=True), sink)
            e = jnp.exp(s - m)
            probs.append(e.astype(BF16))
            invs.append(1.0 / (jnp.sum(e, axis=-1, keepdims=True) + jnp.exp(sink - m)))
        vlo = kv_width + klo
        outs = [jnp.dot(e, kv_scr[r0:r0 + span, vlo:vlo + HEAD_DIM], preferred_element_type=F32)
                for r0, e in zip(subs, probs)]
        for r0, o, inv in zip(subs, outs, invs):
            o = o * inv
            for g in range(group):
                o_ref[r0:r0 + Q_ROWS, (h0 + g) * HEAD_DIM:(h0 + g + 1) * HEAD_DIM] = (
                    o[g * Q_ROWS:(g + 1) * Q_ROWS].astype(o_ref.dtype))


def _pair_mask(n_prev):
    span = n_prev * CHUNK + Q_ROWS
    r = np.arange(Q_ROWS)[:, None]
    j = np.arange(span)[None, :]
    band_pos = j - (r // CHUNK) * CHUNK
    return np.where((band_pos >= 0) & (band_pos < (n_prev + 1) * CHUNK), 0.0, -np.inf).astype(np.float32)


def _relbias_table(rel_bias):
    span = B_PREV_CHUNKS * CHUNK + Q_ROWS
    period = Q_ROWS + span
    k = np.arange(period) - (Q_ROWS - 1)
    idx = np.clip(B_PREV_CHUNKS * CHUNK - k, -REL_CLIP, REL_CLIP) + REL_CLIP
    g = rel_bias[:, idx]
    skew = jnp.tile(g, (1, Q_ROWS))[:, :Q_ROWS * (period - 1)].reshape(HEADS, Q_ROWS, period - 1)
    return skew[:, :, Q_ROWS - 1:Q_ROWS - 1 + span] + _pair_mask(B_PREV_CHUNKS)


def _alibi_table():
    halo = C_PREV_CHUNKS * CHUNK
    span = halo + Q_ROWS
    r = np.arange(Q_ROWS)[:, None]
    j = np.arange(span)[None, :]
    dist = np.abs(halo + r - j).astype(np.float32)
    per_head = [(-np.float32(s)) * dist + _pair_mask(C_PREV_CHUNKS) for s in ALIBI_SLOPES]
    group = HEADS // C_KV_HEADS
    return np.stack([np.concatenate(per_head[kv * group:(kv + 1) * group], axis=0)
                     for kv in range(C_KV_HEADS)], axis=0)


def _sgu_kernel(u_ref, v_ref, g_ref, w_ref, b_ref, o_ref):
    u = _gelu_tanh(u_ref[...].astype(F32))
    v = _gelu_tanh(v_ref[...].astype(F32))
    mu = jnp.mean(v, axis=-1, keepdims=True)
    var = jnp.mean(jnp.square(v - mu), axis=-1, keepdims=True)
    v = ((v - mu) * lax.rsqrt(var + EPS) * g_ref[...]).astype(BF16)
    row = lax.broadcasted_iota(jnp.int32, (SGU_BLOCK, SGU_BLOCK), 0)
    col = lax.broadcasted_iota(jnp.int32, (SGU_BLOCK, SGU_BLOCK), 1)
    for g in range(HEADS):
        lo, hi = g * HEAD_DIM, (g + 1) * HEAD_DIM
        w = jnp.where(row >= col, w_ref[g], 0.0).astype(BF16)
        bias = b_ref[:, g:g + 1]
        for n in range(ROWS_ATT // SGU_BLOCK):
            r0 = n * SGU_BLOCK
            mixed = jnp.dot(w, v[r0:r0 + SGU_BLOCK, lo:hi], preferred_element_type=F32) + bias
            o_ref[r0:r0 + SGU_BLOCK, lo:hi] = (u[r0:r0 + SGU_BLOCK, lo:hi] * mixed).astype(o_ref.dtype)


def _mixers_kernel(bq_ref, bkc_ref, bkp_ref, bvc_ref, bvp_ref, bbias_ref, sink_ref, cq_ref, ckvc_ref, ckvp_ref,
                   cbias_ref, du_ref, dv_ref, dg_ref, dw_ref, db_ref, ob_ref, oc_ref, od_ref,
                   k_scr, v_scr, kv_scr):
    _relbias_kernel(bq_ref, bkc_ref, bkp_ref, bvc_ref, bvp_ref, bbias_ref, ob_ref, k_scr, v_scr)
    _swa_kernel(sink_ref, cq_ref, ckvc_ref, ckvp_ref, cbias_ref, oc_ref, kv_scr)
    _sgu_kernel(du_ref, dv_ref, dg_ref, dw_ref, db_ref, od_ref)


def _banded_and_gating_mixers(p, rel_table, sinks, alibi, sgu_norm_g, sgu_w, sgu_b_cols, batch, seq):
    t = p.shape[0]
    nb = seq // ROWS_ATT
    b_halo = B_PREV_CHUNKS * CHUNK
    c_halo = C_PREV_CHUNKS * CHUNK

    def cur(j):
        return pl.BlockSpec((ROWS_ATT, D_GROUP), lambda b, i: (b * nb + i, j))

    def prev(rows, j):
        per_tile = ROWS_ATT // rows
        return pl.BlockSpec((rows, D_GROUP), lambda b, i: (jnp.maximum((b * nb + i) * per_tile - 1, 0), j))

    def whole(a):
        return pl.BlockSpec(a.shape, lambda b, i: (0,) * a.ndim)

    norm_g = sgu_norm_g.reshape(1, D_GROUP)
    out = pl.BlockSpec((ROWS_ATT, D_GROUP), lambda b, i: (b * nb + i, 0))
    return pl.pallas_call(
        _mixers_kernel,
        grid=(batch, nb),
        in_specs=[cur(COL_BQ), cur(COL_BK), prev(b_halo, COL_BK), cur(COL_BV), prev(b_halo, COL_BV),
                  whole(rel_table),
                  pl.BlockSpec(memory_space=pltpu.SMEM),
                  cur(COL_CQ), cur(COL_CKV), prev(c_halo, COL_CKV), whole(alibi),
                  cur(COL_DU), cur(COL_DV), whole(norm_g), whole(sgu_w), whole(sgu_b_cols)],
        out_specs=[out, out, out],
        out_shape=[jax.ShapeDtypeStruct((t, D_GROUP), BF16)] * 3,
        scratch_shapes=[pltpu.VMEM((b_halo + ROWS_ATT, D_GROUP), BF16),
                        pltpu.VMEM((b_halo + ROWS_ATT, D_GROUP), BF16),
                        pltpu.VMEM((c_halo + ROWS_ATT, D_GROUP), BF16)],
        compiler_params=_cparams("parallel", "arbitrary"),
        name="banded_and_gating_mixers",
    )(p, p, p, p, p, rel_table, sinks, p, p, p, alibi, p, p, norm_g, sgu_w, sgu_b_cols)


def _outproj_kernel(oa_ref, ob_ref, oc_ref, od_ref, w_ref, x_ref, mod_ref, g_ref, o_ref, w_scr):
    @pl.when(pl.program_id(0) == 0)
    def _():
        w_scr[...] = w_ref[...].astype(BF16)

    half = ROWS_OUT // 2
    scale = mod_ref[0, 2:3, :] * g_ref[...]
    for r0 in (0, half):
        rows = slice(r0, r0 + half)
        y = jnp.dot(oa_ref[rows, :], w_scr[0:D_GROUP, :], preferred_element_type=F32)
        y += jnp.dot(ob_ref[rows, :], w_scr[D_GROUP:2 * D_GROUP, :], preferred_element_type=F32)
        y += jnp.dot(oc_ref[rows, :], w_scr[2 * D_GROUP:3 * D_GROUP, :], preferred_element_type=F32)
        y += jnp.dot(od_ref[rows, :], w_scr[3 * D_GROUP:, :], preferred_element_type=F32)

        def finish(sub, y=y, r0=r0):
            part = y[sub, :]
            part = part * lax.rsqrt(jnp.mean(part * part, axis=-1, keepdims=True) + EPS)
            o_ref[r0 + sub.start:r0 + sub.stop, :] = x_ref[r0 + sub.start:r0 + sub.stop, :] + part * scale

        _for_row_chunks(half, finish)


def _out_projection(outs, w_out, x2, mod, post_g, layer, seq):
    t, d = x2.shape
    tiles_per_batch = seq // ROWS_OUT
    mix = pl.BlockSpec((ROWS_OUT, D_GROUP), lambda m: (m, 0))
    return pl.pallas_call(
        _outproj_kernel,
        grid=(t // ROWS_OUT,),
        in_specs=[mix, mix, mix, mix,
                  pl.BlockSpec((None, HEADS * D_GROUP, d), lambda m: (layer, 0, 0),
                               pipeline_mode=pl.Buffered(1)),
                  pl.BlockSpec((ROWS_OUT, d), lambda m: (m, 0)),
                  pl.BlockSpec((1, 6, d), lambda m: (m // tiles_per_batch, 0, 0)),
                  pl.BlockSpec((1, d), lambda m: (0, 0))],
        out_specs=pl.BlockSpec((ROWS_OUT, d), lambda m: (m, 0)),
        out_shape=jax.ShapeDtypeStruct((t, d), F32),
        scratch_shapes=[pltpu.VMEM((HEADS * D_GROUP, d), BF16)],
        compiler_params=_cparams("arbitrary"),
        name="out_projection",
    )(*outs, w_out, x2, mod, post_g.reshape(1, d))


def _ffn_kernel(x_ref, xp_ref, mod_ref, pre_ref, post_ref, wa_ref, wg_ref, wd_ref, cw_ref, cb_ref,
                o_ref, h_scr, a_scr, *, tiles_per_batch):
    m = pl.program_id(0)
    f = pl.program_id(1)

    def prenorm():
        gain = pre_ref[...] * (1.0 + mod_ref[0, 4:5, :])
        shift = mod_ref[0, 3:4, :]
        h_scr[0:HALO, :] = _modulated_rms(xp_ref[...], gain, shift).astype(BF16)

        def fill(rows):
            h_scr[HALO + rows.start:HALO + rows.stop, :] = (
                _modulated_rms(x_ref[rows, :], gain, shift).astype(BF16))

        _for_row_chunks(ROWS_OUT, fill)

    def channel_tile():
        keep = (m % tiles_per_batch != 0).astype(F32)
        a = jnp.dot(h_scr[...], wa_ref[...], preferred_element_type=F32)
        a_scr[0:HALO, :] = a[0:HALO] * keep
        a_scr[HALO:, :] = a[HALO:]
        gate = jnp.dot(h_scr[HALO:, :], wg_ref[...], preferred_element_type=F32)
        acc = cb_ref[...]
        for tap in range(FFN_CONV):
            off = HALO - (FFN_CONV - 1) + tap
            acc = acc + cw_ref[tap:tap + 1, :] * a_scr[off:off + ROWS_OUT, :]
        act = (_gelu_tanh(acc) * gate).astype(BF16)
        return jnp.dot(act, wd_ref[...], preferred_element_type=F32)

    @pl.when(f == 0)
    def _():
        prenorm()
        o_ref[...] = channel_tile()

    @pl.when(f > 0)
    def _():
        o_ref[...] += channel_tile()

    @pl.when(f == pl.num_programs(1) - 1)
    def _():
        scale = mod_ref[0, 5:6, :] * post_ref[...]

        def finish(rows):
            y = o_ref[rows, :]
            y = y * lax.rsqrt(jnp.mean(y * y, axis=-1, keepdims=True) + EPS)
            o_ref[rows, :] = x_ref[rows, :] + y * scale

        _for_row_chunks(ROWS_OUT, finish)


def _conv_ffn(x2, mod, pre_g, post_g, w_a, w_g, w_d, conv_w, conv_b, layer, seq):
    t, d = x2.shape
    tiles_per_batch = seq // ROWS_OUT
    halo_per_tile = ROWS_OUT // HALO
    return pl.pallas_call(
        functools.partial(_ffn_kernel, tiles_per_batch=tiles_per_batch),
        grid=(t // ROWS_OUT, D_FF_PAD // D_GROUP),
        in_specs=[pl.BlockSpec((ROWS_OUT, d), lambda m, f: (m, 0)),
                  pl.BlockSpec((HALO, d), lambda m, f: (jnp.maximum(m * halo_per_tile - 1, 0), 0)),
                  pl.BlockSpec((1, 6, d), lambda m, f: (m // tiles_per_batch, 0, 0)),
                  pl.BlockSpec((1, d), lambda m, f: (0, 0)),
                  pl.BlockSpec((1, d), lambda m, f: (0, 0)),
                  pl.BlockSpec((d, D_GROUP), lambda m, f: (0, f)),
                  pl.BlockSpec((d, D_GROUP), lambda m, f: (0, f)),
                  pl.BlockSpec((D_GROUP, d), lambda m, f: (f, 0)),
                  pl.BlockSpec((None, FFN_CONV, D_GROUP), lambda m, f: (layer, 0, f)),
                  pl.BlockSpec((None, 1, D_GROUP), lambda m, f: (layer, 0, f))],
        out_specs=pl.BlockSpec((ROWS_OUT, d), lambda m, f: (m, 0)),
        out_shape=jax.ShapeDtypeStruct((t, d), F32),
        scratch_shapes=[pltpu.VMEM((HALO + ROWS_OUT, d), BF16),
                        pltpu.VMEM((HALO + ROWS_OUT, D_GROUP), F32)],
        compiler_params=_cparams("parallel", "arbitrary"),
        name="conv_ffn",
    )(x2, x2, mod, pre_g.reshape(1, d), post_g.reshape(1, d), w_a, w_g, w_d, conv_w, conv_b)


def _pad_last(a, width):
    return jnp.pad(a, [(0, 0)] * (a.ndim - 1) + [(0, width - a.shape[-1])])


def _layer(x2, mod, layer, batch, seq, weights, mix_pre_g, mix_post_g, dn_conv_w, dn_a_log, dn_dt_bias,
           dn_norm_g, rel_bias, sinks, sgu_norm_g, sgu_w, sgu_b, ffn_pre_g, ffn_post_g):
    t = x2.shape[0]
    p, small, small_t = _in_projection(x2, mod, mix_pre_g, weights["in_full"], weights["in_tail"],
                                       weights["in_small"], layer, seq)
    small_t3 = small_t.reshape(SUBLANES, t // CHUNK, CHUNK).transpose(1, 0, 2)

    prm = jnp.stack([dn_dt_bias, dn_a_log], axis=0)
    prm_row = jnp.zeros((SUBLANES, LANES), F32).at[0:2, HEADS:2 * HEADS].set(prm)
    prm_col = jnp.zeros((SUBLANES, LANES), F32).at[HEADS:2 * HEADS, 0:2].set(prm.T)
    out_a, w_a, w_g, w_d = _gated_deltanet(p, small, small_t3, dn_conv_w, prm_row, prm_col, dn_norm_g,
                                           weights["ffn_up"], weights["ffn_down"], layer, batch, seq)
    out_b, out_c, out_d = _banded_and_gating_mixers(
        p, _relbias_table(rel_bias), sinks, jnp.asarray(_alibi_table()), sgu_norm_g, sgu_w,
        _pad_last(sgu_b.T, LANES), batch, seq)

    x2 = _out_projection((out_a, out_b, out_c, out_d), weights["out"], x2, mod, mix_post_g, layer, seq)
    return _conv_ffn(x2, mod, ffn_pre_g, ffn_post_g, w_a, w_g, w_d,
                     weights["ffn_conv_w"], weights["ffn_conv_b"], layer, seq)


def _prepare_weights(w_in, w_out, ffn_w_up, ffn_conv_w, ffn_conv_b, ffn_w_down):
    small_lo = N_LO_TILES * D_GROUP
    w_in_t = jnp.swapaxes(w_in, 1, 2)
    w_small = w_in_t[:, small_lo:small_lo + N_SMALL, :]
    return {
        "in_full": w_in_t,
        "in_tail": w_in_t,
        "in_small": jnp.pad(w_small, ((0, 0), (0, LANES - N_SMALL), (0, 0))).astype(BF16),
        "out": w_out,
        "ffn_up": ffn_w_up,
        "ffn_down": ffn_w_down,
        "ffn_conv_w": _pad_last(ffn_conv_w, D_FF_PAD),
        "ffn_conv_b": _pad_last(ffn_conv_b[:, None, :], D_FF_PAD),
    }


def kernel(x, c, ada_w, ada_b, mix_pre_g, mix_post_g, w_in, dn_conv_w, dn_a_log, dn_dt_bias, dn_norm_g,
           rel_bias, sinks, sgu_norm_g, sgu_w, sgu_b, w_out, ffn_pre_g, ffn_post_g, ffn_w_up, ffn_conv_w,
           ffn_conv_b, ffn_w_down):
    batch, seq, d = x.shape
    depth = ada_w.shape[0]
    assert d == D_MODEL and seq % ROWS_PROJ == 0 and batch <= SUBLANES
    c_pad = jnp.pad(c, ((0, SUBLANES - batch), (0, 0)))
    mod_all = _modulation(c_pad, ada_w, ada_b)[:, :batch].reshape(depth, batch, 6, d)
    weights = _prepare_weights(w_in, w_out, ffn_w_up, ffn_conv_w, ffn_conv_b, ffn_w_down)
    x2 = x.reshape(batch * seq, d)
    for l in range(depth):
        x2 = _layer(x2, mod_all[l], l, batch, seq, weights, mix_pre_g[l], mix_post_g[l], dn_conv_w[l],
                    dn_a_log[l], dn_dt_bias[l], dn_norm_g[l], rel_bias[l], sinks[l], sgu_norm_g[l],
                    sgu_w[l], sgu_b[l], ffn_pre_g[l], ffn_post_g[l])
    return x2.reshape(batch, seq, d)
```

```python
import functools

import numpy as np
import jax
import jax.numpy as jnp
from jax import lax
from jax.experimental import pallas as pl
from jax.experimental.pallas import tpu as pltpu

F32 = jnp.float32
BF16 = jnp.bfloat16

D_MODEL = 2048
CHUNK = 64
HEAD_DIM = 128
HEADS = 4
D_GROUP = HEADS * HEAD_DIM
DN_CONV = 4
B_PREV_CHUNKS = 8
REL_CLIP = 256
C_PREV_CHUNKS = 2
C_KV_HEADS = 2
SGU_BLOCK = 128
D_FF = 5504
FFN_CONV = 3
EPS = 1e-6

LANES = 128
SUBLANES = 8
VMEM_LIMIT_BYTES = 56 * 1024 * 1024

N_SMALL = 2 * HEADS
N_MAIN = 11 * D_GROUP
N_LO_TILES = 4
COL_AQ, COL_AK, COL_AV, COL_AG, COL_BQ, COL_BK, COL_BV, COL_CQ, COL_CKV, COL_DU, COL_DV = range(11)
D_FF_PAD = 11 * D_GROUP

ROWS_ATT = 1024
Q_ROWS = 2 * CHUNK
DN_CHUNKS = 8
DN_ROWS = DN_CHUNKS * CHUNK
HALO = 16
ROW_CHUNK = 16
ROWS_PROJ = 2048
ROWS_OUT = 512
MOD_COLS = 1024

ALIBI_SLOPES = tuple(float(2.0 ** (-8.0 * h / HEADS)) for h in range(1, HEADS + 1))


def _cparams(*sem):
    return pltpu.CompilerParams(dimension_semantics=sem, vmem_limit_bytes=VMEM_LIMIT_BYTES)


def _sigmoid(x):
    return 1.0 / (1.0 + jnp.exp(-x))


def _silu(x):
    return x * _sigmoid(x)


def _softplus(x):
    return jnp.maximum(x, 0.0) + jnp.log1p(jnp.exp(-jnp.abs(x)))


def _gelu_tanh(x):
    return 0.5 * x * (1.0 + jnp.tanh(float(np.sqrt(2.0 / np.pi)) * (x + 0.044715 * (x * x * x))))


def _rms(x, g):
    return x * lax.rsqrt(jnp.mean(x * x, axis=-1, keepdims=True) + EPS) * g


def _modulated_rms(x, gain, shift):
    return x * lax.rsqrt(jnp.mean(x * x, axis=-1, keepdims=True) + EPS) * gain + shift


def _for_row_chunks(n_rows, body):
    for r0 in range(0, n_rows, ROW_CHUNK):
        body(slice(r0, r0 + ROW_CHUNK))


def _dot(a, b):
    return jnp.dot(a.astype(BF16), b.astype(BF16), preferred_element_type=F32)


def _dot_nt(a, b):
    return lax.dot_general(a.astype(BF16), b.astype(BF16), (((1,), (1,)), ((), ())),
                           preferred_element_type=F32)


def _dot_f32(a, b):
    return jnp.dot(a, b, precision=lax.Precision.HIGHEST, preferred_element_type=F32)


def _mod_kernel(c_ref, w_ref, b_ref, o_ref):
    cond = _silu(c_ref[...])
    o_ref[0] = _dot(cond, w_ref[0]) + b_ref[0]


def _modulation(c_pad, ada_w, ada_b):
    depth, d, n = ada_w.shape
    return pl.pallas_call(
        _mod_kernel,
        grid=(depth, n // MOD_COLS),
        in_specs=[pl.BlockSpec((SUBLANES, d), lambda l, j: (0, 0)),
                  pl.BlockSpec((1, d, MOD_COLS), lambda l, j: (l, 0, j)),
                  pl.BlockSpec((1, 1, MOD_COLS), lambda l, j: (l, 0, j))],
        out_specs=pl.BlockSpec((1, SUBLANES, MOD_COLS), lambda l, j: (l, 0, j)),
        out_shape=jax.ShapeDtypeStruct((depth, SUBLANES, n), F32),
        compiler_params=_cparams("parallel", "parallel"),
        name="modulation",
    )(c_pad, ada_w, ada_b.reshape(depth, 1, n))


def _inproj_kernel(x_hbm, mod_ref, g_ref, wlo_ref, whi_ref, ws_ref, p_ref, s_ref, st_ref, h_scr, x_ref, x_sem):
    m = pl.program_id(0)
    n = pl.program_id(1)

    def x_copy(tile):
        rows = pl.ds(pl.multiple_of(tile * ROWS_PROJ, ROWS_PROJ), ROWS_PROJ)
        return pltpu.make_async_copy(x_hbm.at[rows, :], x_ref, x_sem)

    @pl.when((m == 0) & (n == 0))
    def _():
        x_copy(0).start()

    @pl.when((n == 1) & (m + 1 < pl.num_programs(0)))
    def _():
        x_copy(m + 1).start()

    @pl.when(n == 0)
    def _():
        x_copy(m).wait()
        gain = g_ref[...] * (1.0 + mod_ref[0, 1:2, :])
        shift = mod_ref[0, 0:1, :]

        def fill(rows):
            h_scr[rows, :] = _modulated_rms(x_ref[rows, :], gain, shift).astype(BF16)

        _for_row_chunks(ROWS_PROJ, fill)
        small_t = _dot_nt(ws_ref[...], h_scr[...])
        s_ref[...] = small_t.T
        st_ref[...] = small_t[0:SUBLANES, :]

    @pl.when(n < N_LO_TILES)
    def _():
        p_ref[...] = _dot_nt(h_scr[...], wlo_ref[...]).astype(p_ref.dtype)

    @pl.when(n >= N_LO_TILES)
    def _():
        p_ref[...] = _dot_nt(h_scr[...], whi_ref[0]).astype(p_ref.dtype)


def _in_projection(x2, mod, pre_g, w_full, w_tail, w_small, layer, seq):
    t, d = x2.shape
    tiles_per_batch = seq // ROWS_PROJ
    last_tile = N_MAIN // D_GROUP - 1
    return pl.pallas_call(
        _inproj_kernel,
        grid=(t // ROWS_PROJ, N_MAIN // D_GROUP),
        in_specs=[pl.BlockSpec(memory_space=pl.ANY),
                  pl.BlockSpec((1, 6, d), lambda m, n: (m // tiles_per_batch, 0, 0)),
                  pl.BlockSpec((1, d), lambda m, n: (0, 0)),
                  pl.BlockSpec((None, D_GROUP, d), lambda m, n: (layer, jnp.where(n < N_LO_TILES, n, 0), 0)),
                  pl.BlockSpec((pl.Element(1), pl.Element(D_GROUP), pl.Element(d)),
                               lambda m, n: (layer, pl.multiple_of(
                                   N_SMALL + D_GROUP * jnp.where(n < N_LO_TILES, last_tile, n), SUBLANES), 0)),
                  pl.BlockSpec((None, LANES, d), lambda m, n: (layer, 0, 0))],
        out_specs=[pl.BlockSpec((ROWS_PROJ, D_GROUP), lambda m, n: (m, n)),
                   pl.BlockSpec((ROWS_PROJ, LANES), lambda m, n: (m, 0)),
                   pl.BlockSpec((SUBLANES, ROWS_PROJ), lambda m, n: (0, m))],
        out_shape=[jax.ShapeDtypeStruct((t, N_MAIN), BF16),
                   jax.ShapeDtypeStruct((t, LANES), F32),
                   jax.ShapeDtypeStruct((SUBLANES, t), F32)],
        scratch_shapes=[pltpu.VMEM((ROWS_PROJ, d), BF16),
                        pltpu.VMEM((ROWS_PROJ, d), F32),
                        pltpu.SemaphoreType.DMA(())],
        compiler_params=_cparams("arbitrary", "arbitrary"),
        name="in_projection",
    )(x2, mod, pre_g.reshape(1, d), w_full, w_tail, w_small)


def _unit_lower_inverse(mats, row, col):
    blk_r, blk_c = jnp.right_shift(row, 4), jnp.right_shift(col, 4)
    half_r, half_c = jnp.right_shift(row, 5), jnp.right_shift(col, 5)
    eye = (row == col).astype(F32)
    a_diag = [jnp.where(blk_r == blk_c, a, 0.0) for a in mats]
    a_pair = [jnp.where(half_r == half_c, a, 0.0) - d for a, d in zip(mats, a_diag)]
    a_far = [jnp.where(half_r != half_c, a, 0.0) for a in mats]
    p2 = [_dot(d, d) for d in a_diag]
    t = [eye - d for d in a_diag]
    t = [x + _dot(x, p) for x, p in zip(t, p2)]
    p4 = [_dot(p, p) for p in p2]
    t = [x + _dot(x, p) for x, p in zip(t, p4)]
    p8 = [_dot(p, p) for p in p4]
    t = [x + _dot(x, p) for x, p in zip(t, p8)]
    for off in (a_pair, a_far):
        left = [_dot(x, a) for x, a in zip(t, off)]
        t = [x - _dot(l, x) for x, l in zip(t, left)]
    return t


def _ffn_layout_step(tile, up_refs, down_refs, wa_ref, wg_ref, wd_ref):
    @pl.when(tile < D_FF_PAD // D_GROUP)
    def _():
        col = tile * D_GROUP + lax.broadcasted_iota(jnp.int32, (1, D_GROUP), 1)
        row = tile * D_GROUP + lax.broadcasted_iota(jnp.int32, (D_GROUP, 1), 0)
        gate = jnp.concatenate([r[...] for r in up_refs[1:]], axis=1)
        rows = jnp.concatenate([r[...] for r in down_refs], axis=0)
        wa_ref[...] = jnp.where(col < D_FF, up_refs[0][...], 0.0).astype(BF16)
        wg_ref[...] = jnp.where(col < D_FF, gate, 0.0).astype(BF16)
        wd_ref[...] = jnp.where(row < D_FF, rows, 0.0).astype(BF16)


def _delta_kernel(q_ref, k_ref, v_ref, qp_ref, kp_ref, vp_ref, gate_ref, sm_ref, smt_ref, cw_ref,
                  prow_ref, pcol_ref, ng_ref, ua_ref, ug0_ref, ug1_ref, ug2_ref, ug3_ref,
                  dr0_ref, dr1_ref, dr2_ref, dr3_ref, o_ref, wa_ref, wg_ref, wd_ref, state_scr):
    step = pl.program_id(1)
    _ffn_layout_step(pl.program_id(0) * pl.num_programs(1) + step,
                     (ua_ref, ug0_ref, ug1_ref, ug2_ref, ug3_ref),
                     (dr0_ref, dr1_ref, dr2_ref, dr3_ref), wa_ref, wg_ref, wd_ref)

    @pl.when(step == 0)
    def _():
        state_scr[...] = jnp.zeros_like(state_scr)

    keep = (step > 0).astype(F32)
    halo = jnp.concatenate([qp_ref[...], kp_ref[...], vp_ref[...]], axis=1).astype(F32) * keep
    cur = jnp.concatenate([q_ref[...], k_ref[...], v_ref[...]], axis=1).astype(F32)
    xe = jnp.concatenate([halo, cur], axis=0)
    acc = cw_ref[DN_CONV - 1:DN_CONV, :] * cur
    for back in range(1, DN_CONV):
        shifted = pltpu.roll(xe, back, 0)[HALO:, :]
        acc = acc + cw_ref[DN_CONV - 1 - back:DN_CONV - back, :] * shifted
    qkv = _silu(acc)

    row = lax.broadcasted_iota(jnp.int32, (CHUNK, CHUNK), 0)
    col = lax.broadcasted_iota(jnp.int32, (CHUNK, CHUNK), 1)
    causal = row >= col
    strict = row > col
    lower_ones = causal.astype(F32)
    upper_ones = (row <= col).astype(F32)

    sm = sm_ref[...]
    la_cols = -jnp.exp(prow_ref[1:2, :]) * _softplus(sm + prow_ref[0:1, :])
    beta_cols = _sigmoid(sm)
    neg_a_col = -jnp.exp(pcol_ref[:, 1:2])
    dt_col = pcol_ref[:, 0:1]

    scale = HEAD_DIM ** -0.5
    qs, ks, vbs, kbs = [], [], [], []
    for h in range(HEADS):
        lo, hi = h * HEAD_DIM, (h + 1) * HEAD_DIM
        q = qkv[:, lo:hi]
        k = qkv[:, D_GROUP + lo:D_GROUP + hi]
        v = qkv[:, 2 * D_GROUP + lo:2 * D_GROUP + hi]
        beta = beta_cols[:, h:h + 1]
        k = k * lax.rsqrt(jnp.sum(k * k, axis=-1, keepdims=True) + EPS)
        qs.append(q * lax.rsqrt(jnp.sum(q * q, axis=-1, keepdims=True) + EPS) * scale)
        ks.append(k)
        kbs.append(k * beta)
        vbs.append(v * beta)

    probs = [(c, h) for c in range(DN_CHUNKS) for h in range(HEADS)]
    g_cols, g_rows = [], []
    for c in range(DN_CHUNKS):
        rows = slice(c * CHUNK, (c + 1) * CHUNK)
        g_cols.append(_dot_f32(lower_ones, la_cols[rows]))
        g_rows.append(_dot_f32(neg_a_col * _softplus(smt_ref[c] + dt_col), upper_ones))

    def rows_of(c):
        return slice(c * CHUNK, (c + 1) * CHUNK)

    g_col = [g_cols[c][:, HEADS + h:HEADS + h + 1] for c, h in probs]
    g_last = [g[CHUNK - 1:CHUNK, :] for g in g_col]
    decay = [jnp.exp(jnp.where(causal, g - g_rows[c][HEADS + h:HEADS + h + 1, :], -jnp.inf))
             for g, (c, h) in zip(g_col, probs)]
    kk = [_dot_nt(kbs[h][rows_of(c)], ks[h][rows_of(c)]) for c, h in probs]
    qk = [_dot_nt(qs[h][rows_of(c)], ks[h][rows_of(c)]) for c, h in probs]
    t_inv = _unit_lower_inverse([jnp.where(strict, x * d, 0.0) for x, d in zip(kk, decay)], row, col)
    exp_g = [jnp.exp(g) for g in g_col]
    uw = [_dot(t, jnp.concatenate([vbs[h][rows_of(c)], kbs[h][rows_of(c)] * e], axis=1))
          for t, e, (c, h) in zip(t_inv, exp_g, probs)]
    attn = [(x * d).astype(BF16) for x, d in zip(qk, decay)]
    wq = [jnp.concatenate([x[:, HEAD_DIM:], qs[h][rows_of(c)] * e], axis=0).astype(BF16)
          for x, e, (c, h) in zip(uw, exp_g, probs)]
    k_dec_t = [(ks[h][rows_of(c)] * jnp.exp(gl - g)).T.astype(BF16)
               for g, gl, (c, h) in zip(g_col, g_last, probs)]
    decay_last = [jnp.exp(gl) for gl in g_last]

    state = [state_scr[h] for h in range(HEADS)]
    for c in range(DN_CHUNKS):
        idx = [c * HEADS + h for h in range(HEADS)]
        ws_qs = [jnp.dot(wq[i], state[h].astype(BF16), preferred_element_type=F32)
                 for h, i in enumerate(idx)]
        v_new = [(uw[i][:, :HEAD_DIM] - x[:CHUNK]).astype(BF16) for x, i in zip(ws_qs, idx)]
        state = [state[h] * decay_last[i] + jnp.dot(k_dec_t[i], v_new[h], preferred_element_type=F32)
                 for h, i in enumerate(idx)]
        for h, i in enumerate(idx):
            lo, hi = h * HEAD_DIM, (h + 1) * HEAD_DIM
            o = ws_qs[h][CHUNK:] + jnp.dot(attn[i], v_new[h], preferred_element_type=F32)
            o = _rms(o, ng_ref[...]) * _silu(gate_ref[rows_of(c), lo:hi].astype(F32))
            o_ref[rows_of(c), lo:hi] = o.astype(o_ref.dtype)
    for h in range(HEADS):
        state_scr[h] = state[h]


def _gated_deltanet(p, small, small_t3, conv_w, prm_row, prm_col, norm_g, w_up, w_down, layer, batch, seq):
    t = p.shape[0]
    d = w_up.shape[1]
    steps = seq // DN_ROWS
    halo_per_step = DN_ROWS // HALO
    per_tile = D_GROUP // LANES
    n_in = D_FF // LANES
    n_tiles = D_FF_PAD // D_GROUP
    assert batch * steps >= n_tiles

    def cur(j):
        return pl.BlockSpec((DN_ROWS, D_GROUP), lambda b, s: (b * steps + s, j))

    def prev(j):
        return pl.BlockSpec((HALO, D_GROUP),
                            lambda b, s: (jnp.maximum((b * steps + s) * halo_per_step - 1, 0), j))

    def up_tile(b, s):
        return jnp.minimum(b * steps + s, n_tiles - 1)

    down_tile = up_tile

    def gate_spec(k):
        return pl.BlockSpec((None, d, LANES), lambda b, s: (
            layer, 0, jnp.minimum(n_in + per_tile * up_tile(b, s) + k, 2 * n_in - 1)))

    def row_spec(k):
        return pl.BlockSpec((None, LANES, d), lambda b, s: (
            layer, jnp.minimum(per_tile * down_tile(b, s) + k, n_in - 1), 0))

    up_cols = pl.BlockSpec((d, D_GROUP), lambda b, s: (0, up_tile(b, s)))
    return pl.pallas_call(
        _delta_kernel,
        grid=(batch, steps),
        in_specs=[cur(COL_AQ), cur(COL_AK), cur(COL_AV), prev(COL_AQ), prev(COL_AK), prev(COL_AV),
                  cur(COL_AG),
                  pl.BlockSpec((DN_ROWS, LANES), lambda b, s: (b * steps + s, 0)),
                  pl.BlockSpec((DN_CHUNKS, SUBLANES, CHUNK), lambda b, s: (b * steps + s, 0, 0)),
                  pl.BlockSpec((DN_CONV, 3 * D_GROUP), lambda b, s: (0, 0)),
                  pl.BlockSpec((SUBLANES, LANES), lambda b, s: (0, 0)),
                  pl.BlockSpec((SUBLANES, LANES), lambda b, s: (0, 0)),
                  pl.BlockSpec((1, HEAD_DIM), lambda b, s: (0, 0)),
                  pl.BlockSpec((None, d, D_GROUP), lambda b, s: (layer, 0, up_tile(b, s)))]
                 + [gate_spec(k) for k in range(per_tile)] + [row_spec(k) for k in range(per_tile)],
        out_specs=[pl.BlockSpec((DN_ROWS, D_GROUP), lambda b, s: (b * steps + s, 0)),
                   up_cols, up_cols,
                   pl.BlockSpec((D_GROUP, d), lambda b, s: (down_tile(b, s), 0))],
        out_shape=[jax.ShapeDtypeStruct((t, D_GROUP), BF16),
                   jax.ShapeDtypeStruct((d, D_FF_PAD), BF16),
                   jax.ShapeDtypeStruct((d, D_FF_PAD), BF16),
                   jax.ShapeDtypeStruct((D_FF_PAD, d), BF16)],
        scratch_shapes=[pltpu.VMEM((HEADS, HEAD_DIM, HEAD_DIM), F32)],
        compiler_params=_cparams("arbitrary", "arbitrary"),
        name="gated_deltanet",
    )(p, p, p, p, p, p, p, small, small_t3, conv_w, prm_row, prm_col, norm_g.reshape(1, HEAD_DIM),
      w_up, w_up, w_up, w_up, w_up, w_down, w_down, w_down, w_down)


def _relbias_kernel(q_ref, kc_ref, kp_ref, vc_ref, vp_ref, bias_ref, o_ref, k_scr, v_scr):
    i = pl.program_id(1)
    halo = B_PREV_CHUNKS * CHUNK
    k_scr[0:halo, :] = kp_ref[...]
    k_scr[halo:, :] = kc_ref[...]
    v_scr[0:halo, :] = vp_ref[...]
    v_scr[halo:, :] = vc_ref[...]
    span = halo + Q_ROWS
    key_pos = lax.broadcasted_iota(jnp.int32, (Q_ROWS, span), 1)
    first_valid = jnp.where(i > 0, 0, halo)
    scale = HEAD_DIM ** -0.5
    subs = [sub * Q_ROWS for sub in range(ROWS_ATT // Q_ROWS)]
    for h in range(HEADS):
        lo, hi = h * HEAD_DIM, (h + 1) * HEAD_DIM
        bias = bias_ref[h]
        scores = [_dot_nt(q_ref[r0:r0 + Q_ROWS, lo:hi], k_scr[r0:r0 + span, lo:hi]) for r0 in subs]
        probs, invs = [], []
        for r0, s in zip(subs, scores):
            s = jnp.where(key_pos + r0 >= first_valid, s * scale + bias, -jnp.inf)
            e = jnp.exp(s - jnp.max(s, axis=-1, keepdims=True))
            probs.append(e.astype(BF16))
            invs.append(1.0 / jnp.sum(e, axis=-1, keepdims=True))
        outs = [jnp.dot(e, v_scr[r0:r0 + span, lo:hi], preferred_element_type=F32)
                for r0, e in zip(subs, probs)]
        for r0, o, inv in zip(subs, outs, invs):
            o_ref[r0:r0 + Q_ROWS, lo:hi] = (o * inv).astype(o_ref.dtype)


def _swa_kernel(sink_ref, q_ref, kvc_ref, kvp_ref, bias_ref, o_ref, kv_scr):
    i = pl.program_id(1)
    halo = C_PREV_CHUNKS * CHUNK
    kv_scr[0:halo, :] = kvp_ref[...]
    kv_scr[halo:, :] = kvc_ref[...]
    span = halo + Q_ROWS
    group = HEADS // C_KV_HEADS
    stacked = group * Q_ROWS
    key_pos = lax.broadcasted_iota(jnp.int32, (stacked, span), 1)
    q_row = lax.broadcasted_iota(jnp.int32, (stacked, 1), 0)
    first_valid = jnp.where(i > 0, 0, halo)
    scale = HEAD_DIM ** -0.5
    kv_width = C_KV_HEADS * HEAD_DIM
    for kvh in range(C_KV_HEADS):
        klo = kvh * HEAD_DIM
        h0 = kvh * group
        sink = jnp.where(q_row < Q_ROWS, sink_ref[h0], sink_ref[h0 + 1])
        bias = bias_ref[kvh]
        subs = [sub * Q_ROWS for sub in range(ROWS_ATT // Q_ROWS)]
        scores = []
        for r0 in subs:
            q = jnp.concatenate([q_ref[r0:r0 + Q_ROWS, (h0 + g) * HEAD_DIM:(h0 + g + 1) * HEAD_DIM]
                                 for g in range(group)], axis=0)
            scores.append(_dot_nt(q, kv_scr[r0:r0 + span, klo:klo + HEAD_DIM]))
        probs, invs = [], []
        for r0, s in zip(subs, scores):
            s = jnp.where(key_pos + r0 >= first_valid, s * scale + bias, -jnp.inf)
            m = jnp.maximum(jnp.max(s, axis=-1, keepdims=True), sink)
            e = jnp.exp(s - m)
            probs.append(e.astype(BF16))
            invs.append(1.0 / (jnp.sum(e, axis=-1, keepdims=True) + jnp.exp(sink - m)))
        vlo = kv_width + klo
        outs = [jnp.dot(e, kv_scr[r0:r0 + span, vlo:vlo + HEAD_DIM], preferred_element_type=F32)
                for r0, e in zip(subs, probs)]
        for r0, o, inv in zip(subs, outs, invs):
            o = o * inv
            for g in range(group):
                o_ref[r0:r0 + Q_ROWS, (h0 + g) * HEAD_DIM:(h0 + g + 1) * HEAD_DIM] = (
                    o[g * Q_ROWS:(g + 1) * Q_ROWS].astype(o_ref.dtype))


def _pair_mask(n_prev):
    span = n_prev * CHUNK + Q_ROWS
    r = np.arange(Q_ROWS)[:, None]
    j = np.arange(span)[None, :]
    band_pos = j - (r // CHUNK) * CHUNK
    return np.where((band_pos >= 0) & (band_pos < (n_prev + 1) * CHUNK), 0.0, -np.inf).astype(np.float32)


def _relbias_table(rel_bias):
    span = B_PREV_CHUNKS * CHUNK + Q_ROWS
    period = Q_ROWS + span
    k = np.arange(period) - (Q_ROWS - 1)
    idx = np.clip(B_PREV_CHUNKS * CHUNK - k, -REL_CLIP, REL_CLIP) + REL_CLIP
    g = rel_bias[:, idx]
    skew = jnp.tile(g, (1, Q_ROWS))[:, :Q_ROWS * (period - 1)].reshape(HEADS, Q_ROWS, period - 1)
    return skew[:, :, Q_ROWS - 1:Q_ROWS - 1 + span] + _pair_mask(B_PREV_CHUNKS)


def _alibi_table():
    halo = C_PREV_CHUNKS * CHUNK
    span = halo + Q_ROWS
    r = np.arange(Q_ROWS)[:, None]
    j = np.arange(span)[None, :]
    dist = np.abs(halo + r - j).astype(np.float32)
    per_head = [(-np.float32(s)) * dist + _pair_mask(C_PREV_CHUNKS) for s in ALIBI_SLOPES]
    group = HEADS // C_KV_HEADS
    return np.stack([np.concatenate(per_head[kv * group:(kv + 1) * group], axis=0)
                     for kv in range(C_KV_HEADS)], axis=0)


def _sgu_kernel(u_ref, v_ref, g_ref, w_ref, b_ref, o_ref):
    u = _gelu_tanh(u_ref[...].astype(F32))
    v = _gelu_tanh(v_ref[...].astype(F32))
    mu = jnp.mean(v, axis=-1, keepdims=True)
    var = jnp.mean(jnp.square(v - mu), axis=-1, keepdims=True)
    v = ((v - mu) * lax.rsqrt(var + EPS) * g_ref[...]).astype(BF16)
    row = lax.broadcasted_iota(jnp.int32, (SGU_BLOCK, SGU_BLOCK), 0)
    col = lax.broadcasted_iota(jnp.int32, (SGU_BLOCK, SGU_BLOCK), 1)
    for g in range(HEADS):
        lo, hi = g * HEAD_DIM, (g + 1) * HEAD_DIM
        w = jnp.where(row >= col, w_ref[g], 0.0).astype(BF16)
        bias = b_ref[:, g:g + 1]
        for n in range(ROWS_ATT // SGU_BLOCK):
            r0 = n * SGU_BLOCK
            mixed = jnp.dot(w, v[r0:r0 + SGU_BLOCK, lo:hi], preferred_element_type=F32) + bias
            o_ref[r0:r0 + SGU_BLOCK, lo:hi] = (u[r0:r0 + SGU_BLOCK, lo:hi] * mixed).astype(o_ref.dtype)


def _mixers_kernel(bq_ref, bkc_ref, bkp_ref, bvc_ref, bvp_ref, bbias_ref, sink_ref, cq_ref, ckvc_ref, ckvp_ref,
                   cbias_ref, du_ref, dv_ref, dg_ref, dw_ref, db_ref, ob_ref, oc_ref, od_ref,
                   k_scr, v_scr, kv_scr):
    _relbias_kernel(bq_ref, bkc_ref, bkp_ref, bvc_ref, bvp_ref, bbias_ref, ob_ref, k_scr, v_scr)
    _swa_kernel(sink_ref, cq_ref, ckvc_ref, ckvp_ref, cbias_ref, oc_ref, kv_scr)
    _sgu_kernel(du_ref, dv_ref, dg_ref, dw_ref, db_ref, od_ref)


def _banded_and_gating_mixers(p, rel_table, sinks, alibi, sgu_norm_g, sgu_w, sgu_b_cols, batch, seq):
    t = p.shape[0]
    nb = seq // ROWS_ATT
    b_halo = B_PREV_CHUNKS * CHUNK
    c_halo = C_PREV_CHUNKS * CHUNK

    def cur(j):
        return pl.BlockSpec((ROWS_ATT, D_GROUP), lambda b, i: (b * nb + i, j))

    def prev(rows, j):
        per_tile = ROWS_ATT // rows
        return pl.BlockSpec((rows, D_GROUP), lambda b, i: (jnp.maximum((b * nb + i) * per_tile - 1, 0), j))

    def whole(a):
        return pl.BlockSpec(a.shape, lambda b, i: (0,) * a.ndim)

    norm_g = sgu_norm_g.reshape(1, D_GROUP)
    out = pl.BlockSpec((ROWS_ATT, D_GROUP), lambda b, i: (b * nb + i, 0))
    return pl.pallas_call(
        _mixers_kernel,
        grid=(batch, nb),
        in_specs=[cur(COL_BQ), cur(COL_BK), prev(b_halo, COL_BK), cur(COL_BV), prev(b_halo, COL_BV),
                  whole(rel_table),
                  pl.BlockSpec(memory_space=pltpu.SMEM),
                  cur(COL_CQ), cur(COL_CKV), prev(c_halo, COL_CKV), whole(alibi),
                  cur(COL_DU), cur(COL_DV), whole(norm_g), whole(sgu_w), whole(sgu_b_cols)],
        out_specs=[out, out, out],
        out_shape=[jax.ShapeDtypeStruct((t, D_GROUP), BF16)] * 3,
        scratch_shapes=[pltpu.VMEM((b_halo + ROWS_ATT, D_GROUP), BF16),
                        pltpu.VMEM((b_halo + ROWS_ATT, D_GROUP), BF16),
                        pltpu.VMEM((c_halo + ROWS_ATT, D_GROUP), BF16)],
        compiler_params=_cparams("parallel", "arbitrary"),
        name="banded_and_gating_mixers",
    )(p, p, p, p, p, rel_table, sinks, p, p, p, alibi, p, p, norm_g, sgu_w, sgu_b_cols)


def _outproj_kernel(oa_ref, ob_ref, oc_ref, od_ref, w_ref, x_ref, mod_ref, g_ref, o_ref, w_scr):
    @pl.when(pl.program_id(0) == 0)
    def _():
        w_scr[...] = w_ref[...].astype(BF16)

    half = ROWS_OUT // 2
    scale = mod_ref[0, 2:3, :] * g_ref[...]
    for r0 in (0, half):
        rows = slice(r0, r0 + half)
        y = jnp.dot(oa_ref[rows, :], w_scr[0:D_GROUP, :], preferred_element_type=F32)
        y += jnp.dot(ob_ref[rows, :], w_scr[D_GROUP:2 * D_GROUP, :], preferred_element_type=F32)
        y += jnp.dot(oc_ref[rows, :], w_scr[2 * D_GROUP:3 * D_GROUP, :], preferred_element_type=F32)
        y += jnp.dot(od_ref[rows, :], w_scr[3 * D_GROUP:, :], preferred_element_type=F32)

        def finish(sub, y=y, r0=r0):
            part = y[sub, :]
            part = part * lax.rsqrt(jnp.mean(part * part, axis=-1, keepdims=True) + EPS)
            o_ref[r0 + sub.start:r0 + sub.stop, :] = x_ref[r0 + sub.start:r0 + sub.stop, :] + part * scale

        _for_row_chunks(half, finish)


def _out_projection(outs, w_out, x2, mod, post_g, layer, seq):
    t, d = x2.shape
    tiles_per_batch = seq // ROWS_OUT
    mix = pl.BlockSpec((ROWS_OUT, D_GROUP), lambda m: (m, 0))
    return pl.pallas_call(
        _outproj_kernel,
        grid=(t // ROWS_OUT,),
        in_specs=[mix, mix, mix, mix,
                  pl.BlockSpec((None, HEADS * D_GROUP, d), lambda m: (layer, 0, 0),
                               pipeline_mode=pl.Buffered(1)),
                  pl.BlockSpec((ROWS_OUT, d), lambda m: (m, 0)),
                  pl.BlockSpec((1, 6, d), lambda m: (m // tiles_per_batch, 0, 0)),
                  pl.BlockSpec((1, d), lambda m: (0, 0))],
        out_specs=pl.BlockSpec((ROWS_OUT, d), lambda m: (m, 0)),
        out_shape=jax.ShapeDtypeStruct((t, d), F32),
        scratch_shapes=[pltpu.VMEM((HEADS * D_GROUP, d), BF16)],
        compiler_params=_cparams("arbitrary"),
        name="out_projection",
    )(*outs, w_out, x2, mod, post_g.reshape(1, d))


def _ffn_kernel(x_ref, xp_ref, mod_ref, pre_ref, post_ref, wa_ref, wg_ref, wd_ref, cw_ref, cb_ref,
                o_ref, h_scr, a_scr, *, tiles_per_batch):
    m = pl.program_id(0)
    f = pl.program_id(1)

    def prenorm():
        gain = pre_ref[...] * (1.0 + mod_ref[0, 4:5, :])
        shift = mod_ref[0, 3:4, :]
        h_scr[0:HALO, :] = _modulated_rms(xp_ref[...], gain, shift).astype(BF16)

        def fill(rows):
            h_scr[HALO + rows.start:HALO + rows.stop, :] = (
                _modulated_rms(x_ref[rows, :], gain, shift).astype(BF16))

        _for_row_chunks(ROWS_OUT, fill)

    def channel_tile():
        keep = (m % tiles_per_batch != 0).astype(F32)
        a = jnp.dot(h_scr[...], wa_ref[...], preferred_element_type=F32)
        a_scr[0:HALO, :] = a[0:HALO] * keep
        a_scr[HALO:, :] = a[HALO:]
        gate = jnp.dot(h_scr[HALO:, :], wg_ref[...], preferred_element_type=F32)
        acc = cb_ref[...]
        for tap in range(FFN_CONV):
            off = HALO - (FFN_CONV - 1) + tap
            acc = acc + cw_ref[tap:tap + 1, :] * a_scr[off:off + ROWS_OUT, :]
        act = (_gelu_tanh(acc) * gate).astype(BF16)
        return jnp.dot(act, wd_ref[...], preferred_element_type=F32)

    @pl.when(f == 0)
    def _():
        prenorm()
        o_ref[...] = channel_tile()

    @pl.when(f > 0)
    def _():
        o_ref[...] += channel_tile()

    @pl.when(f == pl.num_programs(1) - 1)
    def _():
        scale = mod_ref[0, 5:6, :] * post_ref[...]

        def finish(rows):
            y = o_ref[rows, :]
            y = y * lax.rsqrt(jnp.mean(y * y, axis=-1, keepdims=True) + EPS)
            o_ref[rows, :] = x_ref[rows, :] + y * scale

        _for_row_chunks(ROWS_OUT, finish)


def _conv_ffn(x2, mod, pre_g, post_g, w_a, w_g, w_d, conv_w, conv_b, layer, seq):
    t, d = x2.shape
    tiles_per_batch = seq // ROWS_OUT
    halo_per_tile = ROWS_OUT // HALO
    return pl.pallas_call(
        functools.partial(_ffn_kernel, tiles_per_batch=tiles_per_batch),
        grid=(t // ROWS_OUT, D_FF_PAD // D_GROUP),
        in_specs=[pl.BlockSpec((ROWS_OUT, d), lambda m, f: (m, 0)),
                  pl.BlockSpec((HALO, d), lambda m, f: (jnp.maximum(m * halo_per_tile - 1, 0), 0)),
                  pl.BlockSpec((1, 6, d), lambda m, f: (m // tiles_per_batch, 0, 0)),
                  pl.BlockSpec((1, d), lambda m, f: (0, 0)),
                  pl.BlockSpec((1, d), lambda m, f: (0, 0)),
                  pl.BlockSpec((d, D_GROUP), lambda m, f: (0, f)),
                  pl.BlockSpec((d, D_GROUP), lambda m, f: (0, f)),
                  pl.BlockSpec((D_GROUP, d), lambda m, f: (f, 0)),
                  pl.BlockSpec((None, FFN_CONV, D_GROUP), lambda m, f: (layer, 0, f)),
                  pl.BlockSpec((None, 1, D_GROUP), lambda m, f: (layer, 0, f))],
        out_specs=pl.BlockSpec((ROWS_OUT, d), lambda m, f: (m, 0)),
        out_shape=jax.ShapeDtypeStruct((t, d), F32),
        scratch_shapes=[pltpu.VMEM((HALO + ROWS_OUT, d), BF16),
                        pltpu.VMEM((HALO + ROWS_OUT, D_GROUP), F32)],
        compiler_params=_cparams("parallel", "arbitrary"),
        name="conv_ffn",
    )(x2, x2, mod, pre_g.reshape(1, d), post_g.reshape(1, d), w_a, w_g, w_d, conv_w, conv_b)


def _pad_last(a, width):
    return jnp.pad(a, [(0, 0)] * (a.ndim - 1) + [(0, width - a.shape[-1])])


def _layer(x2, mod, layer, batch, seq, weights, mix_pre_g, mix_post_g, dn_conv_w, dn_a_log, dn_dt_bias,
           dn_norm_g, rel_bias, sinks, sgu_norm_g, sgu_w, sgu_b, ffn_pre_g, ffn_post_g):
    t = x2.shape[0]
    p, small, small_t = _in_projection(x2, mod, mix_pre_g, weights["in_full"], weights["in_tail"],
                                       weights["in_small"], layer, seq)
    small_t3 = small_t.reshape(SUBLANES, t // CHUNK, CHUNK).transpose(1, 0, 2)

    prm = jnp.stack([dn_dt_bias, dn_a_log], axis=0)
    prm_row = jnp.zeros((SUBLANES, LANES), F32).at[0:2, HEADS:2 * HEADS].set(prm)
    prm_col = jnp.zeros((SUBLANES, LANES), F32).at[HEADS:2 * HEADS, 0:2].set(prm.T)
    out_a, w_a, w_g, w_d = _gated_deltanet(p, small, small_t3, dn_conv_w, prm_row, prm_col, dn_norm_g,
                                           weights["ffn_up"], weights["ffn_down"], layer, batch, seq)
    out_b, out_c, out_d = _banded_and_gating_mixers(
        p, _relbias_table(rel_bias), sinks, jnp.asarray(_alibi_table()), sgu_norm_g, sgu_w,
        _pad_last(sgu_b.T, LANES), batch, seq)

    x2 = _out_projection((out_a, out_b, out_c, out_d), weights["out"], x2, mod, mix_post_g, layer, seq)
    return _conv_ffn(x2, mod, ffn_pre_g, ffn_post_g, w_a, w_g, w_d,
                     weights["ffn_conv_w"], weights["ffn_conv_b"], layer, seq)


def _prepare_weights(w_in, w_out, ffn_w_up, ffn_conv_w, ffn_conv_b, ffn_w_down):
    small_lo = N_LO_TILES * D_GROUP
    w_in_t = jnp.swapaxes(w_in, 1, 2)
    w_small = w_in_t[:, small_lo:small_lo + N_SMALL, :]
    return {
        "in_full": w_in_t,
        "in_tail": w_in_t,
        "in_small": jnp.pad(w_small, ((0, 0), (0, LANES - N_SMALL), (0, 0))).astype(BF16),
        "out": w_out,
        "ffn_up": ffn_w_up,
        "ffn_down": ffn_w_down,
        "ffn_conv_w": _pad_last(ffn_conv_w, D_FF_PAD),
        "ffn_conv_b": _pad_last(ffn_conv_b[:, None, :], D_FF_PAD),
    }


def kernel(x, c, ada_w, ada_b, mix_pre_g, mix_post_g, w_in, dn_conv_w, dn_a_log, dn_dt_bias, dn_norm_g,
           rel_bias, sinks, sgu_norm_g, sgu_w, sgu_b, w_out, ffn_pre_g, ffn_post_g, ffn_w_up, ffn_conv_w,
           ffn_conv_b, ffn_w_down):
    batch, seq, d = x.shape
    depth = ada_w.shape[0]
    assert d == D_MODEL and seq % ROWS_PROJ == 0 and batch <= SUBLANES
    c_pad = jnp.pad(c, ((0, SUBLANES - batch), (0, 0)))
    mod_all = _modulation(c_pad, ada_w, ada_b)[:, :batch].reshape(depth, batch, 6, d)
    weights = _prepare_weights(w_in, w_out, ffn_w_up, ffn_conv_w, ffn_conv_b, ffn_w_down)
    x2 = x.reshape(batch * seq, d)
    for l in range(depth):
        x2 = _layer(x2, mod_all[l], l, batch, seq, weights, mix_pre_g[l], mix_post_g[l], dn_conv_w[l],
                    dn_a_log[l], dn_dt_bias[l], dn_norm_g[l], rel_bias[l], sinks[l], sgu_norm_g[l],
                    sgu_w[l], sgu_b[l], ffn_pre_g[l], ffn_post_g[l])
    return x2.reshape(batch, seq, d)
```
